```python
import jax, jax.numpy as jnp
from jax import lax
import numpy as np

D_MODEL = 1024
BATCH = 8
SEQ = 2048
DEPTH = 2
DEC_BATCH = 128
DEC_SEQ = 1
PAST_LEN = 16384
PAGE_SIZE = 128

D_MIX = 2 * D_MODEL
CONV_WIDTH = 4
CHUNK = 64
NORM_EPS = 1e-6
SSD_WIDTH = D_MIX // 2
SSD_HEAD_DIM = 64
SSD_HEADS = SSD_WIDTH // SSD_HEAD_DIM
SSD_GROUPS = 2
SSD_HPG = SSD_HEADS // SSD_GROUPS
SSD_STATE = 128
SSD_CONV_DIM = SSD_WIDTH + 2 * SSD_GROUPS * SSD_STATE
RET_WIDTH = D_MIX // 4
RET_HEADS = 4
RET_HEAD_DIM = RET_WIDTH // RET_HEADS
ROPE_BASE = 10000.0
LRU_WIDTH = D_MIX // 4
LRU_BLOCKS = 8
LRU_BLOCK_DIM = LRU_WIDTH // LRU_BLOCKS
LRU_C = 8.0
IN_WIDTH = SSD_WIDTH + SSD_CONV_DIM + SSD_HEADS + 4 * RET_WIDTH + 2 * LRU_WIDTH
PEER_KEYS = 128
PEER_EXPERTS = PEER_KEYS * PEER_KEYS
PEER_HEADS = 8
PEER_TOPK = 16
PEER_QUERY_DIM = 256
PEER_HALF = PEER_QUERY_DIM // 2
PEER_TOKEN_BLOCK = 256

kernel_name = 'hymba_ssd_retnet_rglru_peer_step'


def rmsnorm(x, w=None):
    xf = x.astype(jnp.float32)
    xf = xf * lax.rsqrt(jnp.mean(xf * xf, axis=-1, keepdims=True) + NORM_EPS)
    if w is not None:
        xf = xf * w.astype(jnp.float32)
    return xf


def causal_conv(x, buf, w, b):
    l = x.shape[1]
    xp = jnp.concatenate([buf.astype(x.dtype), x], axis=1)
    y = b.astype(x.dtype)
    for tap in range(CONV_WIDTH):
        y = y + xp[:, tap:tap + l] * w[tap]
    return y, xp[:, l:]


def rotary(x, pos):
    half = x.shape[-1] // 2
    inv = ROPE_BASE ** (-jnp.arange(half, dtype=jnp.float32) / half)
    ang = pos.astype(jnp.float32)[:, None] * inv[None, :]
    cos = jnp.cos(ang)[None, :, None, :]
    sin = jnp.sin(ang)[None, :, None, :]
    x1 = x[..., :half].astype(jnp.float32)
    x2 = x[..., half:].astype(jnp.float32)
    return jnp.concatenate([x1 * cos - x2 * sin, x1 * sin + x2 * cos], axis=-1).astype(x.dtype)


def pad_seq(a, pad):
    return jnp.pad(a, [(0, 0), (0, pad)] + [(0, 0)] * (a.ndim - 2))


def chunked_scan(q, k, v, log_a, s0):
    b, l = q.shape[:2]
    c = min(CHUNK, l)
    nc = -(-l // c)
    pad = nc * c - l
    q, k, v, log_a = (pad_seq(a, pad) for a in (q, k, v, log_a))
    split = lambda a: a.reshape((b, nc, c) + a.shape[2:])
    q, k, v = split(q), split(k), split(v)
    cum = jnp.cumsum(split(log_a.astype(jnp.float32)), axis=2)
    causal = jnp.tril(jnp.ones((c, c), bool))[None, None, :, :, None, None]
    seg = cum[:, :, :, None] - cum[:, :, None, :]
    decay = jnp.exp(jnp.where(causal, seg, -jnp.inf))
    scores = jnp.einsum('bzign,bzjgn->bzijg', q, k)
    y_intra = jnp.einsum('bzijgr,bzjgrp->bzigrp', scores[..., None] * decay, v)
    to_end = jnp.exp(cum[:, :, -1:] - cum)
    chunk_states = jnp.einsum('bzjgn,bzjgrp->bzgrnp', k, v * to_end[..., None])
    chunk_decay = jnp.exp(cum[:, :, -1])

    def carry(s, inp):
        cs, cd = inp
        return s * cd[..., None, None] + cs, s

    s_final, s_in = lax.scan(carry, s0.astype(jnp.float32),
                             (jnp.moveaxis(chunk_states, 1, 0), jnp.moveaxis(chunk_decay, 1, 0)))
    s_in = jnp.moveaxis(s_in, 0, 1)
    y_inter = jnp.einsum('bzign,bzgrnp->bzigrp', q, s_in) * jnp.exp(cum)[..., None]
    y = (y_intra + y_inter).reshape((b, nc * c) + v.shape[3:])[:, :l]
    return y, s_final


def ssd_group(z, xbc, dt, s0, buf, conv_w, conv_b, dt_bias, a_log, d_skip, norm_w):
    b, l, _ = z.shape
    xbc, new_buf = causal_conv(xbc, buf, conv_w, conv_b)
    xbc = jax.nn.silu(xbc)
    xs = xbc[..., :SSD_WIDTH].reshape(b, l, SSD_GROUPS, SSD_HPG, SSD_HEAD_DIM)
    bm = xbc[..., SSD_WIDTH:SSD_WIDTH + SSD_GROUPS * SSD_STATE].reshape(b, l, SSD_GROUPS, SSD_STATE)
    cm = xbc[..., SSD_WIDTH + SSD_GROUPS * SSD_STATE:].reshape(b, l, SSD_GROUPS, SSD_STATE)
    dt = jax.nn.softplus(dt.astype(jnp.float32) + dt_bias.astype(jnp.float32)).reshape(b, l, SSD_GROUPS, SSD_HPG)
    log_a = -dt * jnp.exp(a_log.astype(jnp.float32)).reshape(SSD_GROUPS, SSD_HPG)
    y, s = chunked_scan(cm, bm, xs * dt[..., None], log_a,
                        s0.reshape(b, SSD_GROUPS, SSD_HPG, SSD_STATE, SSD_HEAD_DIM))
    y = y + d_skip.astype(jnp.float32).reshape(SSD_GROUPS, SSD_HPG, 1) * xs
    y = rmsnorm(y.reshape(b, l, SSD_WIDTH) * jax.nn.silu(z.astype(jnp.float32)), norm_w)
    return y, s.reshape(b, SSD_HEADS, SSD_STATE, SSD_HEAD_DIM), new_buf


def retention_group(q, k, v, g, s0, pos):
    b, l, _ = q.shape
    q = rotary(q.reshape(b, l, RET_HEADS, RET_HEAD_DIM), pos)
    k = rotary(k.reshape(b, l, RET_HEADS, RET_HEAD_DIM), pos) * (RET_HEAD_DIM ** -0.5)
    v = v.reshape(b, l, RET_HEADS, 1, RET_HEAD_DIM)
    log_gamma = jnp.log(1.0 - jnp.exp2(-5.0 - jnp.arange(RET_HEADS, dtype=jnp.float32)))
    log_a = jnp.broadcast_to(log_gamma[:, None], (b, l, RET_HEADS, 1))
    y, s = chunked_scan(q, k, v, log_a, s0.reshape(b, RET_HEADS, 1, RET_HEAD_DIM, RET_HEAD_DIM))
    y = rmsnorm(y[:, :, :, 0]).reshape(b, l, RET_WIDTH) * jax.nn.silu(g.astype(jnp.float32))
    return y, s.reshape(b, RET_HEADS, RET_HEAD_DIM, RET_HEAD_DIM)


def lru_combine(left, right):
    a_l, b_l = left
    a_r, b_r = right
    return a_l * a_r, a_r * b_l + b_r


def rglru_group(gate, xr, h0, buf, conv_w, conv_b, wa, ba, wi, bi, lam):
    b, l, _ = xr.shape
    xr, new_buf = causal_conv(xr, buf, conv_w, conv_b)
    xb = xr.reshape(b, l, LRU_BLOCKS, LRU_BLOCK_DIM)
    r = jax.nn.sigmoid(jnp.einsum('blhi,hij->blhj', xb, wa).reshape(b, l, LRU_WIDTH) + ba).astype(jnp.float32)
    i = jax.nn.sigmoid(jnp.einsum('blhi,hij->blhj', xb, wi).reshape(b, l, LRU_WIDTH) + bi).astype(jnp.float32)
    log_a = -LRU_C * r * jax.nn.softplus(-lam.astype(jnp.float32))
    u = jnp.sqrt(-jnp.expm1(2.0 * log_a)) * (i * xr.astype(jnp.float32))
    a_cum, h = lax.associative_scan(lru_combine, (jnp.exp(log_a), u), axis=1)
    h = h + a_cum * h0.astype(jnp.float32)[:, None]
    y = h * jax.nn.gelu(gate.astype(jnp.float32), approximate=False)
    return y, h[:, -1], new_buf


def peer_ffn(x, wq, keys, u_tab, v_tab):
    b, l, d = x.shape
    t = b * l
    tb = min(PEER_TOKEN_BLOCK, t)
    nb = -(-t // tb)
    xt = jnp.pad(x.reshape(t, d), ((0, nb * tb - t), (0, 0))).reshape(nb, tb, d)

    def block(xb):
        q = (xb @ wq).reshape(tb, PEER_HEADS, 2, PEER_HALF)
        s = jnp.einsum('thcn,hckn->thck', q, keys).astype(jnp.float32)
        s_top, i_top = lax.top_k(s, PEER_TOPK)
        cand = s_top[:, :, 0, :, None] + s_top[:, :, 1, None, :]
        cidx = i_top[:, :, 0, :, None] * PEER_KEYS + i_top[:, :, 1, None, :]
        sc, sel = lax.top_k(cand.reshape(tb, PEER_HEADS, PEER_TOPK * PEER_TOPK), PEER_TOPK)
        idx = jnp.take_along_axis(cidx.reshape(tb, PEER_HEADS, PEER_TOPK * PEER_TOPK), sel, axis=-1)
        gate = jax.nn.softmax(sc, axis=-1).reshape(tb, PEER_HEADS * PEER_TOPK)
        idx = idx.reshape(tb, PEER_HEADS * PEER_TOPK)
        act = jax.nn.gelu(jnp.einsum('td,ted->te', xb, u_tab[idx]).astype(jnp.float32), approximate=False)
        return jnp.einsum('te,ted->td', (gate * act).astype(xb.dtype), v_tab[idx])

    y = lax.map(block, xt).reshape(nb * tb, d)[:t]
    return y.reshape(b, l, d)


def layer(x, pos, state, w):
    (n1, w_in, scw, scb, dtb, alog, dsk, snw, lcw, lcb, wa, ba, wi, bi, lam,
     w_out, n2, wq, keys, u_tab, v_tab) = w
    ssd_s, ssd_buf, ret_s, lru_h, lru_buf = state
    h = rmsnorm(x, n1).astype(x.dtype)
    sizes = [SSD_WIDTH, SSD_CONV_DIM, SSD_HEADS, RET_WIDTH, RET_WIDTH, RET_WIDTH, RET_WIDTH, LRU_WIDTH]
    z, xbc, dt, q, k, vr, g, gate, xr = jnp.split(h @ w_in, np.cumsum(sizes).tolist(), axis=-1)
    y_ssd, ssd_s, ssd_buf = ssd_group(z, xbc, dt, ssd_s, ssd_buf, scw, scb, dtb, alog, dsk, snw)
    y_ret, ret_s = retention_group(q, k, vr, g, ret_s, pos)
    y_lru, lru_h, lru_buf = rglru_group(gate, xr, lru_h, lru_buf, lcw, lcb, wa, ba, wi, bi, lam)
    mix = jnp.concatenate([y_ssd, y_ret, y_lru], axis=-1).astype(x.dtype)
    x = x + mix @ w_out
    x = x + peer_ffn(rmsnorm(x, n2).astype(x.dtype), wq, keys, u_tab, v_tab)
    new_state = tuple(s.astype(x.dtype) for s in (ssd_s, ssd_buf, ret_s, lru_h, lru_buf))
    return x, new_state


def setup_inputs(seed: int = 0) -> dict:
    key = jax.random.key(seed)
    ks = iter(jax.random.split(key, 40))
    nrm = lambda shape, scale: jax.random.normal(next(ks), shape, jnp.float32) * scale
    gain = lambda shape: 1.0 + nrm(shape, 0.01)
    dt0 = jnp.exp(jax.random.uniform(next(ks), (DEPTH, SSD_HEADS), jnp.float32, np.log(1e-3), np.log(1e-1)))
    a0 = jax.random.uniform(next(ks), (DEPTH, LRU_WIDTH), jnp.float32, 0.9, 0.999) ** (1.0 / LRU_C)
    return {
        'x_prompt': nrm((BATCH, SEQ, D_MODEL), 1.0),
        'x_sample': nrm((DEC_BATCH, DEC_SEQ, D_MODEL), 1.0),
        'state_ssd': nrm((DEPTH, DEC_BATCH, SSD_HEADS, SSD_STATE, SSD_HEAD_DIM), 0.5),
        'state_ssd_conv': nrm((DEPTH, DEC_BATCH, CONV_WIDTH - 1, SSD_CONV_DIM), 1.0),
        'state_ret': nrm((DEPTH, DEC_BATCH, RET_HEADS, RET_HEAD_DIM, RET_HEAD_DIM), 1.0),
        'state_lru': nrm((DEPTH, DEC_BATCH, LRU_WIDTH), 1.0),
        'state_lru_conv': nrm((DEPTH, DEC_BATCH, CONV_WIDTH - 1, LRU_WIDTH), 1.0),
        'norm1_w': gain((DEPTH, D_MODEL)),
        'w_in': nrm((DEPTH, D_MODEL, IN_WIDTH), D_MODEL ** -0.5),
        'ssd_conv_w': nrm((DEPTH, CONV_WIDTH, SSD_CONV_DIM), CONV_WIDTH ** -0.5),
        'ssd_conv_b': nrm((DEPTH, SSD_CONV_DIM), 0.01),
        'ssd_dt_bias': dt0 + jnp.log(-jnp.expm1(-dt0)),
        'ssd_a_log': jnp.log(jax.random.uniform(next(ks), (DEPTH, SSD_HEADS), jnp.float32, 1.0, 16.0)),
        'ssd_d': gain((DEPTH, SSD_HEADS)),
        'ssd_norm_w': gain((DEPTH, SSD_WIDTH)),
        'lru_conv_w': nrm((DEPTH, CONV_WIDTH, LRU_WIDTH), CONV_WIDTH ** -0.5),
        'lru_conv_b': nrm((DEPTH, LRU_WIDTH), 0.01),
        'lru_wa': nrm((DEPTH, LRU_BLOCKS, LRU_BLOCK_DIM, LRU_BLOCK_DIM), LRU_BLOCK_DIM ** -0.5),
        'lru_ba': nrm((DEPTH, LRU_WIDTH), 0.01),
        'lru_wi': nrm((DEPTH, LRU_BLOCKS, LRU_BLOCK_DIM, LRU_BLOCK_DIM), LRU_BLOCK_DIM ** -0.5),
        'lru_bi': nrm((DEPTH, LRU_WIDTH), 0.01),
        'lru_lambda': jnp.log(a0) - jnp.log1p(-a0),
        'w_out': nrm((DEPTH, D_MIX, D_MODEL), D_MIX ** -0.5),
        'norm2_w': gain((DEPTH, D_MODEL)),
        'peer_wq': nrm((DEPTH, D_MODEL, PEER_HEADS * PEER_QUERY_DIM), D_MODEL ** -0.5),
        'peer_keys': nrm((DEPTH, PEER_HEADS, 2, PEER_KEYS, PEER_HALF), PEER_HALF ** -0.5),
        'peer_u': nrm((DEPTH, PEER_EXPERTS, D_MODEL), D_MODEL ** -0.5),
        'peer_v': nrm((DEPTH, PEER_EXPERTS, D_MODEL), PEER_HEADS ** -0.5),
        'final_norm_w': gain((D_MODEL,)),
    }


def reference(x_prompt, x_sample, state_ssd, state_ssd_conv, state_ret, state_lru, state_lru_conv,
              norm1_w, w_in, ssd_conv_w, ssd_conv_b, ssd_dt_bias, ssd_a_log, ssd_d, ssd_norm_w,
              lru_conv_w, lru_conv_b, lru_wa, lru_ba, lru_wi, lru_bi, lru_lambda, w_out,
              norm2_w, peer_wq, peer_keys, peer_u, peer_v, final_norm_w):
    bp, lp, _ = x_prompt.shape
    act_dtype = x_prompt.dtype
    pos_p = jnp.arange(lp)
    pos_s = PAST_LEN + jnp.arange(x_sample.shape[1])
    zero_state = (jnp.zeros((bp, SSD_HEADS, SSD_STATE, SSD_HEAD_DIM), act_dtype),
                  jnp.zeros((bp, CONV_WIDTH - 1, SSD_CONV_DIM), act_dtype),
                  jnp.zeros((bp, RET_HEADS, RET_HEAD_DIM, RET_HEAD_DIM), act_dtype),
                  jnp.zeros((bp, LRU_WIDTH), act_dtype),
                  jnp.zeros((bp, CONV_WIDTH - 1, LRU_WIDTH), act_dtype))
    hp, hs = x_prompt, x_sample
    new_p, new_s = [], []
    for i in range(DEPTH):
        w = (norm1_w[i], w_in[i], ssd_conv_w[i], ssd_conv_b[i], ssd_dt_bias[i], ssd_a_log[i], ssd_d[i],
             ssd_norm_w[i], lru_conv_w[i], lru_conv_b[i], lru_wa[i], lru_ba[i], lru_wi[i], lru_bi[i],
             lru_lambda[i], w_out[i], norm2_w[i], peer_wq[i], peer_keys[i], peer_u[i], peer_v[i])
        hp, st_p = layer(hp, pos_p, zero_state, w)
        hs, st_s = layer(hs, pos_s, (state_ssd[i], state_ssd_conv[i], state_ret[i], state_lru[i],
                                     state_lru_conv[i]), w)
        new_p.append(st_p)
        new_s.append(st_s)
    y_prompt = rmsnorm(hp, final_norm_w).astype(act_dtype)
    y_sample = rmsnorm(hs, final_norm_w).astype(x_sample.dtype)
    stk = lambda states, j: jnp.stack([s[j] for s in states])
    return (y_prompt, y_sample,
            stk(new_p, 0), stk(new_p, 1), stk(new_p, 2), stk(new_p, 3), stk(new_p, 4),
            stk(new_s, 0), stk(new_s, 1), stk(new_s, 2), stk(new_s, 3), stk(new_s, 4))
```

```python
import functools

import numpy as np
import jax
import jax.numpy as jnp
from jax import lax
from jax.experimental import pallas as pl
from jax.experimental.pallas import tpu as pltpu

F32 = jnp.float32
BF16 = jnp.bfloat16

D_MODEL = 1024
PAST_LEN = 16384
D_MIX = 2 * D_MODEL
CONV_WIDTH = 4
NORM_EPS = 1e-6
SSD_WIDTH = D_MIX // 2
SSD_HEAD_DIM = 64
SSD_HEADS = SSD_WIDTH // SSD_HEAD_DIM
SSD_GROUPS = 2
SSD_HPG = SSD_HEADS // SSD_GROUPS
SSD_STATE = 128
SSD_CONV_DIM = SSD_WIDTH + 2 * SSD_GROUPS * SSD_STATE
RET_WIDTH = D_MIX // 4
RET_HEADS = 4
RET_HEAD_DIM = RET_WIDTH // RET_HEADS
ROPE_BASE = 10000.0
LRU_WIDTH = D_MIX // 4
LRU_BLOCKS = 8
LRU_BLOCK_DIM = LRU_WIDTH // LRU_BLOCKS
LRU_C = 8.0
PEER_KEYS = 128
PEER_EXPERTS = PEER_KEYS * PEER_KEYS
PEER_HEADS = 8
PEER_TOPK = 16
PEER_QUERY_DIM = 256
PEER_HALF = PEER_QUERY_DIM // 2

LANES = 128
SUBLANES = 8
VMEM_LIMIT = 56 * 1024 * 1024

COL_Z = 0
COL_XBC = COL_Z + SSD_WIDTH
COL_DT16 = COL_XBC + SSD_CONV_DIM
COL_DTX = COL_DT16 + LANES
COL_Q = COL_DTX + SSD_WIDTH
COL_K = COL_Q + RET_WIDTH
COL_V = COL_K + RET_WIDTH
COL_G = COL_V + RET_WIDTH
COL_GATE = COL_G + RET_WIDTH
COL_XR = COL_GATE + LRU_WIDTH
PROJ_WIDTH = COL_XR + LRU_WIDTH

SCAN_CHUNK = 128
SAMPLE_TILE = SUBLANES
SAMPLE_HEAD_BLOCK = 4
EXPERT_TILE = 512
LOG_GAMMA = [float(np.log(1.0 - 2.0 ** (-5.0 - h))) for h in range(RET_HEADS)]

_NT = (((1,), (1,)), ((), ()))


def _rms(x):
    return x * lax.rsqrt(jnp.mean(x * x, axis=-1, keepdims=True) + NORM_EPS)


def _softplus(x):
    return jnp.maximum(x, 0.0) + jnp.log1p(jnp.exp(-jnp.abs(x)))


def _silu(x):
    return x * jax.nn.sigmoid(x)


def _gelu(x):
    return 0.5 * x * (1.0 + lax.erf(x * np.float32(np.sqrt(0.5))))


def _neg_expm1(x):
    return -jnp.tanh(0.5 * x) * (jnp.exp(x) + 1.0)


def _dot(a, b):
    return jnp.dot(a, b, preferred_element_type=F32)


def _dot_nt(a, b):
    return lax.dot_general(a, b, _NT, preferred_element_type=F32)


def _dot_exact(a, b):
    return jnp.dot(a, b, preferred_element_type=F32, precision=lax.Precision.HIGHEST)


def _params(semantics):
    return pltpu.CompilerParams(dimension_semantics=semantics, vmem_limit_bytes=VMEM_LIMIT)


def _const_spec(shape):
    zeros = (0,) * len(shape)
    return pl.BlockSpec(shape, lambda *_: zeros)


def _inproj_kernel(x_ref, n_ref, w_ref, o_ref):
    h = (_rms(x_ref[...]) * n_ref[...]).astype(BF16)
    step = 8 * LANES
    for j in range(0, PROJ_WIDTH, step):
        hi = min(j + step, PROJ_WIDTH)
        o_ref[:, j:hi] = _dot(h, w_ref[:, j:hi])


def _inproj(x, n1, w_in_p, tm):
    t = x.shape[0]
    return pl.pallas_call(
        _inproj_kernel,
        grid=(t // tm,),
        in_specs=[pl.BlockSpec((tm, D_MODEL), lambda i: (i, 0)),
                  _const_spec((1, D_MODEL)),
                  pl.BlockSpec((D_MODEL, PROJ_WIDTH), lambda i: (0, 0), pipeline_mode=pl.Buffered(1))],
        out_specs=pl.BlockSpec((tm, PROJ_WIDTH), lambda i: (i, 0)),
        out_shape=jax.ShapeDtypeStruct((t, PROJ_WIDTH), F32),
        compiler_params=_params(("arbitrary",)),
        name="inproj",
    )(x, n1, w_in_p)


def _prompt_mixer_kernel(proj_ref, x_ref, cos_ref, sin_ref, scw_ref, scb_ref, dtb16_ref, alog16_ref,
                         dtbx_ref, alogx_ref, dskx_ref, snw_ref, lcw_ref, lcb_ref, wa_ref, ba_ref,
                         wi_ref, bi_ref, lam_ref, wout_ref,
                         xo_ref, ssd_ref, sbuf_ref, ret_ref, lruh_ref, lbuf_ref,
                         xps_ref, xpl_ref, mix_ref):
    c = SCAN_CHUNK
    step = pl.program_id(1)
    pad = SUBLANES
    keep = CONV_WIDTH - 1

    @pl.when(step == 0)
    def _():
        ssd_ref[...] = jnp.zeros_like(ssd_ref)
        ret_ref[...] = jnp.zeros_like(ret_ref)
        lruh_ref[...] = jnp.zeros_like(lruh_ref)
        xps_ref[0:pad, :] = jnp.zeros((pad, SSD_CONV_DIM), F32)
        xpl_ref[0:pad, :] = jnp.zeros((pad, LRU_WIDTH), F32)

    def conv(xp_ref, x_new, w_ref, b_ref, buf_out_ref):
        xp_ref[pad:pad + c, :] = x_new
        y = b_ref[...] + xp_ref[pad - keep:pad - keep + c, :] * w_ref[0:1, :]
        for tap in range(1, CONV_WIDTH):
            y = y + xp_ref[pad - keep + tap:pad - keep + tap + c, :] * w_ref[tap:tap + 1, :]
        tail = xp_ref[pad + c - keep:pad + c, :]
        xp_ref[pad - keep:pad, :] = tail
        buf_out_ref[0] = tail
        return y

    row = lax.broadcasted_iota(jnp.int32, (c, c), 0)
    col = lax.broadcasted_iota(jnp.int32, (c, c), 1)
    causal = row >= col
    tri = causal.astype(F32)
    rowf = lax.broadcasted_iota(jnp.int32, (c, LANES), 0).astype(F32)

    xbc = _silu(conv(xps_ref, proj_ref[:, COL_XBC:COL_XBC + SSD_CONV_DIM], scw_ref, scb_ref, sbuf_ref))
    xs = xbc[:, :SSD_WIDTH]
    bm = xbc[:, SSD_WIDTH:SSD_WIDTH + SSD_GROUPS * SSD_STATE]
    cm = xbc[:, SSD_WIDTH + SSD_GROUPS * SSD_STATE:]
    dtx = _softplus(proj_ref[:, COL_DTX:COL_DTX + SSD_WIDTH] + dtbx_ref[...])
    cumx = _dot_exact(tri, -dtx * jnp.exp(alogx_ref[...]))
    dt16 = _softplus(proj_ref[:, COL_DT16:COL_DT16 + LANES] + dtb16_ref[...])
    cum16 = _dot_exact(tri, -dt16 * jnp.exp(alog16_ref[...]))
    cum16_t = cum16.T
    v = xs * dtx
    v_bf = v.astype(BF16)
    cum_last = cumx[c - 1:c, :]
    vte_bf = (v * jnp.exp(cum_last - cumx)).astype(BF16)
    ecum = jnp.exp(cumx)
    chunk_decay = jnp.exp(cum_last)
    for g in range(SSD_GROUPS):
        bmg = bm[:, g * SSD_STATE:(g + 1) * SSD_STATE]
        cmg_bf = cm[:, g * SSD_STATE:(g + 1) * SSD_STATE].astype(BF16)
        scores = _dot_nt(cmg_bf, bmg.astype(BF16))
        bmg_t_bf = bmg.T.astype(BF16)
        for r in range(SSD_HPG):
            h = g * SSD_HPG + r
            hs = slice(h * SSD_HEAD_DIM, (h + 1) * SSD_HEAD_DIM)
            seg = cum16[:, h:h + 1] - cum16_t[h:h + 1, :]
            decay = jnp.exp(jnp.where(causal, seg, -jnp.inf))
            y_intra = _dot((scores * decay).astype(BF16), v_bf[:, hs])
            s_in = ssd_ref[0, h]
            y_inter = _dot(cmg_bf, s_in.astype(BF16)) * ecum[:, hs]
            ssd_ref[0, h] = s_in * chunk_decay[:, hs] + _dot(bmg_t_bf, vte_bf[:, hs])
            mix_ref[:, hs] = y_intra + y_inter
    y = mix_ref[:, :SSD_WIDTH] + dskx_ref[...] * xs
    y = y * _silu(proj_ref[:, COL_Z:COL_Z + SSD_WIDTH])
    mix_ref[:, :SSD_WIDTH] = _rms(y) * snw_ref[...]

    cosf = cos_ref[...]
    sinf = sin_ref[...]
    rowcol = (row - col).astype(F32)
    for h in range(RET_HEADS):
        lg = LOG_GAMMA[h]
        hs = slice(h * RET_HEAD_DIM, (h + 1) * RET_HEAD_DIM)
        qh = proj_ref[:, COL_Q + h * RET_HEAD_DIM:COL_Q + (h + 1) * RET_HEAD_DIM]
        kh = proj_ref[:, COL_K + h * RET_HEAD_DIM:COL_K + (h + 1) * RET_HEAD_DIM]
        vh = proj_ref[:, COL_V + h * RET_HEAD_DIM:COL_V + (h + 1) * RET_HEAD_DIM]
        gh = proj_ref[:, COL_G + h * RET_HEAD_DIM:COL_G + (h + 1) * RET_HEAD_DIM]
        qh = qh * cosf + pltpu.roll(qh, RET_HEAD_DIM // 2, 1) * sinf
        kh = (kh * cosf + pltpu.roll(kh, RET_HEAD_DIM // 2, 1) * sinf) * (RET_HEAD_DIM ** -0.5)
        qh_bf = qh.astype(BF16)
        scores = _dot_nt(qh_bf, kh.astype(BF16))
        decay = jnp.exp(jnp.where(causal, rowcol * lg, -jnp.inf))
        y_intra = _dot((scores * decay).astype(BF16), vh.astype(BF16))
        s_in = ret_ref[0, h]
        y_inter = _dot(qh_bf, s_in.astype(BF16)) * jnp.exp((rowf + 1.0) * lg)
        v_to_end = (vh * jnp.exp((c - 1.0 - rowf) * lg)).astype(BF16)
        ret_ref[0, h] = s_in * float(np.exp(c * lg)) + _dot(kh.T.astype(BF16), v_to_end)
        mix_ref[:, SSD_WIDTH + h * RET_HEAD_DIM:SSD_WIDTH + (h + 1) * RET_HEAD_DIM] = (
            _rms(y_intra + y_inter) * _silu(gh))

    xc = conv(xpl_ref, proj_ref[:, COL_XR:COL_XR + LRU_WIDTH], lcw_ref, lcb_ref, lbuf_ref)
    xc_bf = xc.astype(BF16)
    r_gate = jax.nn.sigmoid(_dot(xc_bf, wa_ref[...]) + ba_ref[...])
    i_gate = jax.nn.sigmoid(_dot(xc_bf, wi_ref[...]) + bi_ref[...])
    log_a = -LRU_C * r_gate * _softplus(-lam_ref[...])
    a_run = jnp.exp(log_a)
    b_run = jnp.sqrt(_neg_expm1(2.0 * log_a)) * (i_gate * xc)
    rowl = lax.broadcasted_iota(jnp.int32, (c, LRU_WIDTH), 0)
    shift = 1
    while shift < c:
        valid = rowl >= shift
        a_prev = pltpu.roll(a_run, shift, 0)
        b_prev = pltpu.roll(b_run, shift, 0)
        b_run = jnp.where(valid, a_run * b_prev + b_run, b_run)
        a_run = jnp.where(valid, a_run * a_prev, a_run)
        shift *= 2
    h_all = b_run + a_run * lruh_ref[0]
    lruh_ref[0] = h_all[c - 1:c, :]
    mix_ref[:, SSD_WIDTH + RET_WIDTH:] = h_all * _gelu(proj_ref[:, COL_GATE:COL_GATE + LRU_WIDTH])

    xo_ref[...] = x_ref[...] + _dot(mix_ref[...].astype(BF16), wout_ref[...])


def _prompt_mixer(proj, x_all, cosf, sinf, wts, batch, seq):
    c = SCAN_CHUNK
    nc = seq // c
    t = x_all.shape[0]
    row_spec = lambda width: pl.BlockSpec((c, width), lambda b, s: (b * nc + s, 0))
    state_spec = lambda shape: pl.BlockSpec((1,) + shape, lambda b, s: (b,) + (0,) * len(shape))
    in_specs = [row_spec(PROJ_WIDTH), row_spec(D_MODEL),
                pl.BlockSpec((c, LANES), lambda b, s: (s, 0)),
                pl.BlockSpec((c, LANES), lambda b, s: (s, 0))]
    in_specs += [_const_spec(w.shape) for w in wts]
    out_shapes = (jax.ShapeDtypeStruct((t, D_MODEL), F32),
                  jax.ShapeDtypeStruct((batch, SSD_HEADS, SSD_STATE, SSD_HEAD_DIM), F32),
                  jax.ShapeDtypeStruct((batch, CONV_WIDTH - 1, SSD_CONV_DIM), F32),
                  jax.ShapeDtypeStruct((batch, RET_HEADS, RET_HEAD_DIM, RET_HEAD_DIM), F32),
                  jax.ShapeDtypeStruct((batch, 1, LRU_WIDTH), F32),
                  jax.ShapeDtypeStruct((batch, CONV_WIDTH - 1, LRU_WIDTH), F32))
    out_specs = (row_spec(D_MODEL),
                 state_spec((SSD_HEADS, SSD_STATE, SSD_HEAD_DIM)),
                 state_spec((CONV_WIDTH - 1, SSD_CONV_DIM)),
                 state_spec((RET_HEADS, RET_HEAD_DIM, RET_HEAD_DIM)),
                 state_spec((1, LRU_WIDTH)),
                 state_spec((CONV_WIDTH - 1, LRU_WIDTH)))
    return pl.pallas_call(
        _prompt_mixer_kernel,
        grid=(batch, nc),
        in_specs=in_specs,
        out_specs=out_specs,
        out_shape=out_shapes,
        scratch_shapes=[pltpu.VMEM((SUBLANES + c, SSD_CONV_DIM), F32),
                        pltpu.VMEM((SUBLANES + c, LRU_WIDTH), F32),
                        pltpu.VMEM((c, D_MIX), F32)],
        compiler_params=_params(("arbitrary", "arbitrary")),
        name="prompt_mixer",
    )(proj, x_all, cosf, sinf, *wts)


def _sample_mixer_kernel(proj_ref, x_ref, xprev_ref, cos_ref, sin_ref, ssd_in_ref, sbuf_in_ref, ret_in_ref,
                         lruh_in_ref, lbuf_in_ref, scw_ref, scb_ref, dtb16_ref, alog16_ref,
                         dtbx_ref, alogx_ref, dskx_ref, snw_ref, lcw_ref, lcb_ref, wa_ref, ba_ref,
                         wi_ref, bi_ref, lam_ref, wout_ref,
                         xo_ref, ssd_ref, sbuf_ref, ret_ref, lruh_ref, lbuf_ref,
                         vh_ref, ah_ref, bmt_ref, cmt_ref, yh_ref, xs_ref, mix_ref):
    del xprev_ref, dtb16_ref, alog16_ref
    bt = SAMPLE_TILE
    hb = pl.program_id(1)
    nhb = pl.num_programs(1)
    keep = CONV_WIDTH - 1

    def conv(x_new, buf_in_ref, w_ref, b_ref, buf_out_ref):
        y = b_ref[...] + x_new * w_ref[keep:keep + 1, :]
        for tap in range(keep):
            y = y + buf_in_ref[:, tap, :] * w_ref[tap:tap + 1, :]
        for tap in range(keep - 1):
            buf_out_ref[:, tap, :] = buf_in_ref[:, tap + 1, :]
        buf_out_ref[:, keep - 1, :] = x_new
        return y

    @pl.when(hb == 0)
    def _():
        xbc = _silu(conv(proj_ref[:, COL_XBC:COL_XBC + SSD_CONV_DIM], sbuf_in_ref, scw_ref, scb_ref, sbuf_ref))
        xs = xbc[:, :SSD_WIDTH]
        xs_ref[...] = xs
        dtx = _softplus(proj_ref[:, COL_DTX:COL_DTX + SSD_WIDTH] + dtbx_ref[...])
        a = jnp.exp(-dtx * jnp.exp(alogx_ref[...]))
        v = xs * dtx
        for h in range(SSD_HEADS):
            hs = slice(h * SSD_HEAD_DIM, (h + 1) * SSD_HEAD_DIM)
            vh_ref[h] = v[:, hs]
            ah_ref[h] = a[:, hs]
        for g in range(SSD_GROUPS):
            bmt_ref[g] = xbc[:, SSD_WIDTH + g * SSD_STATE:SSD_WIDTH + (g + 1) * SSD_STATE].T
            off = SSD_WIDTH + SSD_GROUPS * SSD_STATE
            cmt_ref[g] = xbc[:, off + g * SSD_STATE:off + (g + 1) * SSD_STATE].T

    g = (hb * SAMPLE_HEAD_BLOCK) // SSD_HPG
    bmt = bmt_ref[g]
    cmt = cmt_ref[g]
    for r in range(SAMPLE_HEAD_BLOCK):
        h = hb * SAMPLE_HEAD_BLOCK + r
        vh = vh_ref[h]
        ah = ah_ref[h]
        for b in range(bt):
            s_new = ssd_in_ref[b, r] * ah[b:b + 1, :] + bmt[:, b:b + 1] * vh[b:b + 1, :]
            ssd_ref[b, r] = s_new
            yh_ref[h, b:b + 1, :] = jnp.sum(cmt[:, b:b + 1] * s_new, axis=0, keepdims=True)

    @pl.when(hb == nhb - 1)
    def _():
        for h in range(SSD_HEADS):
            mix_ref[:, h * SSD_HEAD_DIM:(h + 1) * SSD_HEAD_DIM] = yh_ref[h]
        y = mix_ref[:, :SSD_WIDTH] + dskx_ref[...] * xs_ref[...]
        y = y * _silu(proj_ref[:, COL_Z:COL_Z + SSD_WIDTH])
        mix_ref[:, :SSD_WIDTH] = _rms(y) * snw_ref[...]

        cosf = cos_ref[...]
        sinf = sin_ref[...]
        for h in range(RET_HEADS):
            lg = LOG_GAMMA[h]
            qh = proj_ref[:, COL_Q + h * RET_HEAD_DIM:COL_Q + (h + 1) * RET_HEAD_DIM]
            kh = proj_ref[:, COL_K + h * RET_HEAD_DIM:COL_K + (h + 1) * RET_HEAD_DIM]
            vh = proj_ref[:, COL_V + h * RET_HEAD_DIM:COL_V + (h + 1) * RET_HEAD_DIM]
            gh = proj_ref[:, COL_G + h * RET_HEAD_DIM:COL_G + (h + 1) * RET_HEAD_DIM]
            qh = qh * cosf + pltpu.roll(qh, RET_HEAD_DIM // 2, 1) * sinf
            kh = (kh * cosf + pltpu.roll(kh, RET_HEAD_DIM // 2, 1) * sinf) * (RET_HEAD_DIM ** -0.5)
            qt = qh.T
            kt = kh.T
            hs = slice(SSD_WIDTH + h * RET_HEAD_DIM, SSD_WIDTH + (h + 1) * RET_HEAD_DIM)
            for b in range(bt):
                s_new = ret_in_ref[b, h] * float(np.exp(lg)) + kt[:, b:b + 1] * vh[b:b + 1, :]
                ret_ref[b, h] = s_new
                mix_ref[b:b + 1, hs] = jnp.sum(qt[:, b:b + 1] * s_new, axis=0, keepdims=True)
            mix_ref[:, hs] = _rms(mix_ref[:, hs]) * _silu(gh)

        xc = conv(proj_ref[:, COL_XR:COL_XR + LRU_WIDTH], lbuf_in_ref, lcw_ref, lcb_ref, lbuf_ref)
        xc_bf = xc.astype(BF16)
        r_gate = jax.nn.sigmoid(_dot(xc_bf, wa_ref[...]) + ba_ref[...])
        i_gate = jax.nn.sigmoid(_dot(xc_bf, wi_ref[...]) + bi_ref[...])
        log_a = -LRU_C * r_gate * _softplus(-lam_ref[...])
        h_new = (jnp.exp(log_a) * lruh_in_ref[...]
                 + jnp.sqrt(_neg_expm1(2.0 * log_a)) * (i_gate * xc))
        lruh_ref[...] = h_new
        mix_ref[:, SSD_WIDTH + RET_WIDTH:] = h_new * _gelu(proj_ref[:, COL_GATE:COL_GATE + LRU_WIDTH])

        xo_ref[...] = x_ref[...] + _dot(mix_ref[...].astype(BF16), wout_ref[...])


def _sample_mixer(proj, x_all, x_mixed, cosf, sinf, states, wts, n_prompt, n_sample):
    bt = SAMPLE_TILE
    hbk = SAMPLE_HEAD_BLOCK
    base = n_prompt // bt
    t = x_all.shape[0]
    ssd, sbuf, ret, lruh, lbuf = states
    row_spec = lambda width: pl.BlockSpec((bt, width), lambda i, j: (base + i, 0))
    ssd_spec = pl.BlockSpec((bt, hbk, SSD_STATE, SSD_HEAD_DIM), lambda i, j: (i, j, 0, 0))
    sbuf_spec = pl.BlockSpec((bt, CONV_WIDTH - 1, SSD_CONV_DIM), lambda i, j: (i, 0, 0))
    ret_spec = pl.BlockSpec((bt, RET_HEADS, RET_HEAD_DIM, RET_HEAD_DIM), lambda i, j: (i, 0, 0, 0))
    lruh_spec = pl.BlockSpec((bt, LRU_WIDTH), lambda i, j: (i, 0))
    lbuf_spec = pl.BlockSpec((bt, CONV_WIDTH - 1, LRU_WIDTH), lambda i, j: (i, 0, 0))
    in_specs = [row_spec(PROJ_WIDTH), row_spec(D_MODEL),
                pl.BlockSpec(memory_space=pl.ANY),
                _const_spec((1, LANES)), _const_spec((1, LANES)),
                ssd_spec, sbuf_spec, ret_spec, lruh_spec, lbuf_spec]
    in_specs += [_const_spec(w.shape) for w in wts]
    out_shapes = (jax.ShapeDtypeStruct((t, D_MODEL), F32),
                  jax.ShapeDtypeStruct(ssd.shape, F32),
                  jax.ShapeDtypeStruct(sbuf.shape, F32),
                  jax.ShapeDtypeStruct(ret.shape, F32),
                  jax.ShapeDtypeStruct(lruh.shape, F32),
                  jax.ShapeDtypeStruct(lbuf.shape, F32))
    out_specs = (row_spec(D_MODEL), ssd_spec, sbuf_spec, ret_spec, lruh_spec, lbuf_spec)
    return pl.pallas_call(
        _sample_mixer_kernel,
        grid=(n_sample // bt, SSD_HEADS // hbk),
        in_specs=in_specs,
        out_specs=out_specs,
        out_shape=out_shapes,
        scratch_shapes=[pltpu.VMEM((SSD_HEADS, bt, SSD_HEAD_DIM), F32),
                        pltpu.VMEM((SSD_HEADS, bt, SSD_HEAD_DIM), F32),
                        pltpu.VMEM((SSD_GROUPS, SSD_STATE, bt), F32),
                        pltpu.VMEM((SSD_GROUPS, SSD_STATE, bt), F32),
                        pltpu.VMEM((SSD_HEADS, bt, SSD_HEAD_DIM), F32),
                        pltpu.VMEM((bt, SSD_WIDTH), F32),
                        pltpu.VMEM((bt, D_MIX), F32)],
        input_output_aliases={2: 0},
        compiler_params=_params(("arbitrary", "arbitrary")),
        name="sample_mixer",
    )(proj, x_all, x_mixed, cosf, sinf, ssd, sbuf, ret, lruh, lbuf, *wts)


def _top_rows(vals, out_ref, k):
    n = vals.shape[0]
    rows = lax.broadcasted_iota(jnp.int32, vals.shape, 0).astype(F32)

    def body(i, cur):
        m = jnp.max(cur, axis=0, keepdims=True)
        out_ref[pl.ds(i, 1), :] = m
        first = jnp.min(jnp.where(cur == m, rows, float(n)), axis=0, keepdims=True)
        return jnp.where(rows == first, -jnp.inf, cur)

    lax.fori_loop(0, k, body, vals)


def _peer_route_kernel(x_ref, n_ref, wq_ref, keys_ref,
                       xn_ref, s1_ref, e1_ref, s2_ref, e2_ref, tau_ref,
                       top1_ref, top2_ref, ctop_ref):
    k = PEER_TOPK
    xn = (_rms(x_ref[...]) * n_ref[...]).astype(BF16)
    xn_ref[...] = xn
    q = _dot(xn, wq_ref[...]).astype(BF16)
    for h in range(PEER_HEADS):
        s1 = _dot_nt(keys_ref[2 * h], q[:, (2 * h) * PEER_HALF:(2 * h + 1) * PEER_HALF])
        s2 = _dot_nt(keys_ref[2 * h + 1], q[:, (2 * h + 1) * PEER_HALF:(2 * h + 2) * PEER_HALF])
        _top_rows(s1, top1_ref, k)
        _top_rows(s2, top2_ref, k)
        t1 = top1_ref[...]
        t2 = top2_ref[...]
        cand = jnp.concatenate([t1[a:a + 1, :] + t2 for a in range(k)], axis=0)
        _top_rows(cand, ctop_ref, k)
        ctop = ctop_ref[...]
        z = jnp.sum(jnp.exp(ctop - ctop[0:1, :]), axis=0, keepdims=True)
        s1_ref[h] = s1
        s2_ref[h] = s2
        e1_ref[h] = jnp.exp(s1 - t1[0:1, :]) / z
        e2_ref[h] = jnp.exp(s2 - t2[0:1, :])
        tau_ref[h:h + 1, :] = ctop[k - 1:k, :]


def _peer_route(x, n2, wq, keys, tm):
    t = x.shape[0]
    head_spec = pl.BlockSpec((PEER_HEADS, PEER_KEYS, tm), lambda i: (0, 0, i))
    head_shape = jax.ShapeDtypeStruct((PEER_HEADS, PEER_KEYS, t), F32)
    return pl.pallas_call(
        _peer_route_kernel,
        grid=(t // tm,),
        in_specs=[pl.BlockSpec((tm, D_MODEL), lambda i: (i, 0)),
                  _const_spec((1, D_MODEL)),
                  _const_spec(wq.shape),
                  _const_spec(keys.shape)],
        out_specs=(pl.BlockSpec((tm, D_MODEL), lambda i: (i, 0)),
                   head_spec, head_spec, head_spec, head_spec,
                   pl.BlockSpec((PEER_HEADS, tm), lambda i: (0, i))),
        out_shape=(jax.ShapeDtypeStruct((t, D_MODEL), BF16),
                   head_shape, head_shape, head_shape, head_shape,
                   jax.ShapeDtypeStruct((PEER_HEADS, t), F32)),
        scratch_shapes=[pltpu.VMEM((PEER_TOPK, tm), F32),
                        pltpu.VMEM((PEER_TOPK, tm), F32),
                        pltpu.VMEM((PEER_TOPK, tm), F32)],
        compiler_params=_params(("arbitrary",)),
        name="peer_route",
    )(x, n2, wq, keys)


def _peer_mlp_kernel(xn_ref, u_ref, vt_ref, s1_ref, e1_ref, s2_ref, e2_ref, tau_ref, x_ref, fw_ref,
                     o_ref, acc_ref, wt_ref, *, final_norm):
    e = pl.program_id(1)
    rows_per_tile = EXPERT_TILE // PEER_KEYS

    @pl.when(e == 0)
    def _():
        acc_ref[...] = jnp.zeros_like(acc_ref)

    xn = xn_ref[...]
    for j in range(rows_per_tile):
        i1 = e * rows_per_tile + j
        act = _dot_nt(u_ref[j * PEER_KEYS:(j + 1) * PEER_KEYS, :], xn)
        gate = jnp.zeros_like(act)
        for h in range(PEER_HEADS):
            s1_row = s1_ref[h, pl.ds(i1, 1), :]
            e1_row = e1_ref[h, pl.ds(i1, 1), :]
            selected = (s1_row + s2_ref[h]) >= tau_ref[h:h + 1, :]
            gate = gate + jnp.where(selected, e2_ref[h] * e1_row, 0.0)
        wt_ref[j * PEER_KEYS:(j + 1) * PEER_KEYS, :] = (gate * _gelu(act)).astype(BF16)
    acc_ref[...] += _dot(vt_ref[...], wt_ref[...])

    @pl.when(e == pl.num_programs(1) - 1)
    def _():
        y = x_ref[...] + acc_ref[...].T
        if final_norm:
            y = _rms(y) * fw_ref[...]
        o_ref[...] = y


def _peer_mlp(xn, u_bf, vt_bf, route, x, fw, tm, final_norm):
    t = x.shape[0]
    s1, e1, s2, e2, tau = route
    head_spec = pl.BlockSpec((PEER_HEADS, PEER_KEYS, tm), lambda i, e: (0, 0, i))
    return pl.pallas_call(
        functools.partial(_peer_mlp_kernel, final_norm=final_norm),
        grid=(t // tm, PEER_EXPERTS // EXPERT_TILE),
        in_specs=[pl.BlockSpec((tm, D_MODEL), lambda i, e: (i, 0)),
                  pl.BlockSpec((EXPERT_TILE, D_MODEL), lambda i, e: (e, 0)),
                  pl.BlockSpec((D_MODEL, EXPERT_TILE), lambda i, e: (0, e)),
                  head_spec, head_spec, head_spec, head_spec,
                  pl.BlockSpec((PEER_HEADS, tm), lambda i, e: (0, i)),
                  pl.BlockSpec((tm, D_MODEL), lambda i, e: (i, 0)),
                  _const_spec((1, D_MODEL))],
        out_specs=pl.BlockSpec((tm, D_MODEL), lambda i, e: (i, 0)),
        out_shape=jax.ShapeDtypeStruct((t, D_MODEL), F32),
        scratch_shapes=[pltpu.VMEM((D_MODEL, tm), F32),
                        pltpu.VMEM((EXPERT_TILE, tm), BF16)],
        compiler_params=_params(("arbitrary", "arbitrary")),
        name="peer_mlp",
    )(xn, u_bf, vt_bf, s1, e1, s2, e2, tau, x, fw)


def _token_tile(t):
    for tm in (512, 384, 256, 128):
        if t % tm == 0:
            return tm
    raise ValueError(f"token count {t} must be a multiple of {LANES}")


def _rope_tables(pos):
    half = RET_HEAD_DIM // 2
    inv = ROPE_BASE ** (-jnp.arange(half, dtype=F32) / half)
    ang = pos.astype(F32)[:, None] * inv[None, :]
    cos = jnp.cos(ang)
    sin = jnp.sin(ang)
    return jnp.concatenate([cos, cos], axis=1), jnp.concatenate([-sin, sin], axis=1)


def _block_diag(w):
    eye = jnp.eye(LRU_BLOCKS, dtype=w.dtype)
    return jnp.einsum('hij,hg->higj', w, eye).reshape(LRU_WIDTH, LRU_WIDTH)


def kernel(x_prompt, x_sample, state_ssd, state_ssd_conv, state_ret, state_lru, state_lru_conv, norm1_w, w_in, ssd_conv_w, ssd_conv_b, ssd_dt_bias, ssd_a_log, ssd_d, ssd_norm_w, lru_conv_w, lru_conv_b, lru_wa, lru_ba, lru_wi, lru_bi, lru_lambda, w_out, norm2_w, peer_wq, peer_keys, peer_u, peer_v, final_norm_w):
    batch, seq, _ = x_prompt.shape
    n_sample, dec_seq, _ = x_sample.shape
    depth = w_in.shape[0]
    assert dec_seq == 1 and seq % SCAN_CHUNK == 0 and n_sample % SAMPLE_TILE == 0
    n_prompt = batch * seq
    t = n_prompt + n_sample
    tm = _token_tile(t)

    x = jnp.concatenate([x_prompt.reshape(n_prompt, D_MODEL), x_sample.reshape(n_sample, D_MODEL)], axis=0)
    cos_p, sin_p = _rope_tables(jnp.arange(seq))
    cos_s, sin_s = _rope_tables(PAST_LEN + jnp.arange(dec_seq))
    row = lambda a: a.reshape(1, -1).astype(F32)
    per_channel = lambda a: jnp.repeat(a.astype(F32), SSD_HEAD_DIM).reshape(1, SSD_WIDTH)
    per_head = lambda a: jnp.pad(a.astype(F32), (0, LANES - SSD_HEADS)).reshape(1, LANES)
    off = SSD_WIDTH + SSD_CONV_DIM

    new_p, new_s = [], []
    for i in range(depth):
        wi_ = w_in[i]
        dt_cols = wi_[:, off:off + SSD_HEADS]
        w_in_p = jnp.concatenate(
            [wi_[:, :off],
             jnp.pad(dt_cols, ((0, 0), (0, LANES - SSD_HEADS))),
             jnp.repeat(dt_cols, SSD_HEAD_DIM, axis=1),
             wi_[:, off + SSD_HEADS:]], axis=1).astype(BF16)
        wts = (ssd_conv_w[i], row(ssd_conv_b[i]), per_head(ssd_dt_bias[i]), per_head(ssd_a_log[i]),
               per_channel(ssd_dt_bias[i]), per_channel(ssd_a_log[i]), per_channel(ssd_d[i]),
               row(ssd_norm_w[i]), lru_conv_w[i], row(lru_conv_b[i]),
               _block_diag(lru_wa[i]).astype(BF16), row(lru_ba[i]),
               _block_diag(lru_wi[i]).astype(BF16), row(lru_bi[i]), row(lru_lambda[i]),
               w_out[i].astype(BF16))

        proj = _inproj(x, row(norm1_w[i]), w_in_p, tm)
        x_mixed, *st_p = _prompt_mixer(proj, x, cos_p, sin_p, wts, batch, seq)
        st_p[3] = st_p[3].reshape(batch, LRU_WIDTH)
        x, *st_s = _sample_mixer(proj, x, x_mixed, cos_s, sin_s,
                                 (state_ssd[i], state_ssd_conv[i], state_ret[i], state_lru[i],
                                  state_lru_conv[i]), wts, n_prompt, n_sample)
        new_p.append(st_p)
        new_s.append(st_s)

        keys_bf = peer_keys[i].reshape(PEER_HEADS * 2, PEER_KEYS, PEER_HALF).astype(BF16)
        xn, *route = _peer_route(x, row(norm2_w[i]), peer_wq[i].astype(BF16), keys_bf, tm)
        x = _peer_mlp(xn, peer_u[i].astype(BF16), peer_v[i].astype(BF16).T, route, x,
                      row(final_norm_w), tm, final_norm=(i == depth - 1))

    y_prompt = x[:n_prompt].reshape(batch, seq, D_MODEL)
    y_sample = x[n_prompt:].reshape(n_sample, dec_seq, D_MODEL)
    stk = lambda states, j: jnp.stack([s[j] for s in states])
    return (y_prompt, y_sample,
            stk(new_p, 0), stk(new_p, 1), stk(new_p, 2), stk(new_p, 3), stk(new_p, 4),
            stk(new_s, 0), stk(new_s, 1), stk(new_s, 2), stk(new_s, 3), stk(new_s, 4))
```

```python
import functools

import numpy as np
import jax
import jax.numpy as jnp
from jax import lax
from jax.experimental import pallas as pl
from jax.experimental.pallas import tpu as pltpu

F32 = jnp.float32
BF16 = jnp.bfloat16

D_MODEL = 1024
PAST_LEN = 16384
D_MIX = 2 * D_MODEL
CONV_WIDTH = 4
NORM_EPS = 1e-6
SSD_WIDTH = D_MIX // 2
SSD_HEAD_DIM = 64
SSD_HEADS = SSD_WIDTH // SSD_HEAD_DIM
SSD_GROUPS = 2
SSD_HPG = SSD_HEADS // SSD_GROUPS
SSD_STATE = 128
SSD_CONV_DIM = SSD_WIDTH + 2 * SSD_GROUPS * SSD_STATE
RET_WIDTH = D_MIX // 4
RET_HEADS = 4
RET_HEAD_DIM = RET_WIDTH // RET_HEADS
ROPE_BASE = 10000.0
LRU_WIDTH = D_MIX // 4
LRU_BLOCKS = 8
LRU_BLOCK_DIM = LRU_WIDTH // LRU_BLOCKS
LRU_C = 8.0
PEER_KEYS = 128
PEER_EXPERTS = PEER_KEYS * PEER_KEYS
PEER_HEADS = 8
PEER_TOPK = 16
PEER_QUERY_DIM = 256
PEER_HALF = PEER_QUERY_DIM // 2

LANES = 128
SUBLANES = 8
VMEM_LIMIT = 56 * 1024 * 1024

COL_Z = 0
COL_XBC = COL_Z + SSD_WIDTH
COL_DT16 = COL_XBC + SSD_CONV_DIM
COL_DTX = COL_DT16 + LANES
COL_Q = COL_DTX + SSD_WIDTH
COL_K = COL_Q + RET_WIDTH
COL_V = COL_K + RET_WIDTH
COL_G = COL_V + RET_WIDTH
COL_GATE = COL_G + RET_WIDTH
COL_XR = COL_GATE + LRU_WIDTH
PROJ_WIDTH = COL_XR + LRU_WIDTH

SCAN_CHUNK = 128
SAMPLE_TILE = SUBLANES
SAMPLE_HEAD_BLOCK = 4
EXPERT_TILE = 512
LOG_GAMMA = [float(np.log(1.0 - 2.0 ** (-5.0 - h))) for h in range(RET_HEADS)]

_NT = (((1,), (1,)), ((), ()))


def _rms(x):
    return x * lax.rsqrt(jnp.mean(x * x, axis=-1, keepdims=True) + NORM_EPS)


def _softplus(x):
    return jnp.maximum(x, 0.0) + jnp.log1p(jnp.exp(-jnp.abs(x)))


def _silu(x):
    return x * jax.nn.sigmoid(x)


def _gelu(x):
    return 0.5 * x * (1.0 + lax.erf(x * np.float32(np.sqrt(0.5))))


def _neg_expm1(x):
    return -jnp.tanh(0.5 * x) * (jnp.exp(x) + 1.0)


def _dot(a, b):
    return jnp.dot(a, b, preferred_element_type=F32)


def _dot_nt(a, b):
    return lax.dot_general(a, b, _NT, preferred_element_type=F32)


def _dot_exact(a, b):
    return jnp.dot(a, b, preferred_element_type=F32, precision=lax.Precision.HIGHEST)


def _params(semantics):
    return pltpu.CompilerParams(dimension_semantics=semantics, vmem_limit_bytes=VMEM_LIMIT)


def _const_spec(shape):
    zeros = (0,) * len(shape)
    return pl.BlockSpec(shape, lambda *_: zeros)


def _inproj_kernel(x_ref, n_ref, w_ref, o_ref):
    h = (_rms(x_ref[...]) * n_ref[...]).astype(BF16)
    step = 8 * LANES
    for j in range(0, PROJ_WIDTH, step):
        hi = min(j + step, PROJ_WIDTH)
        o_ref[:, j:hi] = _dot(h, w_ref[:, j:hi])


def _inproj(x, n1, w_in_p, tm):
    t = x.shape[0]
    return pl.pallas_call(
        _inproj_kernel,
        grid=(t // tm,),
        in_specs=[pl.BlockSpec((tm, D_MODEL), lambda i: (i, 0)),
                  _const_spec((1, D_MODEL)),
                  pl.BlockSpec((D_MODEL, PROJ_WIDTH), lambda i: (0, 0), pipeline_mode=pl.Buffered(1))],
        out_specs=pl.BlockSpec((tm, PROJ_WIDTH), lambda i: (i, 0)),
        out_shape=jax.ShapeDtypeStruct((t, PROJ_WIDTH), F32),
        compiler_params=_params(("arbitrary",)),
        name="inproj",
    )(x, n1, w_in_p)


def _prompt_mixer_kernel(proj_ref, x_ref, cos_ref, sin_ref, scw_ref, scb_ref, dtb16_ref, alog16_ref,
                         dtbx_ref, alogx_ref, dskx_ref, snw_ref, lcw_ref, lcb_ref, wa_ref, ba_ref,
                         wi_ref, bi_ref, lam_ref, wout_ref,
                         xo_ref, ssd_ref, sbuf_ref, ret_ref, lruh_ref, lbuf_ref,
                         xps_ref, xpl_ref, mix_ref):
    c = SCAN_CHUNK
    step = pl.program_id(1)
    pad = SUBLANES
    keep = CONV_WIDTH - 1

    @pl.when(step == 0)
    def _():
        ssd_ref[...] = jnp.zeros_like(ssd_ref)
        ret_ref[...] = jnp.zeros_like(ret_ref)
        lruh_ref[...] = jnp.zeros_like(lruh_ref)
        xps_ref[0:pad, :] = jnp.zeros((pad, SSD_CONV_DIM), F32)
        xpl_ref[0:pad, :] = jnp.zeros((pad, LRU_WIDTH), F32)

    def conv(xp_ref, x_new, w_ref, b_ref, buf_out_ref):
        xp_ref[pad:pad + c, :] = x_new
        y = b_ref[...] + xp_ref[pad - keep:pad - keep + c, :] * w_ref[0:1, :]
        for tap in range(1, CONV_WIDTH):
            y = y + xp_ref[pad - keep + tap:pad - keep + tap + c, :] * w_ref[tap:tap + 1, :]
        tail = xp_ref[pad + c - keep:pad + c, :]
        xp_ref[pad - keep:pad, :] = tail
        buf_out_ref[0] = tail
        return y

    row = lax.broadcasted_iota(jnp.int32, (c, c), 0)
    col = lax.broadcasted_iota(jnp.int32, (c, c), 1)
    causal = row >= col
    tri = causal.astype(F32)
    rowf = lax.broadcasted_iota(jnp.int32, (c, LANES), 0).astype(F32)

    xbc = _silu(conv(xps_ref, proj_ref[:, COL_XBC:COL_XBC + SSD_CONV_DIM], scw_ref, scb_ref, sbuf_ref))
    xs = xbc[:, :SSD_WIDTH]
    bm = xbc[:, SSD_WIDTH:SSD_WIDTH + SSD_GROUPS * SSD_STATE]
    cm = xbc[:, SSD_WIDTH + SSD_GROUPS * SSD_STATE:]
    dtx = _softplus(proj_ref[:, COL_DTX:COL_DTX + SSD_WIDTH] + dtbx_ref[...])
    cumx = _dot_exact(tri, -dtx * jnp.exp(alogx_ref[...]))
    dt16 = _softplus(proj_ref[:, COL_DT16:COL_DT16 + LANES] + dtb16_ref[...])
    cum16 = _dot_exact(tri, -dt16 * jnp.exp(alog16_ref[...]))
    cum16_t = cum16.T
    v = xs * dtx
    v_bf = v.astype(BF16)
    cum_last = cumx[c - 1:c, :]
    vte_bf = (v * jnp.exp(cum_last - cumx)).astype(BF16)
    ecum = jnp.exp(cumx)
    chunk_decay = jnp.exp(cum_last)
    for g in range(SSD_GROUPS):
        bmg = bm[:, g * SSD_STATE:(g + 1) * SSD_STATE]
        cmg_bf = cm[:, g * SSD_STATE:(g + 1) * SSD_STATE].astype(BF16)
        scores = _dot_nt(cmg_bf, bmg.astype(BF16))
        bmg_t_bf = bmg.T.astype(BF16)
        for r in range(SSD_HPG):
            h = g * SSD_HPG + r
            hs = slice(h * SSD_HEAD_DIM, (h + 1) * SSD_HEAD_DIM)
            seg = cum16[:, h:h + 1] - cum16_t[h:h + 1, :]
            decay = jnp.exp(jnp.where(causal, seg, -jnp.inf))
            y_intra = _dot((scores * decay).astype(BF16), v_bf[:, hs])
            s_in = ssd_ref[0, h]
            y_inter = _dot(cmg_bf, s_in.astype(BF16)) * ecum[:, hs]
            ssd_ref[0, h] = s_in * chunk_decay[:, hs] + _dot(bmg_t_bf, vte_bf[:, hs])
            mix_ref[:, hs] = y_intra + y_inter
    y = mix_ref[:, :SSD_WIDTH] + dskx_ref[...] * xs
    y = y * _silu(proj_ref[:, COL_Z:COL_Z + SSD_WIDTH])
    mix_ref[:, :SSD_WIDTH] = _rms(y) * snw_ref[...]

    cosf = cos_ref[...]
    sinf = sin_ref[...]
    rowcol = (row - col).astype(F32)
    for h in range(RET_HEADS):
        lg = LOG_GAMMA[h]
        hs = slice(h * RET_HEAD_DIM, (h + 1) * RET_HEAD_DIM)
        qh = proj_ref[:, COL_Q + h * RET_HEAD_DIM:COL_Q + (h + 1) * RET_HEAD_DIM]
        kh = proj_ref[:, COL_K + h * RET_HEAD_DIM:COL_K + (h + 1) * RET_HEAD_DIM]
        vh = proj_ref[:, COL_V + h * RET_HEAD_DIM:COL_V + (h + 1) * RET_HEAD_DIM]
        gh = proj_ref[:, COL_G + h * RET_HEAD_DIM:COL_G + (h + 1) * RET_HEAD_DIM]
        qh = qh * cosf + pltpu.roll(qh, RET_HEAD_DIM // 2, 1) * sinf
        kh = (kh * cosf + pltpu.roll(kh, RET_HEAD_DIM // 2, 1) * sinf) * (RET_HEAD_DIM ** -0.5)
        qh_bf = qh.astype(BF16)
        scores = _dot_nt(qh_bf, kh.astype(BF16))
        decay = jnp.exp(jnp.where(causal, rowcol * lg, -jnp.inf))
        y_intra = _dot((scores * decay).astype(BF16), vh.astype(BF16))
        s_in = ret_ref[0, h]
        y_inter = _dot(qh_bf, s_in.astype(BF16)) * jnp.exp((rowf + 1.0) * lg)
        v_to_end = (vh * jnp.exp((c - 1.0 - rowf) * lg)).astype(BF16)
        ret_ref[0, h] = s_in * float(np.exp(c * lg)) + _dot(kh.T.astype(BF16), v_to_end)
        mix_ref[:, SSD_WIDTH + h * RET_HEAD_DIM:SSD_WIDTH + (h + 1) * RET_HEAD_DIM] = (
            _rms(y_intra + y_inter) * _silu(gh))

    xc = conv(xpl_ref, proj_ref[:, COL_XR:COL_XR + LRU_WIDTH], lcw_ref, lcb_ref, lbuf_ref)
    xc_bf = xc.astype(BF16)
    r_gate = jax.nn.sigmoid(_dot(xc_bf, wa_ref[...]) + ba_ref[...])
    i_gate = jax.nn.sigmoid(_dot(xc_bf, wi_ref[...]) + bi_ref[...])
    log_a = -LRU_C * r_gate * _softplus(-lam_ref[...])
    a_run = jnp.exp(log_a)
    b_run = jnp.sqrt(_neg_expm1(2.0 * log_a)) * (i_gate * xc)
    rowl = lax.broadcasted_iota(jnp.int32, (c, LRU_WIDTH), 0)
    shift = 1
    while shift < c:
        valid = rowl >= shift
        a_prev = pltpu.roll(a_run, shift, 0)
        b_prev = pltpu.roll(b_run, shift, 0)
        b_run = jnp.where(valid, a_run * b_prev + b_run, b_run)
        a_run = jnp.where(valid, a_run * a_prev, a_run)
        shift *= 2
    h_all = b_run + a_run * lruh_ref[0]
    lruh_ref[0] = h_all[c - 1:c, :]
    mix_ref[:, SSD_WIDTH + RET_WIDTH:] = h_all * _gelu(proj_ref[:, COL_GATE:COL_GATE + LRU_WIDTH])

    xo_ref[...] = x_ref[...] + _dot(mix_ref[...].astype(BF16), wout_ref[...])


def _prompt_mixer(proj, x_all, cosf, sinf, wts, batch, seq):
    c = SCAN_CHUNK
    nc = seq // c
    t = x_all.shape[0]
    row_spec = lambda width: pl.BlockSpec((c, width), lambda b, s: (b * nc + s, 0))
    state_spec = lambda shape: pl.BlockSpec((1,) + shape, lambda b, s: (b,) + (0,) * len(shape))
    in_specs = [row_spec(PROJ_WIDTH), row_spec(D_MODEL),
                pl.BlockSpec((c, LANES), lambda b, s: (s, 0)),
                pl.BlockSpec((c, LANES), lambda b, s: (s, 0))]
    in_specs += [_const_spec(w.shape) for w in wts]
    out_shapes = (jax.ShapeDtypeStruct((t, D_MODEL), F32),
                  jax.ShapeDtypeStruct((batch, SSD_HEADS, SSD_STATE, SSD_HEAD_DIM), F32),
                  jax.ShapeDtypeStruct((batch, CONV_WIDTH - 1, SSD_CONV_DIM), F32),
                  jax.ShapeDtypeStruct((batch, RET_HEADS, RET_HEAD_DIM, RET_HEAD_DIM), F32),
                  jax.ShapeDtypeStruct((batch, 1, LRU_WIDTH), F32),
                  jax.ShapeDtypeStruct((batch, CONV_WIDTH - 1, LRU_WIDTH), F32))
    out_specs = (row_spec(D_MODEL),
                 state_spec((SSD_HEADS, SSD_STATE, SSD_HEAD_DIM)),
                 state_spec((CONV_WIDTH - 1, SSD_CONV_DIM)),
                 state_spec((RET_HEADS, RET_HEAD_DIM, RET_HEAD_DIM)),
                 state_spec((1, LRU_WIDTH)),
                 state_spec((CONV_WIDTH - 1, LRU_WIDTH)))
    return pl.pallas_call(
        _prompt_mixer_kernel,
        grid=(batch, nc),
        in_specs=in_specs,
        out_specs=out_specs,
        out_shape=out_shapes,
        scratch_shapes=[pltpu.VMEM((SUBLANES + c, SSD_CONV_DIM), F32),
                        pltpu.VMEM((SUBLANES + c, LRU_WIDTH), F32),
                        pltpu.VMEM((c, D_MIX), F32)],
        compiler_params=_params(("arbitrary", "arbitrary")),
        name="prompt_mixer",
    )(proj, x_all, cosf, sinf, *wts)


def _sample_mixer_kernel(proj_ref, x_ref, xprev_ref, cos_ref, sin_ref, ssd_in_ref, sbuf_in_ref, ret_in_ref,
                         lruh_in_ref, lbuf_in_ref, scw_ref, scb_ref, dtb16_ref, alog16_ref,
                         dtbx_ref, alogx_ref, dskx_ref, snw_ref, lcw_ref, lcb_ref, wa_ref, ba_ref,
                         wi_ref, bi_ref, lam_ref, wout_ref,
                         xo_ref, ssd_ref, sbuf_ref, ret_ref, lruh_ref, lbuf_ref,
                         vh_ref, ah_ref, bmt_ref, cmt_ref, yh_ref, xs_ref, mix_ref):
    del xprev_ref, dtb16_ref, alog16_ref
    bt = SAMPLE_TILE
    hb = pl.program_id(1)
    nhb = pl.num_programs(1)
    keep = CONV_WIDTH - 1

    def conv(x_new, buf_in_ref, w_ref, b_ref, buf_out_ref):
        y = b_ref[...] + x_new * w_ref[keep:keep + 1, :]
        for tap in range(keep):
            y = y + buf_in_ref[:, tap, :] * w_ref[tap:tap + 1, :]
        for tap in range(keep - 1):
            buf_out_ref[:, tap, :] = buf_in_ref[:, tap + 1, :]
        buf_out_ref[:, keep - 1, :] = x_new
        return y

    @pl.when(hb == 0)
    def _():
        xbc = _silu(conv(proj_ref[:, COL_XBC:COL_XBC + SSD_CONV_DIM], sbuf_in_ref, scw_ref, scb_ref, sbuf_ref))
        xs = xbc[:, :SSD_WIDTH]
        xs_ref[...] = xs
        dtx = _softplus(proj_ref[:, COL_DTX:COL_DTX + SSD_WIDTH] + dtbx_ref[...])
        a = jnp.exp(-dtx * jnp.exp(alogx_ref[...]))
        v = xs * dtx
        for h in range(SSD_HEADS):
            hs = slice(h * SSD_HEAD_DIM, (h + 1) * SSD_HEAD_DIM)
            vh_ref[h] = v[:, hs]
            ah_ref[h] = a[:, hs]
        for g in range(SSD_GROUPS):
            bmt_ref[g] = xbc[:, SSD_WIDTH + g * SSD_STATE:SSD_WIDTH + (g + 1) * SSD_STATE].T
            off = SSD_WIDTH + SSD_GROUPS * SSD_STATE
            cmt_ref[g] = xbc[:, off + g * SSD_STATE:off + (g + 1) * SSD_STATE].T

    g = (hb * SAMPLE_HEAD_BLOCK) // SSD_HPG
    bmt = bmt_ref[g]
    cmt = cmt_ref[g]
    for r in range(SAMPLE_HEAD_BLOCK):
        h = hb * SAMPLE_HEAD_BLOCK + r
        vh = vh_ref[h]
        ah = ah_ref[h]
        for b in range(bt):
            s_new = ssd_in_ref[b, r] * ah[b:b + 1, :] + bmt[:, b:b + 1] * vh[b:b + 1, :]
            ssd_ref[b, r] = s_new
            yh_ref[h, b:b + 1, :] = jnp.sum(cmt[:, b:b + 1] * s_new, axis=0, keepdims=True)

    @pl.when(hb == nhb - 1)
    def _():
        for h in range(SSD_HEADS):
            mix_ref[:, h * SSD_HEAD_DIM:(h + 1) * SSD_HEAD_DIM] = yh_ref[h]
        y = mix_ref[:, :SSD_WIDTH] + dskx_ref[...] * xs_ref[...]
        y = y * _silu(proj_ref[:, COL_Z:COL_Z + SSD_WIDTH])
        mix_ref[:, :SSD_WIDTH] = _rms(y) * snw_ref[...]

        cosf = cos_ref[...]
        sinf = sin_ref[...]
        for h in range(RET_HEADS):
            lg = LOG_GAMMA[h]
            qh = proj_ref[:, COL_Q + h * RET_HEAD_DIM:COL_Q + (h + 1) * RET_HEAD_DIM]
            kh = proj_ref[:, COL_K + h * RET_HEAD_DIM:COL_K + (h + 1) * RET_HEAD_DIM]
            vh = proj_ref[:, COL_V + h * RET_HEAD_DIM:COL_V + (h + 1) * RET_HEAD_DIM]
            gh = proj_ref[:, COL_G + h * RET_HEAD_DIM:COL_G + (h + 1) * RET_HEAD_DIM]
            qh = qh * cosf + pltpu.roll(qh, RET_HEAD_DIM // 2, 1) * sinf
            kh = (kh * cosf + pltpu.roll(kh, RET_HEAD_DIM // 2, 1) * sinf) * (RET_HEAD_DIM ** -0.5)
            qt = qh.T
            kt = kh.T
            hs = slice(SSD_WIDTH + h * RET_HEAD_DIM, SSD_WIDTH + (h + 1) * RET_HEAD_DIM)
            for b in range(bt):
                s_new = ret_in_ref[b, h] * float(np.exp(lg)) + kt[:, b:b + 1] * vh[b:b + 1, :]
                ret_ref[b, h] = s_new
                mix_ref[b:b + 1, hs] = jnp.sum(qt[:, b:b + 1] * s_new, axis=0, keepdims=True)
            mix_ref[:, hs] = _rms(mix_ref[:, hs]) * _silu(gh)

        xc = conv(proj_ref[:, COL_XR:COL_XR + LRU_WIDTH], lbuf_in_ref, lcw_ref, lcb_ref, lbuf_ref)
        xc_bf = xc.astype(BF16)
        r_gate = jax.nn.sigmoid(_dot(xc_bf, wa_ref[...]) + ba_ref[...])
        i_gate = jax.nn.sigmoid(_dot(xc_bf, wi_ref[...]) + bi_ref[...])
        log_a = -LRU_C * r_gate * _softplus(-lam_ref[...])
        h_new = (jnp.exp(log_a) * lruh_in_ref[...]
                 + jnp.sqrt(_neg_expm1(2.0 * log_a)) * (i_gate * xc))
        lruh_ref[...] = h_new
        mix_ref[:, SSD_WIDTH + RET_WIDTH:] = h_new * _gelu(proj_ref[:, COL_GATE:COL_GATE + LRU_WIDTH])

        xo_ref[...] = x_ref[...] + _dot(mix_ref[...].astype(BF16), wout_ref[...])


def _sample_mixer(proj, x_all, x_mixed, cosf, sinf, states, wts, n_prompt, n_sample):
    bt = SAMPLE_TILE
    hbk = SAMPLE_HEAD_BLOCK
    base = n_prompt // bt
    t = x_all.shape[0]
    ssd, sbuf, ret, lruh, lbuf = states
    row_spec = lambda width: pl.BlockSpec((bt, width), lambda i, j: (base + i, 0))
    ssd_spec = pl.BlockSpec((bt, hbk, SSD_STATE, SSD_HEAD_DIM), lambda i, j: (i, j, 0, 0))
    sbuf_spec = pl.BlockSpec((bt, CONV_WIDTH - 1, SSD_CONV_DIM), lambda i, j: (i, 0, 0))
    ret_spec = pl.BlockSpec((bt, RET_HEADS, RET_HEAD_DIM, RET_HEAD_DIM), lambda i, j: (i, 0, 0, 0))
    lruh_spec = pl.BlockSpec((bt, LRU_WIDTH), lambda i, j: (i, 0))
    lbuf_spec = pl.BlockSpec((bt, CONV_WIDTH - 1, LRU_WIDTH), lambda i, j: (i, 0, 0))
    in_specs = [row_spec(PROJ_WIDTH), row_spec(D_MODEL),
                pl.BlockSpec(memory_space=pl.ANY),
                _const_spec((1, LANES)), _const_spec((1, LANES)),
                ssd_spec, sbuf_spec, ret_spec, lruh_spec, lbuf_spec]
    in_specs += [_const_spec(w.shape) for w in wts]
    out_shapes = (jax.ShapeDtypeStruct((t, D_MODEL), F32),
                  jax.ShapeDtypeStruct(ssd.shape, F32),
                  jax.ShapeDtypeStruct(sbuf.shape, F32),
                  jax.ShapeDtypeStruct(ret.shape, F32),
                  jax.ShapeDtypeStruct(lruh.shape, F32),
                  jax.ShapeDtypeStruct(lbuf.shape, F32))
    out_specs = (row_spec(D_MODEL), ssd_spec, sbuf_spec, ret_spec, lruh_spec, lbuf_spec)
    return pl.pallas_call(
        _sample_mixer_kernel,
        grid=(n_sample // bt, SSD_HEADS // hbk),
        in_specs=in_specs,
        out_specs=out_specs,
        out_shape=out_shapes,
        scratch_shapes=[pltpu.VMEM((SSD_HEADS, bt, SSD_HEAD_DIM), F32),
                        pltpu.VMEM((SSD_HEADS, bt, SSD_HEAD_DIM), F32),
                        pltpu.VMEM((SSD_GROUPS, SSD_STATE, bt), F32),
                        pltpu.VMEM((SSD_GROUPS, SSD_STATE, bt), F32),
                        pltpu.VMEM((SSD_HEADS, bt, SSD_HEAD_DIM), F32),
                        pltpu.VMEM((bt, SSD_WIDTH), F32),
                        pltpu.VMEM((bt, D_MIX), F32)],
        input_output_aliases={2: 0},
        compiler_params=_params(("arbitrary", "arbitrary")),
        name="sample_mixer",
    )(proj, x_all, x_mixed, cosf, sinf, ssd, sbuf, ret, lruh, lbuf, *wts)


def _batcher_pairs(n):
    pairs = []

    def merge(lo, hi, r):
        step = r * 2
        if step < hi - lo:
            merge(lo, hi, step)
            merge(lo + r, hi, step)
            pairs.extend((i, i + r) for i in range(lo + r, hi - r, step))
        else:
            pairs.append((lo, lo + r))

    def sort(lo, hi):
        if hi - lo >= 1:
            mid = lo + (hi - lo) // 2
            sort(lo, mid)
            sort(mid + 1, hi)
            merge(lo, hi, 1)

    sort(0, n - 1)
    return pairs


_SORT16_PAIRS = _batcher_pairs(PEER_TOPK)


def _first(a, b):
    return (a[0] > b[0]) | ((a[0] == b[0]) & (a[1] < b[1]))


def _exchange(a, b):
    a_first = _first(a, b)
    hi = (jnp.maximum(a[0], b[0]),) + tuple(jnp.where(a_first, pa, pb) for pa, pb in zip(a[1:], b[1:]))
    lo = (jnp.minimum(a[0], b[0]),) + tuple(jnp.where(a_first, pb, pa) for pa, pb in zip(a[1:], b[1:]))
    return hi, lo


def _better(a, b):
    a_first = _first(a, b)
    return (jnp.maximum(a[0], b[0]),) + tuple(jnp.where(a_first, pa, pb) for pa, pb in zip(a[1:], b[1:]))


def _sort16(items):
    items = list(items)
    for i, j in _SORT16_PAIRS:
        items[i], items[j] = _exchange(items[i], items[j])
    return items


def _merge_top16(a, b):
    k = PEER_TOPK
    c = [_better(a[i], b[k - 1 - i]) for i in range(k)]
    d = k // 2
    while d >= 1:
        for i in range(k):
            if i & d == 0:
                c[i], c[i + d] = _exchange(c[i], c[i + d])
        d //= 2
    return c


def _top16_of_lists(vals, tags):
    k = PEER_TOPK
    best = None
    for g in range(0, len(vals), k):
        group = _sort16([(vals[i]() if callable(vals[i]) else vals[i], tags[i]) for i in range(g, g + k)])
        best = group if best is None else _merge_top16(best, group)
    return [b[0] for b in best], [b[1] for b in best]


ROUTE_TILE = SUBLANES * LANES
ROW_PITCH = PEER_KEYS + SUBLANES


def _peer_route_kernel(x_ref, n_ref, wq_ref, keys_ref,
                       xn_ref, ia_ref, ib_ref, gate_ref,
                       q_ref, z1_ref, z2_ref, za_ref, zb_ref, zg_ref):
    k = PEER_TOPK
    tile = (SUBLANES, LANES)
    xn = (_rms(x_ref[...]) * n_ref[...]).astype(BF16)
    xn_ref[...] = xn
    q = _dot(xn, wq_ref[...]).astype(BF16)
    for hc in range(2 * PEER_HEADS):
        q_ref[hc] = q[:, hc * PEER_HALF:(hc + 1) * PEER_HALF]

    def head(h, carry):
        for c, z_ref in ((0, z1_ref), (1, z2_ref)):
            keys = keys_ref[2 * h + c]
            for j in range(SUBLANES):
                z_ref[j * ROW_PITCH:j * ROW_PITCH + PEER_KEYS, :] = _dot_nt(
                    keys, q_ref[2 * h + c, j * LANES:(j + 1) * LANES, :])

        def top_keys(z_ref):
            load = lambda key: (lambda: z_ref[pl.ds(key, SUBLANES, stride=ROW_PITCH), :])
            return _top16_of_lists([load(key) for key in range(PEER_KEYS)],
                                   [jnp.full(tile, float(key), F32) for key in range(PEER_KEYS)])

        t1, i1 = top_keys(z1_ref)
        t2, i2 = top_keys(z2_ref)
        i1s = [v * float(PEER_KEYS) for v in i1]

        def pair(a, b):
            return (t1[a] + t2[b], jnp.full(tile, float(a * k + b), F32), i1s[a] + i2[b])

        pad = (jnp.full(tile, -jnp.inf, F32), jnp.full(tile, float(k * k), F32), jnp.zeros(tile, F32))
        row = lambda a: [pair(a, b) for b in range(k // (a + 1))]
        first = _merge_top16(row(0), row(1) + [pad] * (k - len(row(1))))
        second = _sort16(row(2) + row(3) + row(4) + row(5) + row(6))
        third_items = row(7) + [pair(a, 0) for a in range(8, k)]
        third = _sort16(third_items + [pad] * (k - len(third_items)))
        best = _merge_top16(_merge_top16(first, second), third)

        e = [jnp.exp(item[0] - best[0][0]) for item in best]
        z = e[0]
        for v in e[1:]:
            z = z + v
        inv = 1.0 / z
        for r in range(k):
            sel = h * k + r
            expert = best[r][2]
            a_idx = jnp.floor(expert * (1.0 / PEER_KEYS))
            za_ref[pl.ds(sel, SUBLANES, stride=ROW_PITCH), :] = a_idx
            zb_ref[pl.ds(sel, SUBLANES, stride=ROW_PITCH), :] = expert - a_idx * float(PEER_KEYS)
            zg_ref[pl.ds(sel, SUBLANES, stride=ROW_PITCH), :] = e[r] * inv
        return carry

    lax.fori_loop(0, PEER_HEADS, head, 0)
    for j in range(SUBLANES):
        rows = slice(j * LANES, (j + 1) * LANES)
        src = slice(j * ROW_PITCH, j * ROW_PITCH + PEER_HEADS * k)
        ia_ref[rows, :] = za_ref[src, :].T
        ib_ref[rows, :] = zb_ref[src, :].T
        gate_ref[rows, :] = zg_ref[src, :].T


def _peer_route(x, n2, wq, keys):
    t = x.shape[0]
    tm = ROUTE_TILE
    sel_spec = pl.BlockSpec((tm, PEER_HEADS * PEER_TOPK), lambda i: (i, 0))
    sel_shape = jax.ShapeDtypeStruct((t, PEER_HEADS * PEER_TOPK), F32)
    z_scratch = pltpu.VMEM((SUBLANES * ROW_PITCH, LANES), F32)
    return pl.pallas_call(
        _peer_route_kernel,
        grid=(pl.cdiv(t, tm),),
        in_specs=[pl.BlockSpec((tm, D_MODEL), lambda i: (i, 0)),
                  _const_spec((1, D_MODEL)),
                  _const_spec(wq.shape),
                  _const_spec(keys.shape)],
        out_specs=(pl.BlockSpec((tm, D_MODEL), lambda i: (i, 0)), sel_spec, sel_spec, sel_spec),
        out_shape=(jax.ShapeDtypeStruct((t, D_MODEL), BF16), sel_shape, sel_shape, sel_shape),
        scratch_shapes=[pltpu.VMEM((2 * PEER_HEADS, tm, PEER_HALF), BF16),
                        z_scratch, z_scratch, z_scratch, z_scratch, z_scratch],
        compiler_params=_params(("arbitrary",)),
        name="peer_route",
    )(x, n2, wq, keys)


def _peer_mlp_kernel(xn_ref, u_ref, v_ref, ia_ref, ib_ref, gate_ref, x_ref, fw_ref,
                     o_ref, acc_ref, w_ref, g_ref, *, final_norm):
    e = pl.program_id(1)
    tm = xn_ref.shape[0]
    rows_per_tile = EXPERT_TILE // PEER_KEYS

    @pl.when(e == 0)
    def _():
        acc_ref[...] = jnp.zeros_like(acc_ref)
        key = lax.broadcasted_iota(jnp.int32, (PEER_KEYS, PEER_HEADS * PEER_TOPK), 0).astype(F32)

        def build(t, carry):
            gate = gate_ref[pl.ds(t, 1), :]
            sel_gate = jnp.where(key == ia_ref[pl.ds(t, 1), :], gate, 0.0)
            hi = sel_gate.astype(BF16)
            lo = (sel_gate - hi.astype(F32)).astype(BF16)
            onehot = jnp.where(key == ib_ref[pl.ds(t, 1), :], 1.0, 0.0).astype(BF16)
            g_ref[pl.ds(pl.multiple_of(t * ROW_PITCH, SUBLANES), PEER_KEYS), :] = (
                _dot_nt(hi, onehot) + _dot_nt(lo, onehot))
            return carry

        lax.fori_loop(0, tm, build, 0)

    xn = xn_ref[...]
    for j in range(rows_per_tile):
        i1 = e * rows_per_tile + j
        act = _dot_nt(xn, u_ref[j * PEER_KEYS:(j + 1) * PEER_KEYS, :])
        gate = g_ref[pl.ds(i1, tm, stride=ROW_PITCH), :]
        w_ref[:, j * PEER_KEYS:(j + 1) * PEER_KEYS] = (gate * _gelu(act)).astype(BF16)
    acc_ref[...] += _dot(w_ref[...], v_ref[...])

    @pl.when(e == pl.num_programs(1) - 1)
    def _():
        y = x_ref[...] + acc_ref[...]
        if final_norm:
            y = _rms(y) * fw_ref[...]
        o_ref[...] = y


def _peer_mlp(xn, u_bf, v_bf, route, x, fw, tm, final_norm):
    t = x.shape[0]
    ia, ib, gate = route
    tok_spec = lambda width: pl.BlockSpec((tm, width), lambda i, e: (i, 0))
    return pl.pallas_call(
        functools.partial(_peer_mlp_kernel, final_norm=final_norm),
        grid=(t // tm, PEER_EXPERTS // EXPERT_TILE),
        in_specs=[tok_spec(D_MODEL),
                  pl.BlockSpec((EXPERT_TILE, D_MODEL), lambda i, e: (e, 0)),
                  pl.BlockSpec((EXPERT_TILE, D_MODEL), lambda i, e: (e, 0)),
                  tok_spec(PEER_HEADS * PEER_TOPK), tok_spec(PEER_HEADS * PEER_TOPK),
                  tok_spec(PEER_HEADS * PEER_TOPK),
                  tok_spec(D_MODEL),
                  _const_spec((1, D_MODEL))],
        out_specs=tok_spec(D_MODEL),
        out_shape=jax.ShapeDtypeStruct((t, D_MODEL), F32),
        scratch_shapes=[pltpu.VMEM((tm, D_MODEL), F32),
                        pltpu.VMEM((tm, EXPERT_TILE), BF16),
                        pltpu.VMEM((tm * ROW_PITCH, PEER_KEYS), F32)],
        compiler_params=_params(("arbitrary", "arbitrary")),
        name="peer_mlp",
    )(xn, u_bf, v_bf, ia, ib, gate, x, fw)


def _token_tile(t):
    for tm in (512, 384, 256, 128):
        if t % tm == 0:
            return tm
    raise ValueError(f"token count {t} must be a multiple of {LANES}")


def _rope_tables(pos):
    half = RET_HEAD_DIM // 2
    inv = ROPE_BASE ** (-jnp.arange(half, dtype=F32) / half)
    ang = pos.astype(F32)[:, None] * inv[None, :]
    cos = jnp.cos(ang)
    sin = jnp.sin(ang)
    return jnp.concatenate([cos, cos], axis=1), jnp.concatenate([-sin, sin], axis=1)


def _block_diag(w):
    eye = jnp.eye(LRU_BLOCKS, dtype=w.dtype)
    return jnp.einsum('hij,hg->higj', w, eye).reshape(LRU_WIDTH, LRU_WIDTH)


def kernel(x_prompt, x_sample, state_ssd, state_ssd_conv, state_ret, state_lru, state_lru_conv, norm1_w, w_in, ssd_conv_w, ssd_conv_b, ssd_dt_bias, ssd_a_log, ssd_d, ssd_norm_w, lru_conv_w, lru_conv_b, lru_wa, lru_ba, lru_wi, lru_bi, lru_lambda, w_out, norm2_w, peer_wq, peer_keys, peer_u, peer_v, final_norm_w):
    batch, seq, _ = x_prompt.shape
    n_sample, dec_seq, _ = x_sample.shape
    depth = w_in.shape[0]
    assert dec_seq == 1 and seq % SCAN_CHUNK == 0 and n_sample % SAMPLE_TILE == 0
    n_prompt = batch * seq
    t = n_prompt + n_sample
    tm = _token_tile(t)

    x = jnp.concatenate([x_prompt.reshape(n_prompt, D_MODEL), x_sample.reshape(n_sample, D_MODEL)], axis=0)
    cos_p, sin_p = _rope_tables(jnp.arange(seq))
    cos_s, sin_s = _rope_tables(PAST_LEN + jnp.arange(dec_seq))
    row = lambda a: a.reshape(1, -1).astype(F32)
    per_channel = lambda a: jnp.repeat(a.astype(F32), SSD_HEAD_DIM).reshape(1, SSD_WIDTH)
    per_head = lambda a: jnp.pad(a.astype(F32), (0, LANES - SSD_HEADS)).reshape(1, LANES)
    off = SSD_WIDTH + SSD_CONV_DIM

    new_p, new_s = [], []
    for i in range(depth):
        wi_ = w_in[i]
        dt_cols = wi_[:, off:off + SSD_HEADS]
        w_in_p = jnp.concatenate(
            [wi_[:, :off],
             jnp.pad(dt_cols, ((0, 0), (0, LANES - SSD_HEADS))),
             jnp.repeat(dt_cols, SSD_HEAD_DIM, axis=1),
             wi_[:, off + SSD_HEADS:]], axis=1).astype(BF16)
        wts = (ssd_conv_w[i], row(ssd_conv_b[i]), per_head(ssd_dt_bias[i]), per_head(ssd_a_log[i]),
               per_channel(ssd_dt_bias[i]), per_channel(ssd_a_log[i]), per_channel(ssd_d[i]),
               row(ssd_norm_w[i]), lru_conv_w[i], row(lru_conv_b[i]),
               _block_diag(lru_wa[i]).astype(BF16), row(lru_ba[i]),
               _block_diag(lru_wi[i]).astype(BF16), row(lru_bi[i]), row(lru_lambda[i]),
               w_out[i].astype(BF16))

        proj = _inproj(x, row(norm1_w[i]), w_in_p, tm)
        x_mixed, *st_p = _prompt_mixer(proj, x, cos_p, sin_p, wts, batch, seq)
        st_p[3] = st_p[3].reshape(batch, LRU_WIDTH)
        x, *st_s = _sample_mixer(proj, x, x_mixed, cos_s, sin_s,
                                 (state_ssd[i], state_ssd_conv[i], state_ret[i], state_lru[i],
                                  state_lru_conv[i]), wts, n_prompt, n_sample)
        new_p.append(st_p)
        new_s.append(st_s)

        keys_bf = peer_keys[i].reshape(PEER_HEADS * 2, PEER_KEYS, PEER_HALF).astype(BF16)
        xn, *route = _peer_route(x, row(norm2_w[i]), peer_wq[i].astype(BF16), keys_bf)
        x = _peer_mlp(xn, peer_u[i].astype(BF16), peer_v[i].astype(BF16), route, x,
                      row(final_norm_w), tm, final_norm=(i == depth - 1))

    y_prompt = x[:n_prompt].reshape(batch, seq, D_MODEL)
    y_sample = x[n_prompt:].reshape(n_sample, dec_seq, D_MODEL)
    stk = lambda states, j: jnp.stack([s[j] for s in states])
    return (y_prompt, y_sample,
            stk(new_p, 0), stk(new_p, 1), stk(new_p, 2), stk(new_p, 3), stk(new_p, 4),
            stk(new_s, 0), stk(new_s, 1), stk(new_s, 2), stk(new_s, 3), stk(new_s, 4))
```

```python
import functools

import numpy as np
import jax
import jax.numpy as jnp
from jax import lax
from jax.experimental import pallas as pl
from jax.experimental.pallas import tpu as pltpu

F32 = jnp.float32
BF16 = jnp.bfloat16

D_MODEL = 1024
PAST_LEN = 16384
D_MIX = 2 * D_MODEL
CONV_WIDTH = 4
NORM_EPS = 1e-6
SSD_WIDTH = D_MIX // 2
SSD_HEAD_DIM = 64
SSD_HEADS = SSD_WIDTH // SSD_HEAD_DIM
SSD_GROUPS = 2
SSD_HPG = SSD_HEADS // SSD_GROUPS
SSD_STATE = 128
SSD_CONV_DIM = SSD_WIDTH + 2 * SSD_GROUPS * SSD_STATE
RET_WIDTH = D_MIX // 4
RET_HEADS = 4
RET_HEAD_DIM = RET_WIDTH // RET_HEADS
ROPE_BASE = 10000.0
LRU_WIDTH = D_MIX // 4
LRU_BLOCKS = 8
LRU_BLOCK_DIM = LRU_WIDTH // LRU_BLOCKS
LRU_C = 8.0
PEER_KEYS = 128
PEER_EXPERTS = PEER_KEYS * PEER_KEYS
PEER_HEADS = 8
PEER_TOPK = 16
PEER_QUERY_DIM = 256
PEER_HALF = PEER_QUERY_DIM // 2

LANES = 128
SUBLANES = 8
VMEM_LIMIT = 56 * 1024 * 1024

COL_Z = 0
COL_XBC = COL_Z + SSD_WIDTH
COL_DT16 = COL_XBC + SSD_CONV_DIM
COL_DTX = COL_DT16 + LANES
COL_Q = COL_DTX + SSD_WIDTH
COL_K = COL_Q + RET_WIDTH
COL_V = COL_K + RET_WIDTH
COL_G = COL_V + RET_WIDTH
COL_GATE = COL_G + RET_WIDTH
COL_XR = COL_GATE + LRU_WIDTH
PROJ_WIDTH = COL_XR + LRU_WIDTH

SCAN_CHUNK = 128
SAMPLE_TILE = SUBLANES
SAMPLE_HEAD_BLOCK = 4
EXPERT_TILE = 1024
BUILD_UNROLL = 8
LOG_GAMMA = [float(np.log(1.0 - 2.0 ** (-5.0 - h))) for h in range(RET_HEADS)]

_NT = (((1,), (1,)), ((), ()))


def _rms(x):
    return x * lax.rsqrt(jnp.mean(x * x, axis=-1, keepdims=True) + NORM_EPS)


def _softplus(x):
    return jnp.maximum(x, 0.0) + jnp.log1p(jnp.exp(-jnp.abs(x)))


def _silu(x):
    return x * jax.nn.sigmoid(x)


def _gelu(x):
    return 0.5 * x * (1.0 + lax.erf(x * np.float32(np.sqrt(0.5))))


def _neg_expm1(x):
    return -jnp.tanh(0.5 * x) * (jnp.exp(x) + 1.0)


def _dot(a, b):
    return jnp.dot(a, b, preferred_element_type=F32)


def _dot_nt(a, b):
    return lax.dot_general(a, b, _NT, preferred_element_type=F32)


def _dot_exact(a, b):
    return jnp.dot(a, b, preferred_element_type=F32, precision=lax.Precision.HIGHEST)


def _params(semantics):
    return pltpu.CompilerParams(dimension_semantics=semantics, vmem_limit_bytes=VMEM_LIMIT)


def _const_spec(shape):
    zeros = (0,) * len(shape)
    return pl.BlockSpec(shape, lambda *_: zeros)


def _inproj_kernel(x_ref, n_ref, w_ref, o_ref):
    h = (_rms(x_ref[...]) * n_ref[...]).astype(BF16)
    step = 8 * LANES
    for j in range(0, PROJ_WIDTH, step):
        hi = min(j + step, PROJ_WIDTH)
        o_ref[:, j:hi] = _dot(h, w_ref[:, j:hi])


def _inproj(x, n1, w_in_p, tm):
    t = x.shape[0]
    return pl.pallas_call(
        _inproj_kernel,
        grid=(t // tm,),
        in_specs=[pl.BlockSpec((tm, D_MODEL), lambda i: (i, 0)),
                  _const_spec((1, D_MODEL)),
                  pl.BlockSpec((D_MODEL, PROJ_WIDTH), lambda i: (0, 0), pipeline_mode=pl.Buffered(1))],
        out_specs=pl.BlockSpec((tm, PROJ_WIDTH), lambda i: (i, 0)),
        out_shape=jax.ShapeDtypeStruct((t, PROJ_WIDTH), F32),
        compiler_params=_params(("arbitrary",)),
        name="inproj",
    )(x, n1, w_in_p)


def _prompt_mixer_kernel(proj_ref, x_ref, cos_ref, sin_ref, scw_ref, scb_ref, dtb16_ref, alog16_ref,
                         dtbx_ref, alogx_ref, dskx_ref, snw_ref, lcw_ref, lcb_ref, wa_ref, ba_ref,
                         wi_ref, bi_ref, lam_ref, wout_ref,
                         xo_ref, ssd_ref, sbuf_ref, ret_ref, lruh_ref, lbuf_ref,
                         xps_ref, xpl_ref, mix_ref):
    c = SCAN_CHUNK
    step = pl.program_id(1)
    pad = SUBLANES
    keep = CONV_WIDTH - 1

    @pl.when(step == 0)
    def _():
        ssd_ref[...] = jnp.zeros_like(ssd_ref)
        ret_ref[...] = jnp.zeros_like(ret_ref)
        lruh_ref[...] = jnp.zeros_like(lruh_ref)
        xps_ref[0:pad, :] = jnp.zeros((pad, SSD_CONV_DIM), F32)
        xpl_ref[0:pad, :] = jnp.zeros((pad, LRU_WIDTH), F32)

    def conv(xp_ref, x_new, w_ref, b_ref, buf_out_ref):
        xp_ref[pad:pad + c, :] = x_new
        y = b_ref[...] + xp_ref[pad - keep:pad - keep + c, :] * w_ref[0:1, :]
        for tap in range(1, CONV_WIDTH):
            y = y + xp_ref[pad - keep + tap:pad - keep + tap + c, :] * w_ref[tap:tap + 1, :]
        tail = xp_ref[pad + c - keep:pad + c, :]
        xp_ref[pad - keep:pad, :] = tail
        buf_out_ref[0] = tail
        return y

    row = lax.broadcasted_iota(jnp.int32, (c, c), 0)
    col = lax.broadcasted_iota(jnp.int32, (c, c), 1)
    causal = row >= col
    tri = causal.astype(F32)
    rowf = lax.broadcasted_iota(jnp.int32, (c, LANES), 0).astype(F32)

    xbc = _silu(conv(xps_ref, proj_ref[:, COL_XBC:COL_XBC + SSD_CONV_DIM], scw_ref, scb_ref, sbuf_ref))
    xs = xbc[:, :SSD_WIDTH]
    bm = xbc[:, SSD_WIDTH:SSD_WIDTH + SSD_GROUPS * SSD_STATE]
    cm = xbc[:, SSD_WIDTH + SSD_GROUPS * SSD_STATE:]
    dtx = _softplus(proj_ref[:, COL_DTX:COL_DTX + SSD_WIDTH] + dtbx_ref[...])
    cumx = _dot_exact(tri, -dtx * jnp.exp(alogx_ref[...]))
    dt16 = _softplus(proj_ref[:, COL_DT16:COL_DT16 + LANES] + dtb16_ref[...])
    cum16 = _dot_exact(tri, -dt16 * jnp.exp(alog16_ref[...]))
    cum16_t = cum16.T
    v = xs * dtx
    v_bf = v.astype(BF16)
    cum_last = cumx[c - 1:c, :]
    vte_bf = (v * jnp.exp(cum_last - cumx)).astype(BF16)
    ecum = jnp.exp(cumx)
    chunk_decay = jnp.exp(cum_last)
    for g in range(SSD_GROUPS):
        bmg = bm[:, g * SSD_STATE:(g + 1) * SSD_STATE]
        cmg_bf = cm[:, g * SSD_STATE:(g + 1) * SSD_STATE].astype(BF16)
        scores = _dot_nt(cmg_bf, bmg.astype(BF16))
        bmg_t_bf = bmg.T.astype(BF16)
        for r in range(SSD_HPG):
            h = g * SSD_HPG + r
            hs = slice(h * SSD_HEAD_DIM, (h + 1) * SSD_HEAD_DIM)
            seg = cum16[:, h:h + 1] - cum16_t[h:h + 1, :]
            decay = jnp.exp(jnp.where(causal, seg, -jnp.inf))
            y_intra = _dot((scores * decay).astype(BF16), v_bf[:, hs])
            s_in = ssd_ref[0, h]
            y_inter = _dot(cmg_bf, s_in.astype(BF16)) * ecum[:, hs]
            ssd_ref[0, h] = s_in * chunk_decay[:, hs] + _dot(bmg_t_bf, vte_bf[:, hs])
            mix_ref[:, hs] = y_intra + y_inter
    y = mix_ref[:, :SSD_WIDTH] + dskx_ref[...] * xs
    y = y * _silu(proj_ref[:, COL_Z:COL_Z + SSD_WIDTH])
    mix_ref[:, :SSD_WIDTH] = _rms(y) * snw_ref[...]

    cosf = cos_ref[...]
    sinf = sin_ref[...]
    rowcol = (row - col).astype(F32)
    for h in range(RET_HEADS):
        lg = LOG_GAMMA[h]
        hs = slice(h * RET_HEAD_DIM, (h + 1) * RET_HEAD_DIM)
        qh = proj_ref[:, COL_Q + h * RET_HEAD_DIM:COL_Q + (h + 1) * RET_HEAD_DIM]
        kh = proj_ref[:, COL_K + h * RET_HEAD_DIM:COL_K + (h + 1) * RET_HEAD_DIM]
        vh = proj_ref[:, COL_V + h * RET_HEAD_DIM:COL_V + (h + 1) * RET_HEAD_DIM]
        gh = proj_ref[:, COL_G + h * RET_HEAD_DIM:COL_G + (h + 1) * RET_HEAD_DIM]
        qh = qh * cosf + pltpu.roll(qh, RET_HEAD_DIM // 2, 1) * sinf
        kh = (kh * cosf + pltpu.roll(kh, RET_HEAD_DIM // 2, 1) * sinf) * (RET_HEAD_DIM ** -0.5)
        qh_bf = qh.astype(BF16)
        scores = _dot_nt(qh_bf, kh.astype(BF16))
        decay = jnp.exp(jnp.where(causal, rowcol * lg, -jnp.inf))
        y_intra = _dot((scores * decay).astype(BF16), vh.astype(BF16))
        s_in = ret_ref[0, h]
        y_inter = _dot(qh_bf, s_in.astype(BF16)) * jnp.exp((rowf + 1.0) * lg)
        v_to_end = (vh * jnp.exp((c - 1.0 - rowf) * lg)).astype(BF16)
        ret_ref[0, h] = s_in * float(np.exp(c * lg)) + _dot(kh.T.astype(BF16), v_to_end)
        mix_ref[:, SSD_WIDTH + h * RET_HEAD_DIM:SSD_WIDTH + (h + 1) * RET_HEAD_DIM] = (
            _rms(y_intra + y_inter) * _silu(gh))

    xc = conv(xpl_ref, proj_ref[:, COL_XR:COL_XR + LRU_WIDTH], lcw_ref, lcb_ref, lbuf_ref)
    xc_bf = xc.astype(BF16)
    r_gate = jax.nn.sigmoid(_dot(xc_bf, wa_ref[...]) + ba_ref[...])
    i_gate = jax.nn.sigmoid(_dot(xc_bf, wi_ref[...]) + bi_ref[...])
    log_a = -LRU_C * r_gate * _softplus(-lam_ref[...])
    a_run = jnp.exp(log_a)
    b_run = jnp.sqrt(_neg_expm1(2.0 * log_a)) * (i_gate * xc)
    rowl = lax.broadcasted_iota(jnp.int32, (c, LRU_WIDTH), 0)
    shift = 1
    while shift < c:
        valid = rowl >= shift
        a_prev = pltpu.roll(a_run, shift, 0)
        b_prev = pltpu.roll(b_run, shift, 0)
        b_run = jnp.where(valid, a_run * b_prev + b_run, b_run)
        a_run = jnp.where(valid, a_run * a_prev, a_run)
        shift *= 2
    h_all = b_run + a_run * lruh_ref[0]
    lruh_ref[0] = h_all[c - 1:c, :]
    mix_ref[:, SSD_WIDTH + RET_WIDTH:] = h_all * _gelu(proj_ref[:, COL_GATE:COL_GATE + LRU_WIDTH])

    xo_ref[...] = x_ref[...] + _dot(mix_ref[...].astype(BF16), wout_ref[...])


def _prompt_mixer(proj, x_all, cosf, sinf, wts, batch, seq):
    c = SCAN_CHUNK
    nc = seq // c
    t = x_all.shape[0]
    row_spec = lambda width: pl.BlockSpec((c, width), lambda b, s: (b * nc + s, 0))
    state_spec = lambda shape: pl.BlockSpec((1,) + shape, lambda b, s: (b,) + (0,) * len(shape))
    in_specs = [row_spec(PROJ_WIDTH), row_spec(D_MODEL),
                pl.BlockSpec((c, LANES), lambda b, s: (s, 0)),
                pl.BlockSpec((c, LANES), lambda b, s: (s, 0))]
    in_specs += [_const_spec(w.shape) for w in wts]
    out_shapes = (jax.ShapeDtypeStruct((t, D_MODEL), F32),
                  jax.ShapeDtypeStruct((batch, SSD_HEADS, SSD_STATE, SSD_HEAD_DIM), F32),
                  jax.ShapeDtypeStruct((batch, CONV_WIDTH - 1, SSD_CONV_DIM), F32),
                  jax.ShapeDtypeStruct((batch, RET_HEADS, RET_HEAD_DIM, RET_HEAD_DIM), F32),
                  jax.ShapeDtypeStruct((batch, 1, LRU_WIDTH), F32),
                  jax.ShapeDtypeStruct((batch, CONV_WIDTH - 1, LRU_WIDTH), F32))
    out_specs = (row_spec(D_MODEL),
                 state_spec((SSD_HEADS, SSD_STATE, SSD_HEAD_DIM)),
                 state_spec((CONV_WIDTH - 1, SSD_CONV_DIM)),
                 state_spec((RET_HEADS, RET_HEAD_DIM, RET_HEAD_DIM)),
                 state_spec((1, LRU_WIDTH)),
                 state_spec((CONV_WIDTH - 1, LRU_WIDTH)))
    return pl.pallas_call(
        _prompt_mixer_kernel,
        grid=(batch, nc),
        in_specs=in_specs,
        out_specs=out_specs,
        out_shape=out_shapes,
        scratch_shapes=[pltpu.VMEM((SUBLANES + c, SSD_CONV_DIM), F32),
                        pltpu.VMEM((SUBLANES + c, LRU_WIDTH), F32),
                        pltpu.VMEM((c, D_MIX), F32)],
        compiler_params=_params(("arbitrary", "arbitrary")),
        name="prompt_mixer",
    )(proj, x_all, cosf, sinf, *wts)


def _sample_mixer_kernel(proj_ref, x_ref, xprev_ref, cos_ref, sin_ref, ssd_in_ref, sbuf_in_ref, ret_in_ref,
                         lruh_in_ref, lbuf_in_ref, scw_ref, scb_ref, dtb16_ref, alog16_ref,
                         dtbx_ref, alogx_ref, dskx_ref, snw_ref, lcw_ref, lcb_ref, wa_ref, ba_ref,
                         wi_ref, bi_ref, lam_ref, wout_ref,
                         xo_ref, ssd_ref, sbuf_ref, ret_ref, lruh_ref, lbuf_ref,
                         vh_ref, ah_ref, bmt_ref, cmt_ref, yh_ref, xs_ref, mix_ref):
    del xprev_ref, dtb16_ref, alog16_ref
    bt = SAMPLE_TILE
    hb = pl.program_id(1)
    nhb = pl.num_programs(1)
    keep = CONV_WIDTH - 1

    def conv(x_new, buf_in_ref, w_ref, b_ref, buf_out_ref):
        y = b_ref[...] + x_new * w_ref[keep:keep + 1, :]
        for tap in range(keep):
            y = y + buf_in_ref[:, tap, :] * w_ref[tap:tap + 1, :]
        for tap in range(keep - 1):
            buf_out_ref[:, tap, :] = buf_in_ref[:, tap + 1, :]
        buf_out_ref[:, keep - 1, :] = x_new
        return y

    @pl.when(hb == 0)
    def _():
        xbc = _silu(conv(proj_ref[:, COL_XBC:COL_XBC + SSD_CONV_DIM], sbuf_in_ref, scw_ref, scb_ref, sbuf_ref))
        xs = xbc[:, :SSD_WIDTH]
        xs_ref[...] = xs
        dtx = _softplus(proj_ref[:, COL_DTX:COL_DTX + SSD_WIDTH] + dtbx_ref[...])
        a = jnp.exp(-dtx * jnp.exp(alogx_ref[...]))
        v = xs * dtx
        for h in range(SSD_HEADS):
            hs = slice(h * SSD_HEAD_DIM, (h + 1) * SSD_HEAD_DIM)
            vh_ref[h] = v[:, hs]
            ah_ref[h] = a[:, hs]
        for g in range(SSD_GROUPS):
            bmt_ref[g] = xbc[:, SSD_WIDTH + g * SSD_STATE:SSD_WIDTH + (g + 1) * SSD_STATE].T
            off = SSD_WIDTH + SSD_GROUPS * SSD_STATE
            cmt_ref[g] = xbc[:, off + g * SSD_STATE:off + (g + 1) * SSD_STATE].T

    g = (hb * SAMPLE_HEAD_BLOCK) // SSD_HPG
    bmt = bmt_ref[g]
    cmt = cmt_ref[g]
    for r in range(SAMPLE_HEAD_BLOCK):
        h = hb * SAMPLE_HEAD_BLOCK + r
        vh = vh_ref[h]
        ah = ah_ref[h]
        for b in range(bt):
            s_new = ssd_in_ref[b, r] * ah[b:b + 1, :] + bmt[:, b:b + 1] * vh[b:b + 1, :]
            ssd_ref[b, r] = s_new
            yh_ref[h, b:b + 1, :] = jnp.sum(cmt[:, b:b + 1] * s_new, axis=0, keepdims=True)

    @pl.when(hb == nhb - 1)
    def _():
        for h in range(SSD_HEADS):
            mix_ref[:, h * SSD_HEAD_DIM:(h + 1) * SSD_HEAD_DIM] = yh_ref[h]
        y = mix_ref[:, :SSD_WIDTH] + dskx_ref[...] * xs_ref[...]
        y = y * _silu(proj_ref[:, COL_Z:COL_Z + SSD_WIDTH])
        mix_ref[:, :SSD_WIDTH] = _rms(y) * snw_ref[...]

        cosf = cos_ref[...]
        sinf = sin_ref[...]
        for h in range(RET_HEADS):
            lg = LOG_GAMMA[h]
            qh = proj_ref[:, COL_Q + h * RET_HEAD_DIM:COL_Q + (h + 1) * RET_HEAD_DIM]
            kh = proj_ref[:, COL_K + h * RET_HEAD_DIM:COL_K + (h + 1) * RET_HEAD_DIM]
            vh = proj_ref[:, COL_V + h * RET_HEAD_DIM:COL_V + (h + 1) * RET_HEAD_DIM]
            gh = proj_ref[:, COL_G + h * RET_HEAD_DIM:COL_G + (h + 1) * RET_HEAD_DIM]
            qh = qh * cosf + pltpu.roll(qh, RET_HEAD_DIM // 2, 1) * sinf
            kh = (kh * cosf + pltpu.roll(kh, RET_HEAD_DIM // 2, 1) * sinf) * (RET_HEAD_DIM ** -0.5)
            qt = qh.T
            kt = kh.T
            hs = slice(SSD_WIDTH + h * RET_HEAD_DIM, SSD_WIDTH + (h + 1) * RET_HEAD_DIM)
            for b in range(bt):
                s_new = ret_in_ref[b, h] * float(np.exp(lg)) + kt[:, b:b + 1] * vh[b:b + 1, :]
                ret_ref[b, h] = s_new
                mix_ref[b:b + 1, hs] = jnp.sum(qt[:, b:b + 1] * s_new, axis=0, keepdims=True)
            mix_ref[:, hs] = _rms(mix_ref[:, hs]) * _silu(gh)

        xc = conv(proj_ref[:, COL_XR:COL_XR + LRU_WIDTH], lbuf_in_ref, lcw_ref, lcb_ref, lbuf_ref)
        xc_bf = xc.astype(BF16)
        r_gate = jax.nn.sigmoid(_dot(xc_bf, wa_ref[...]) + ba_ref[...])
        i_gate = jax.nn.sigmoid(_dot(xc_bf, wi_ref[...]) + bi_ref[...])
        log_a = -LRU_C * r_gate * _softplus(-lam_ref[...])
        h_new = (jnp.exp(log_a) * lruh_in_ref[...]
                 + jnp.sqrt(_neg_expm1(2.0 * log_a)) * (i_gate * xc))
        lruh_ref[...] = h_new
        mix_ref[:, SSD_WIDTH + RET_WIDTH:] = h_new * _gelu(proj_ref[:, COL_GATE:COL_GATE + LRU_WIDTH])

        xo_ref[...] = x_ref[...] + _dot(mix_ref[...].astype(BF16), wout_ref[...])


def _sample_mixer(proj, x_all, x_mixed, cosf, sinf, states, wts, n_prompt, n_sample):
    bt = SAMPLE_TILE
    hbk = SAMPLE_HEAD_BLOCK
    base = n_prompt // bt
    t = x_all.shape[0]
    ssd, sbuf, ret, lruh, lbuf = states
    row_spec = lambda width: pl.BlockSpec((bt, width), lambda i, j: (base + i, 0))
    ssd_spec = pl.BlockSpec((bt, hbk, SSD_STATE, SSD_HEAD_DIM), lambda i, j: (i, j, 0, 0))
    sbuf_spec = pl.BlockSpec((bt, CONV_WIDTH - 1, SSD_CONV_DIM), lambda i, j: (i, 0, 0))
    ret_spec = pl.BlockSpec((bt, RET_HEADS, RET_HEAD_DIM, RET_HEAD_DIM), lambda i, j: (i, 0, 0, 0))
    lruh_spec = pl.BlockSpec((bt, LRU_WIDTH), lambda i, j: (i, 0))
    lbuf_spec = pl.BlockSpec((bt, CONV_WIDTH - 1, LRU_WIDTH), lambda i, j: (i, 0, 0))
    in_specs = [row_spec(PROJ_WIDTH), row_spec(D_MODEL),
                pl.BlockSpec(memory_space=pl.ANY),
                _const_spec((1, LANES)), _const_spec((1, LANES)),
                ssd_spec, sbuf_spec, ret_spec, lruh_spec, lbuf_spec]
    in_specs += [_const_spec(w.shape) for w in wts]
    out_shapes = (jax.ShapeDtypeStruct((t, D_MODEL), F32),
                  jax.ShapeDtypeStruct(ssd.shape, F32),
                  jax.ShapeDtypeStruct(sbuf.shape, F32),
                  jax.ShapeDtypeStruct(ret.shape, F32),
                  jax.ShapeDtypeStruct(lruh.shape, F32),
                  jax.ShapeDtypeStruct(lbuf.shape, F32))
    out_specs = (row_spec(D_MODEL), ssd_spec, sbuf_spec, ret_spec, lruh_spec, lbuf_spec)
    return pl.pallas_call(
        _sample_mixer_kernel,
        grid=(n_sample // bt, SSD_HEADS // hbk),
        in_specs=in_specs,
        out_specs=out_specs,
        out_shape=out_shapes,
        scratch_shapes=[pltpu.VMEM((SSD_HEADS, bt, SSD_HEAD_DIM), F32),
                        pltpu.VMEM((SSD_HEADS, bt, SSD_HEAD_DIM), F32),
                        pltpu.VMEM((SSD_GROUPS, SSD_STATE, bt), F32),
                        pltpu.VMEM((SSD_GROUPS, SSD_STATE, bt), F32),
                        pltpu.VMEM((SSD_HEADS, bt, SSD_HEAD_DIM), F32),
                        pltpu.VMEM((bt, SSD_WIDTH), F32),
                        pltpu.VMEM((bt, D_MIX), F32)],
        input_output_aliases={2: 0},
        compiler_params=_params(("arbitrary", "arbitrary")),
        name="sample_mixer",
    )(proj, x_all, x_mixed, cosf, sinf, ssd, sbuf, ret, lruh, lbuf, *wts)


def _batcher_pairs(n):
    pairs = []

    def merge(lo, hi, r):
        step = r * 2
        if step < hi - lo:
            merge(lo, hi, step)
            merge(lo + r, hi, step)
            pairs.extend((i, i + r) for i in range(lo + r, hi - r, step))
        else:
            pairs.append((lo, lo + r))

    def sort(lo, hi):
        if hi - lo >= 1:
            mid = lo + (hi - lo) // 2
            sort(lo, mid)
            sort(mid + 1, hi)
            merge(lo, hi, 1)

    sort(0, n - 1)
    return pairs


_SORT16_PAIRS = _batcher_pairs(PEER_TOPK)


def _first(a, b):
    return (a[0] > b[0]) | ((a[0] == b[0]) & (a[1] < b[1]))


def _exchange(a, b):
    a_first = _first(a, b)
    hi = (jnp.maximum(a[0], b[0]),) + tuple(jnp.where(a_first, pa, pb) for pa, pb in zip(a[1:], b[1:]))
    lo = (jnp.minimum(a[0], b[0]),) + tuple(jnp.where(a_first, pb, pa) for pa, pb in zip(a[1:], b[1:]))
    return hi, lo


def _better(a, b):
    a_first = _first(a, b)
    return (jnp.maximum(a[0], b[0]),) + tuple(jnp.where(a_first, pa, pb) for pa, pb in zip(a[1:], b[1:]))


def _sort16(items):
    items = list(items)
    for i, j in _SORT16_PAIRS:
        items[i], items[j] = _exchange(items[i], items[j])
    return items


def _merge_top16(a, b):
    k = PEER_TOPK
    c = [_better(a[i], b[k - 1 - i]) for i in range(k)]
    d = k // 2
    while d >= 1:
        for i in range(k):
            if i & d == 0:
                c[i], c[i + d] = _exchange(c[i], c[i + d])
        d //= 2
    return c


def _top16_of_lists(vals, tags):
    k = PEER_TOPK
    best = None
    for g in range(0, len(vals), k):
        group = _sort16([(vals[i]() if callable(vals[i]) else vals[i], tags[i]) for i in range(g, g + k)])
        best = group if best is None else _merge_top16(best, group)
    return [b[0] for b in best], [b[1] for b in best]


ROUTE_TILE = SUBLANES * LANES
ROW_PITCH = PEER_KEYS + SUBLANES


def _peer_route_kernel(x_ref, n_ref, wq_ref, keys_ref,
                       xn_ref, ia_ref, ib_ref, gate_ref,
                       q_ref, z1_ref, z2_ref, za_ref, zb_ref, zg_ref):
    k = PEER_TOPK
    tile = (SUBLANES, LANES)
    xn = (_rms(x_ref[...]) * n_ref[...]).astype(BF16)
    xn_ref[...] = xn
    q = _dot(xn, wq_ref[...]).astype(BF16)
    for hc in range(2 * PEER_HEADS):
        q_ref[hc] = q[:, hc * PEER_HALF:(hc + 1) * PEER_HALF]

    def head(h, carry):
        for c, z_ref in ((0, z1_ref), (1, z2_ref)):
            keys = keys_ref[2 * h + c]
            for j in range(SUBLANES):
                z_ref[j * ROW_PITCH:j * ROW_PITCH + PEER_KEYS, :] = _dot_nt(
                    keys, q_ref[2 * h + c, j * LANES:(j + 1) * LANES, :])

        def top_keys(z_ref):
            load = lambda key: (lambda: z_ref[pl.ds(key, SUBLANES, stride=ROW_PITCH), :])
            return _top16_of_lists([load(key) for key in range(PEER_KEYS)],
                                   [jnp.full(tile, float(key), F32) for key in range(PEER_KEYS)])

        t1, i1 = top_keys(z1_ref)
        t2, i2 = top_keys(z2_ref)
        i1s = [v * float(PEER_KEYS) for v in i1]

        def pair(a, b):
            return (t1[a] + t2[b], jnp.full(tile, float(a * k + b), F32), i1s[a] + i2[b])

        pad = (jnp.full(tile, -jnp.inf, F32), jnp.full(tile, float(k * k), F32), jnp.zeros(tile, F32))
        row = lambda a: [pair(a, b) for b in range(k // (a + 1))]
        first = _merge_top16(row(0), row(1) + [pad] * (k - len(row(1))))
        second = _sort16(row(2) + row(3) + row(4) + row(5) + row(6))
        third_items = row(7) + [pair(a, 0) for a in range(8, k)]
        third = _sort16(third_items + [pad] * (k - len(third_items)))
        best = _merge_top16(_merge_top16(first, second), third)

        e = [jnp.exp(item[0] - best[0][0]) for item in best]
        z = e[0]
        for v in e[1:]:
            z = z + v
        inv = 1.0 / z
        for r in range(k):
            sel = h * k + r
            expert = best[r][2]
            a_idx = jnp.floor(expert * (1.0 / PEER_KEYS))
            za_ref[pl.ds(sel, SUBLANES, stride=ROW_PITCH), :] = a_idx
            zb_ref[pl.ds(sel, SUBLANES, stride=ROW_PITCH), :] = expert - a_idx * float(PEER_KEYS)
            zg_ref[pl.ds(sel, SUBLANES, stride=ROW_PITCH), :] = e[r] * inv
        return carry

    lax.fori_loop(0, PEER_HEADS, head, 0)
    for j in range(SUBLANES):
        rows = slice(j * LANES, (j + 1) * LANES)
        src = slice(j * ROW_PITCH, j * ROW_PITCH + PEER_HEADS * k)
        ia_ref[rows, :] = za_ref[src, :].T
        ib_ref[rows, :] = zb_ref[src, :].T
        gate_ref[rows, :] = zg_ref[src, :].T


def _peer_route(x, n2, wq, keys):
    t = x.shape[0]
    tm = ROUTE_TILE
    sel_spec = pl.BlockSpec((tm, PEER_HEADS * PEER_TOPK), lambda i: (i, 0))
    sel_shape = jax.ShapeDtypeStruct((t, PEER_HEADS * PEER_TOPK), F32)
    z_scratch = pltpu.VMEM((SUBLANES * ROW_PITCH, LANES), F32)
    return pl.pallas_call(
        _peer_route_kernel,
        grid=(pl.cdiv(t, tm),),
        in_specs=[pl.BlockSpec((tm, D_MODEL), lambda i: (i, 0)),
                  _const_spec((1, D_MODEL)),
                  _const_spec(wq.shape),
                  _const_spec(keys.shape)],
        out_specs=(pl.BlockSpec((tm, D_MODEL), lambda i: (i, 0)), sel_spec, sel_spec, sel_spec),
        out_shape=(jax.ShapeDtypeStruct((t, D_MODEL), BF16), sel_shape, sel_shape, sel_shape),
        scratch_shapes=[pltpu.VMEM((2 * PEER_HEADS, tm, PEER_HALF), BF16),
                        z_scratch, z_scratch, z_scratch, z_scratch, z_scratch],
        compiler_params=_params(("arbitrary",)),
        name="peer_route",
    )(x, n2, wq, keys)


def _peer_mlp_kernel(xn_ref, u_ref, v_ref, ia_ref, ib_ref, gate_ref, x_ref, fw_ref,
                     o_ref, acc_ref, act_ref, w_ref, g_ref, *, final_norm):
    e = pl.program_id(1)
    tm = xn_ref.shape[0]
    rows_per_tile = EXPERT_TILE // PEER_KEYS

    @pl.when(e == 0)
    def _():
        acc_ref[...] = jnp.zeros_like(acc_ref)
        key = lax.broadcasted_iota(jnp.int32, (PEER_KEYS, PEER_HEADS * PEER_TOPK), 0).astype(F32)

        def build(t, carry):
            gate = gate_ref[pl.ds(t, 1), :]
            sel_gate = jnp.where(key == ia_ref[pl.ds(t, 1), :], gate, 0.0)
            hi = sel_gate.astype(BF16)
            lo = (sel_gate - hi.astype(F32)).astype(BF16)
            onehot = jnp.where(key == ib_ref[pl.ds(t, 1), :], 1.0, 0.0).astype(BF16)
            g_ref[pl.ds(pl.multiple_of(t * ROW_PITCH, SUBLANES), PEER_KEYS), :] = _dot_nt(
                jnp.concatenate([hi, lo], axis=1), jnp.concatenate([onehot, onehot], axis=1))
            return carry

        lax.fori_loop(0, tm, build, 0, unroll=BUILD_UNROLL)

    act_ref[...] = _dot_nt(xn_ref[...], u_ref[...])
    for j in range(rows_per_tile):
        i1 = e * rows_per_tile + j
        cols = slice(j * PEER_KEYS, (j + 1) * PEER_KEYS)
        gate = g_ref[pl.ds(i1, tm, stride=ROW_PITCH), :]
        w_ref[:, cols] = (gate * _gelu(act_ref[:, cols])).astype(BF16)
    acc_ref[...] += _dot(w_ref[...], v_ref[...])

    @pl.when(e == pl.num_programs(1) - 1)
    def _():
        y = x_ref[...] + acc_ref[...]
        if final_norm:
            y = _rms(y) * fw_ref[...]
        o_ref[...] = y


def _peer_mlp(xn, u_bf, v_bf, route, x, fw, tm, final_norm):
    t = x.shape[0]
    ia, ib, gate = route
    tok_spec = lambda width: pl.BlockSpec((tm, width), lambda i, e: (i, 0))
    return pl.pallas_call(
        functools.partial(_peer_mlp_kernel, final_norm=final_norm),
        grid=(t // tm, PEER_EXPERTS // EXPERT_TILE),
        in_specs=[tok_spec(D_MODEL),
                  pl.BlockSpec((EXPERT_TILE, D_MODEL), lambda i, e: (e, 0)),
                  pl.BlockSpec((EXPERT_TILE, D_MODEL), lambda i, e: (e, 0)),
                  tok_spec(PEER_HEADS * PEER_TOPK), tok_spec(PEER_HEADS * PEER_TOPK),
                  tok_spec(PEER_HEADS * PEER_TOPK),
                  tok_spec(D_MODEL),
                  _const_spec((1, D_MODEL))],
        out_specs=tok_spec(D_MODEL),
        out_shape=jax.ShapeDtypeStruct((t, D_MODEL), F32),
        scratch_shapes=[pltpu.VMEM((tm, D_MODEL), F32),
                        pltpu.VMEM((tm, EXPERT_TILE), F32),
                        pltpu.VMEM((tm, EXPERT_TILE), BF16),
                        pltpu.VMEM((tm * ROW_PITCH, PEER_KEYS), F32)],
        compiler_params=_params(("arbitrary", "arbitrary")),
        name="peer_mlp",
    )(xn, u_bf, v_bf, ia, ib, gate, x, fw)


def _token_tile(t):
    for tm in (512, 384, 256, 128):
        if t % tm == 0:
            return tm
    raise ValueError(f"token count {t} must be a multiple of {LANES}")


def _rope_tables(pos):
    half = RET_HEAD_DIM // 2
    inv = ROPE_BASE ** (-jnp.arange(half, dtype=F32) / half)
    ang = pos.astype(F32)[:, None] * inv[None, :]
    cos = jnp.cos(ang)
    sin = jnp.sin(ang)
    return jnp.concatenate([cos, cos], axis=1), jnp.concatenate([-sin, sin], axis=1)


def _block_diag(w):
    eye = jnp.eye(LRU_BLOCKS, dtype=w.dtype)
    return jnp.einsum('hij,hg->higj', w, eye).reshape(LRU_WIDTH, LRU_WIDTH)


def kernel(x_prompt, x_sample, state_ssd, state_ssd_conv, state_ret, state_lru, state_lru_conv, norm1_w, w_in, ssd_conv_w, ssd_conv_b, ssd_dt_bias, ssd_a_log, ssd_d, ssd_norm_w, lru_conv_w, lru_conv_b, lru_wa, lru_ba, lru_wi, lru_bi, lru_lambda, w_out, norm2_w, peer_wq, peer_keys, peer_u, peer_v, final_norm_w):
    batch, seq, _ = x_prompt.shape
    n_sample, dec_seq, _ = x_sample.shape
    depth = w_in.shape[0]
    assert dec_seq == 1 and seq % SCAN_CHUNK == 0 and n_sample % SAMPLE_TILE == 0
    n_prompt = batch * seq
    t = n_prompt + n_sample
    tm = _token_tile(t)

    x = jnp.concatenate([x_prompt.reshape(n_prompt, D_MODEL), x_sample.reshape(n_sample, D_MODEL)], axis=0)
    cos_p, sin_p = _rope_tables(jnp.arange(seq))
    cos_s, sin_s = _rope_tables(PAST_LEN + jnp.arange(dec_seq))
    row = lambda a: a.reshape(1, -1).astype(F32)
    per_channel = lambda a: jnp.repeat(a.astype(F32), SSD_HEAD_DIM).reshape(1, SSD_WIDTH)
    per_head = lambda a: jnp.pad(a.astype(F32), (0, LANES - SSD_HEADS)).reshape(1, LANES)
    off = SSD_WIDTH + SSD_CONV_DIM

    new_p, new_s = [], []
    for i in range(depth):
        wi_ = w_in[i]
        dt_cols = wi_[:, off:off + SSD_HEADS]
        w_in_p = jnp.concatenate(
            [wi_[:, :off],
             jnp.pad(dt_cols, ((0, 0), (0, LANES - SSD_HEADS))),
             jnp.repeat(dt_cols, SSD_HEAD_DIM, axis=1),
             wi_[:, off + SSD_HEADS:]], axis=1).astype(BF16)
        wts = (ssd_conv_w[i], row(ssd_conv_b[i]), per_head(ssd_dt_bias[i]), per_head(ssd_a_log[i]),
               per_channel(ssd_dt_bias[i]), per_channel(ssd_a_log[i]), per_channel(ssd_d[i]),
               row(ssd_norm_w[i]), lru_conv_w[i], row(lru_conv_b[i]),
               _block_diag(lru_wa[i]).astype(BF16), row(lru_ba[i]),
               _block_diag(lru_wi[i]).astype(BF16), row(lru_bi[i]), row(lru_lambda[i]),
               w_out[i].astype(BF16))

        proj = _inproj(x, row(norm1_w[i]), w_in_p, tm)
        x_mixed, *st_p = _prompt_mixer(proj, x, cos_p, sin_p, wts, batch, seq)
        st_p[3] = st_p[3].reshape(batch, LRU_WIDTH)
        x, *st_s = _sample_mixer(proj, x, x_mixed, cos_s, sin_s,
                                 (state_ssd[i], state_ssd_conv[i], state_ret[i], state_lru[i],
                                  state_lru_conv[i]), wts, n_prompt, n_sample)
        new_p.append(st_p)
        new_s.append(st_s)

        keys_bf = peer_keys[i].reshape(PEER_HEADS * 2, PEER_KEYS, PEER_HALF).astype(BF16)
        xn, *route = _peer_route(x, row(norm2_w[i]), peer_wq[i].astype(BF16), keys_bf)
        x = _peer_mlp(xn, peer_u[i].astype(BF16), peer_v[i].astype(BF16), route, x,
                      row(final_norm_w), tm, final_norm=(i == depth - 1))

    y_prompt = x[:n_prompt].reshape(batch, seq, D_MODEL)
    y_sample = x[n_prompt:].reshape(n_sample, dec_seq, D_MODEL)
    stk = lambda states, j: jnp.stack([s[j] for s in states])
    return (y_prompt, y_sample,
            stk(new_p, 0), stk(new_p, 1), stk(new_p, 2), stk(new_p, 3), stk(new_p, 4),
            stk(new_s, 0), stk(new_s, 1), stk(new_s, 2), stk(new_s, 3), stk(new_s, 4))
```

```python
import functools

import numpy as np
import jax
import jax.numpy as jnp
from jax import lax
from jax.experimental import pallas as pl
from jax.experimental.pallas import tpu as pltpu

F32 = jnp.float32
BF16 = jnp.bfloat16

D_MODEL = 1024
PAST_LEN = 16384
D_MIX = 2 * D_MODEL
CONV_WIDTH = 4
NORM_EPS = 1e-6
SSD_WIDTH = D_MIX // 2
SSD_HEAD_DIM = 64
SSD_HEADS = SSD_WIDTH // SSD_HEAD_DIM
SSD_GROUPS = 2
SSD_HPG = SSD_HEADS // SSD_GROUPS
SSD_STATE = 128
SSD_CONV_DIM = SSD_WIDTH + 2 * SSD_GROUPS * SSD_STATE
RET_WIDTH = D_MIX // 4
RET_HEADS = 4
RET_HEAD_DIM = RET_WIDTH // RET_HEADS
ROPE_BASE = 10000.0
LRU_WIDTH = D_MIX // 4
LRU_BLOCKS = 8
LRU_BLOCK_DIM = LRU_WIDTH // LRU_BLOCKS
LRU_C = 8.0
PEER_KEYS = 128
PEER_EXPERTS = PEER_KEYS * PEER_KEYS
PEER_HEADS = 8
PEER_TOPK = 16
PEER_QUERY_DIM = 256
PEER_HALF = PEER_QUERY_DIM // 2

LANES = 128
SUBLANES = 8
VMEM_LIMIT = 56 * 1024 * 1024

COL_Z = 0
COL_XBC = COL_Z + SSD_WIDTH
COL_DT16 = COL_XBC + SSD_CONV_DIM
COL_DTX = COL_DT16 + LANES
COL_Q = COL_DTX + SSD_WIDTH
COL_K = COL_Q + RET_WIDTH
COL_V = COL_K + RET_WIDTH
COL_G = COL_V + RET_WIDTH
COL_GATE = COL_G + RET_WIDTH
COL_XR = COL_GATE + LRU_WIDTH
PROJ_WIDTH = COL_XR + LRU_WIDTH

SCAN_CHUNK = 128
SAMPLE_TILE = SUBLANES
SAMPLE_HEAD_BLOCK = 4
EXPERT_TILE = 1024
BUILD_UNROLL = 8
LOG_GAMMA = [float(np.log(1.0 - 2.0 ** (-5.0 - h))) for h in range(RET_HEADS)]

_NT = (((1,), (1,)), ((), ()))


def _rms(x):
    return x * lax.rsqrt(jnp.mean(x * x, axis=-1, keepdims=True) + NORM_EPS)


def _softplus(x):
    return jnp.maximum(x, 0.0) + jnp.log1p(jnp.exp(-jnp.abs(x)))


def _silu(x):
    return x * jax.nn.sigmoid(x)


def _gelu(x):
    return 0.5 * x * (1.0 + lax.erf(x * np.float32(np.sqrt(0.5))))


def _neg_expm1(x):
    return -jnp.tanh(0.5 * x) * (jnp.exp(x) + 1.0)


def _dot(a, b):
    return jnp.dot(a, b, preferred_element_type=F32)


def _dot_nt(a, b):
    return lax.dot_general(a, b, _NT, preferred_element_type=F32)


def _dot_exact(a, b):
    return jnp.dot(a, b, preferred_element_type=F32, precision=lax.Precision.HIGHEST)


def _params(semantics):
    return pltpu.CompilerParams(dimension_semantics=semantics, vmem_limit_bytes=VMEM_LIMIT)


def _const_spec(shape):
    zeros = (0,) * len(shape)
    return pl.BlockSpec(shape, lambda *_: zeros)


def _inproj_kernel(x_ref, n_ref, w_ref, o_ref):
    h = (_rms(x_ref[...]) * n_ref[...]).astype(BF16)
    step = 8 * LANES
    for j in range(0, PROJ_WIDTH, step):
        hi = min(j + step, PROJ_WIDTH)
        o_ref[:, j:hi] = _dot(h, w_ref[:, j:hi])


def _inproj(x, n1, w_in_p, tm):
    t = x.shape[0]
    return pl.pallas_call(
        _inproj_kernel,
        grid=(t // tm,),
        in_specs=[pl.BlockSpec((tm, D_MODEL), lambda i: (i, 0)),
                  _const_spec((1, D_MODEL)),
                  pl.BlockSpec((D_MODEL, PROJ_WIDTH), lambda i: (0, 0), pipeline_mode=pl.Buffered(1))],
        out_specs=pl.BlockSpec((tm, PROJ_WIDTH), lambda i: (i, 0)),
        out_shape=jax.ShapeDtypeStruct((t, PROJ_WIDTH), F32),
        compiler_params=_params(("arbitrary",)),
        name="inproj",
    )(x, n1, w_in_p)


def _prompt_mixer_kernel(proj_ref, x_ref, cos_ref, sin_ref, scw_ref, scb_ref, dtb16_ref, alog16_ref,
                         dtbx_ref, alogx_ref, dskx_ref, snw_ref, lcw_ref, lcb_ref, wa_ref, ba_ref,
                         wi_ref, bi_ref, lam_ref, wout_ref,
                         xo_ref, ssd_ref, sbuf_ref, ret_ref, lruh_ref, lbuf_ref,
                         xps_ref, xpl_ref, mix_ref):
    c = SCAN_CHUNK
    step = pl.program_id(1)
    pad = SUBLANES
    keep = CONV_WIDTH - 1

    @pl.when(step == 0)
    def _():
        ssd_ref[...] = jnp.zeros_like(ssd_ref)
        ret_ref[...] = jnp.zeros_like(ret_ref)
        lruh_ref[...] = jnp.zeros_like(lruh_ref)
        xps_ref[0:pad, :] = jnp.zeros((pad, SSD_CONV_DIM), F32)
        xpl_ref[0:pad, :] = jnp.zeros((pad, LRU_WIDTH), F32)

    def conv(xp_ref, x_new, w_ref, b_ref, buf_out_ref):
        xp_ref[pad:pad + c, :] = x_new
        y = b_ref[...] + xp_ref[pad - keep:pad - keep + c, :] * w_ref[0:1, :]
        for tap in range(1, CONV_WIDTH):
            y = y + xp_ref[pad - keep + tap:pad - keep + tap + c, :] * w_ref[tap:tap + 1, :]
        tail = xp_ref[pad + c - keep:pad + c, :]
        xp_ref[pad - keep:pad, :] = tail
        buf_out_ref[0] = tail
        return y

    row = lax.broadcasted_iota(jnp.int32, (c, c), 0)
    col = lax.broadcasted_iota(jnp.int32, (c, c), 1)
    causal = row >= col
    tri = causal.astype(F32)
    rowf = lax.broadcasted_iota(jnp.int32, (c, LANES), 0).astype(F32)

    xbc = _silu(conv(xps_ref, proj_ref[:, COL_XBC:COL_XBC + SSD_CONV_DIM], scw_ref, scb_ref, sbuf_ref))
    xs = xbc[:, :SSD_WIDTH]
    bm = xbc[:, SSD_WIDTH:SSD_WIDTH + SSD_GROUPS * SSD_STATE]
    cm = xbc[:, SSD_WIDTH + SSD_GROUPS * SSD_STATE:]
    dtx = _softplus(proj_ref[:, COL_DTX:COL_DTX + SSD_WIDTH] + dtbx_ref[...])
    cumx = _dot_exact(tri, -dtx * jnp.exp(alogx_ref[...]))
    dt16 = _softplus(proj_ref[:, COL_DT16:COL_DT16 + LANES] + dtb16_ref[...])
    cum16 = _dot_exact(tri, -dt16 * jnp.exp(alog16_ref[...]))
    cum16_t = cum16.T
    v = xs * dtx
    v_bf = v.astype(BF16)
    cum_last = cumx[c - 1:c, :]
    vte_bf = (v * jnp.exp(cum_last - cumx)).astype(BF16)
    ecum = jnp.exp(cumx)
    chunk_decay = jnp.exp(cum_last)
    for g in range(SSD_GROUPS):
        bmg = bm[:, g * SSD_STATE:(g + 1) * SSD_STATE]
        cmg_bf = cm[:, g * SSD_STATE:(g + 1) * SSD_STATE].astype(BF16)
        scores = _dot_nt(cmg_bf, bmg.astype(BF16))
        bmg_t_bf = bmg.T.astype(BF16)
        for r in range(SSD_HPG):
            h = g * SSD_HPG + r
            hs = slice(h * SSD_HEAD_DIM, (h + 1) * SSD_HEAD_DIM)
            seg = cum16[:, h:h + 1] - cum16_t[h:h + 1, :]
            decay = jnp.exp(jnp.where(causal, seg, -jnp.inf))
            y_intra = _dot((scores * decay).astype(BF16), v_bf[:, hs])
            s_in = ssd_ref[0, h]
            y_inter = _dot(cmg_bf, s_in.astype(BF16)) * ecum[:, hs]
            ssd_ref[0, h] = s_in * chunk_decay[:, hs] + _dot(bmg_t_bf, vte_bf[:, hs])
            mix_ref[:, hs] = y_intra + y_inter
    y = mix_ref[:, :SSD_WIDTH] + dskx_ref[...] * xs
    y = y * _silu(proj_ref[:, COL_Z:COL_Z + SSD_WIDTH])
    mix_ref[:, :SSD_WIDTH] = _rms(y) * snw_ref[...]

    cosf = cos_ref[...]
    sinf = sin_ref[...]
    rowcol = (row - col).astype(F32)
    for h in range(RET_HEADS):
        lg = LOG_GAMMA[h]
        hs = slice(h * RET_HEAD_DIM, (h + 1) * RET_HEAD_DIM)
        qh = proj_ref[:, COL_Q + h * RET_HEAD_DIM:COL_Q + (h + 1) * RET_HEAD_DIM]
        kh = proj_ref[:, COL_K + h * RET_HEAD_DIM:COL_K + (h + 1) * RET_HEAD_DIM]
        vh = proj_ref[:, COL_V + h * RET_HEAD_DIM:COL_V + (h + 1) * RET_HEAD_DIM]
        gh = proj_ref[:, COL_G + h * RET_HEAD_DIM:COL_G + (h + 1) * RET_HEAD_DIM]
        qh = qh * cosf + pltpu.roll(qh, RET_HEAD_DIM // 2, 1) * sinf
        kh = (kh * cosf + pltpu.roll(kh, RET_HEAD_DIM // 2, 1) * sinf) * (RET_HEAD_DIM ** -0.5)
        qh_bf = qh.astype(BF16)
        scores = _dot_nt(qh_bf, kh.astype(BF16))
        decay = jnp.exp(jnp.where(causal, rowcol * lg, -jnp.inf))
        y_intra = _dot((scores * decay).astype(BF16), vh.astype(BF16))
        s_in = ret_ref[0, h]
        y_inter = _dot(qh_bf, s_in.astype(BF16)) * jnp.exp((rowf + 1.0) * lg)
        v_to_end = (vh * jnp.exp((c - 1.0 - rowf) * lg)).astype(BF16)
        ret_ref[0, h] = s_in * float(np.exp(c * lg)) + _dot(kh.T.astype(BF16), v_to_end)
        mix_ref[:, SSD_WIDTH + h * RET_HEAD_DIM:SSD_WIDTH + (h + 1) * RET_HEAD_DIM] = (
            _rms(y_intra + y_inter) * _silu(gh))

    xc = conv(xpl_ref, proj_ref[:, COL_XR:COL_XR + LRU_WIDTH], lcw_ref, lcb_ref, lbuf_ref)
    xc_bf = xc.astype(BF16)
    r_gate = jax.nn.sigmoid(_dot(xc_bf, wa_ref[...]) + ba_ref[...])
    i_gate = jax.nn.sigmoid(_dot(xc_bf, wi_ref[...]) + bi_ref[...])
    log_a = -LRU_C * r_gate * _softplus(-lam_ref[...])
    a_run = jnp.exp(log_a)
    b_run = jnp.sqrt(_neg_expm1(2.0 * log_a)) * (i_gate * xc)
    rowl = lax.broadcasted_iota(jnp.int32, (c, LRU_WIDTH), 0)
    shift = 1
    while shift < c:
        valid = rowl >= shift
        a_prev = pltpu.roll(a_run, shift, 0)
        b_prev = pltpu.roll(b_run, shift, 0)
        b_run = jnp.where(valid, a_run * b_prev + b_run, b_run)
        a_run = jnp.where(valid, a_run * a_prev, a_run)
        shift *= 2
    h_all = b_run + a_run * lruh_ref[0]
    lruh_ref[0] = h_all[c - 1:c, :]
    mix_ref[:, SSD_WIDTH + RET_WIDTH:] = h_all * _gelu(proj_ref[:, COL_GATE:COL_GATE + LRU_WIDTH])

    xo_ref[...] = x_ref[...] + _dot(mix_ref[...].astype(BF16), wout_ref[...])


def _prompt_mixer(proj, x_all, cosf, sinf, wts, batch, seq):
    c = SCAN_CHUNK
    nc = seq // c
    t = x_all.shape[0]
    row_spec = lambda width: pl.BlockSpec((c, width), lambda b, s: (b * nc + s, 0))
    state_spec = lambda shape: pl.BlockSpec((1,) + shape, lambda b, s: (b,) + (0,) * len(shape))
    in_specs = [row_spec(PROJ_WIDTH), row_spec(D_MODEL),
                pl.BlockSpec((c, LANES), lambda b, s: (s, 0)),
                pl.BlockSpec((c, LANES), lambda b, s: (s, 0))]
    in_specs += [_const_spec(w.shape) for w in wts]
    out_shapes = (jax.ShapeDtypeStruct((t, D_MODEL), F32),
                  jax.ShapeDtypeStruct((batch, SSD_HEADS, SSD_STATE, SSD_HEAD_DIM), F32),
                  jax.ShapeDtypeStruct((batch, CONV_WIDTH - 1, SSD_CONV_DIM), F32),
                  jax.ShapeDtypeStruct((batch, RET_HEADS, RET_HEAD_DIM, RET_HEAD_DIM), F32),
                  jax.ShapeDtypeStruct((batch, 1, LRU_WIDTH), F32),
                  jax.ShapeDtypeStruct((batch, CONV_WIDTH - 1, LRU_WIDTH), F32))
    out_specs = (row_spec(D_MODEL),
                 state_spec((SSD_HEADS, SSD_STATE, SSD_HEAD_DIM)),
                 state_spec((CONV_WIDTH - 1, SSD_CONV_DIM)),
                 state_spec((RET_HEADS, RET_HEAD_DIM, RET_HEAD_DIM)),
                 state_spec((1, LRU_WIDTH)),
                 state_spec((CONV_WIDTH - 1, LRU_WIDTH)))
    return pl.pallas_call(
        _prompt_mixer_kernel,
        grid=(batch, nc),
        in_specs=in_specs,
        out_specs=out_specs,
        out_shape=out_shapes,
        scratch_shapes=[pltpu.VMEM((SUBLANES + c, SSD_CONV_DIM), F32),
                        pltpu.VMEM((SUBLANES + c, LRU_WIDTH), F32),
                        pltpu.VMEM((c, D_MIX), F32)],
        compiler_params=_params(("arbitrary", "arbitrary")),
        name="prompt_mixer",
    )(proj, x_all, cosf, sinf, *wts)


def _sample_mixer_kernel(proj_ref, x_ref, cos_ref, sin_ref, ssd_in_ref, sbuf_in_ref, ret_in_ref,
                         lruh_in_ref, lbuf_in_ref, scw_ref, scb_ref, dtb16_ref, alog16_ref,
                         dtbx_ref, alogx_ref, dskx_ref, snw_ref, lcw_ref, lcb_ref, wa_ref, ba_ref,
                         wi_ref, bi_ref, lam_ref, wout_ref, *rest, n_aliased):
    (xo_ref, ssd_ref, sbuf_ref, ret_ref, lruh_ref, lbuf_ref,
     vh_ref, ah_ref, bmt_ref, cmt_ref, yh_ref, xs_ref, mix_ref) = rest[n_aliased:]
    del dtb16_ref, alog16_ref
    bt = SAMPLE_TILE
    hb = pl.program_id(1)
    nhb = pl.num_programs(1)
    keep = CONV_WIDTH - 1

    def conv(x_new, buf_in_ref, w_ref, b_ref, buf_out_ref):
        y = b_ref[...] + x_new * w_ref[keep:keep + 1, :]
        for tap in range(keep):
            y = y + buf_in_ref[:, tap, :] * w_ref[tap:tap + 1, :]
        for tap in range(keep - 1):
            buf_out_ref[:, tap, :] = buf_in_ref[:, tap + 1, :]
        buf_out_ref[:, keep - 1, :] = x_new
        return y

    @pl.when(hb == 0)
    def _():
        xbc = _silu(conv(proj_ref[:, COL_XBC:COL_XBC + SSD_CONV_DIM], sbuf_in_ref, scw_ref, scb_ref, sbuf_ref))
        xs = xbc[:, :SSD_WIDTH]
        xs_ref[...] = xs
        dtx = _softplus(proj_ref[:, COL_DTX:COL_DTX + SSD_WIDTH] + dtbx_ref[...])
        a = jnp.exp(-dtx * jnp.exp(alogx_ref[...]))
        v = xs * dtx
        for h in range(SSD_HEADS):
            hs = slice(h * SSD_HEAD_DIM, (h + 1) * SSD_HEAD_DIM)
            vh_ref[h] = v[:, hs]
            ah_ref[h] = a[:, hs]
        for g in range(SSD_GROUPS):
            bmt_ref[g] = xbc[:, SSD_WIDTH + g * SSD_STATE:SSD_WIDTH + (g + 1) * SSD_STATE].T
            off = SSD_WIDTH + SSD_GROUPS * SSD_STATE
            cmt_ref[g] = xbc[:, off + g * SSD_STATE:off + (g + 1) * SSD_STATE].T

    g = (hb * SAMPLE_HEAD_BLOCK) // SSD_HPG
    bmt = bmt_ref[g]
    cmt = cmt_ref[g]
    for r in range(SAMPLE_HEAD_BLOCK):
        h = hb * SAMPLE_HEAD_BLOCK + r
        vh = vh_ref[h]
        ah = ah_ref[h]
        for b in range(bt):
            s_new = ssd_in_ref[b, r] * ah[b:b + 1, :] + bmt[:, b:b + 1] * vh[b:b + 1, :]
            ssd_ref[b, r] = s_new
            yh_ref[h, b:b + 1, :] = jnp.sum(cmt[:, b:b + 1] * s_new, axis=0, keepdims=True)

    @pl.when(hb == nhb - 1)
    def _():
        for h in range(SSD_HEADS):
            mix_ref[:, h * SSD_HEAD_DIM:(h + 1) * SSD_HEAD_DIM] = yh_ref[h]
        y = mix_ref[:, :SSD_WIDTH] + dskx_ref[...] * xs_ref[...]
        y = y * _silu(proj_ref[:, COL_Z:COL_Z + SSD_WIDTH])
        mix_ref[:, :SSD_WIDTH] = _rms(y) * snw_ref[...]

        cosf = cos_ref[...]
        sinf = sin_ref[...]
        for h in range(RET_HEADS):
            lg = LOG_GAMMA[h]
            qh = proj_ref[:, COL_Q + h * RET_HEAD_DIM:COL_Q + (h + 1) * RET_HEAD_DIM]
            kh = proj_ref[:, COL_K + h * RET_HEAD_DIM:COL_K + (h + 1) * RET_HEAD_DIM]
            vh = proj_ref[:, COL_V + h * RET_HEAD_DIM:COL_V + (h + 1) * RET_HEAD_DIM]
            gh = proj_ref[:, COL_G + h * RET_HEAD_DIM:COL_G + (h + 1) * RET_HEAD_DIM]
            qh = qh * cosf + pltpu.roll(qh, RET_HEAD_DIM // 2, 1) * sinf
            kh = (kh * cosf + pltpu.roll(kh, RET_HEAD_DIM // 2, 1) * sinf) * (RET_HEAD_DIM ** -0.5)
            qt = qh.T
            kt = kh.T
            hs = slice(SSD_WIDTH + h * RET_HEAD_DIM, SSD_WIDTH + (h + 1) * RET_HEAD_DIM)
            for b in range(bt):
                s_new = ret_in_ref[b, h] * float(np.exp(lg)) + kt[:, b:b + 1] * vh[b:b + 1, :]
                ret_ref[b, h] = s_new
                mix_ref[b:b + 1, hs] = jnp.sum(qt[:, b:b + 1] * s_new, axis=0, keepdims=True)
            mix_ref[:, hs] = _rms(mix_ref[:, hs]) * _silu(gh)

        xc = conv(proj_ref[:, COL_XR:COL_XR + LRU_WIDTH], lbuf_in_ref, lcw_ref, lcb_ref, lbuf_ref)
        xc_bf = xc.astype(BF16)
        r_gate = jax.nn.sigmoid(_dot(xc_bf, wa_ref[...]) + ba_ref[...])
        i_gate = jax.nn.sigmoid(_dot(xc_bf, wi_ref[...]) + bi_ref[...])
        log_a = -LRU_C * r_gate * _softplus(-lam_ref[...])
        h_new = (jnp.exp(log_a) * lruh_in_ref[...]
                 + jnp.sqrt(_neg_expm1(2.0 * log_a)) * (i_gate * xc))
        lruh_ref[...] = h_new
        mix_ref[:, SSD_WIDTH + RET_WIDTH:] = h_new * _gelu(proj_ref[:, COL_GATE:COL_GATE + LRU_WIDTH])

        xo_ref[...] = x_ref[...] + _dot(mix_ref[...].astype(BF16), wout_ref[...])


def _sample_mixer(proj, x_all, x_mixed, cosf, sinf, states, new_states, wts, layer, n_prompt, n_sample):
    bt = SAMPLE_TILE
    hbk = SAMPLE_HEAD_BLOCK
    base = n_prompt // bt
    t = x_all.shape[0]
    row_spec = lambda width: pl.BlockSpec((bt, width), lambda i, j: (base + i, 0))
    ssd_spec = pl.BlockSpec((None, bt, hbk, SSD_STATE, SSD_HEAD_DIM), lambda i, j: (layer, i, j, 0, 0))
    sbuf_spec = pl.BlockSpec((None, bt, CONV_WIDTH - 1, SSD_CONV_DIM), lambda i, j: (layer, i, 0, 0))
    ret_spec = pl.BlockSpec((None, bt, RET_HEADS, RET_HEAD_DIM, RET_HEAD_DIM), lambda i, j: (layer, i, 0, 0, 0))
    lruh_spec = pl.BlockSpec((None, bt, LRU_WIDTH), lambda i, j: (layer, i, 0))
    lbuf_spec = pl.BlockSpec((None, bt, CONV_WIDTH - 1, LRU_WIDTH), lambda i, j: (layer, i, 0, 0))
    state_specs = [ssd_spec, sbuf_spec, ret_spec, lruh_spec, lbuf_spec]
    aliased = (x_mixed,) + (tuple(new_states) if new_states is not None else ())
    in_specs = [row_spec(PROJ_WIDTH), row_spec(D_MODEL),
                _const_spec((1, LANES)), _const_spec((1, LANES))] + state_specs
    in_specs += [_const_spec(w.shape) for w in wts]
    n_in = len(in_specs)
    in_specs += [pl.BlockSpec(memory_space=pl.ANY)] * len(aliased)
    out_shapes = (jax.ShapeDtypeStruct((t, D_MODEL), F32),) + tuple(
        jax.ShapeDtypeStruct(s.shape, F32) for s in states)
    out_specs = (row_spec(D_MODEL),) + tuple(state_specs)
    return pl.pallas_call(
        functools.partial(_sample_mixer_kernel, n_aliased=len(aliased)),
        grid=(n_sample // bt, SSD_HEADS // hbk),
        in_specs=in_specs,
        out_specs=out_specs,
        out_shape=out_shapes,
        scratch_shapes=[pltpu.VMEM((SSD_HEADS, bt, SSD_HEAD_DIM), F32),
                        pltpu.VMEM((SSD_HEADS, bt, SSD_HEAD_DIM), F32),
                        pltpu.VMEM((SSD_GROUPS, SSD_STATE, bt), F32),
                        pltpu.VMEM((SSD_GROUPS, SSD_STATE, bt), F32),
                        pltpu.VMEM((SSD_HEADS, bt, SSD_HEAD_DIM), F32),
                        pltpu.VMEM((bt, SSD_WIDTH), F32),
                        pltpu.VMEM((bt, D_MIX), F32)],
        input_output_aliases={n_in + k: k for k in range(len(aliased))},
        compiler_params=_params(("arbitrary", "arbitrary")),
        name="sample_mixer",
    )(proj, x_all, cosf, sinf, *states, *wts, *aliased)


def _batcher_pairs(n):
    pairs = []

    def merge(lo, hi, r):
        step = r * 2
        if step < hi - lo:
            merge(lo, hi, step)
            merge(lo + r, hi, step)
            pairs.extend((i, i + r) for i in range(lo + r, hi - r, step))
        else:
            pairs.append((lo, lo + r))

    def sort(lo, hi):
        if hi - lo >= 1:
            mid = lo + (hi - lo) // 2
            sort(lo, mid)
            sort(mid + 1, hi)
            merge(lo, hi, 1)

    sort(0, n - 1)
    return pairs


_SORT16_PAIRS = _batcher_pairs(PEER_TOPK)


def _first(a, b):
    return (a[0] > b[0]) | ((a[0] == b[0]) & (a[1] < b[1]))


def _exchange(a, b):
    a_first = _first(a, b)
    hi = (jnp.maximum(a[0], b[0]),) + tuple(jnp.where(a_first, pa, pb) for pa, pb in zip(a[1:], b[1:]))
    lo = (jnp.minimum(a[0], b[0]),) + tuple(jnp.where(a_first, pb, pa) for pa, pb in zip(a[1:], b[1:]))
    return hi, lo


def _better(a, b):
    a_first = _first(a, b)
    return (jnp.maximum(a[0], b[0]),) + tuple(jnp.where(a_first, pa, pb) for pa, pb in zip(a[1:], b[1:]))


def _sort16(items):
    items = list(items)
    for i, j in _SORT16_PAIRS:
        items[i], items[j] = _exchange(items[i], items[j])
    return items


def _merge_top16(a, b):
    k = PEER_TOPK
    c = [_better(a[i], b[k - 1 - i]) for i in range(k)]
    d = k // 2
    while d >= 1:
        for i in range(k):
            if i & d == 0:
                c[i], c[i + d] = _exchange(c[i], c[i + d])
        d //= 2
    return c


def _top16_of_lists(vals, tags):
    k = PEER_TOPK
    best = None
    for g in range(0, len(vals), k):
        group = _sort16([(vals[i]() if callable(vals[i]) else vals[i], tags[i]) for i in range(g, g + k)])
        best = group if best is None else _merge_top16(best, group)
    return [b[0] for b in best], [b[1] for b in best]


ROUTE_TILE = SUBLANES * LANES
ROW_PITCH = PEER_KEYS + SUBLANES


def _peer_route_kernel(x_ref, n_ref, wq_ref, keys_ref,
                       xn_ref, ia_ref, ib_ref, gate_ref,
                       q_ref, z1_ref, z2_ref, za_ref, zb_ref, zg_ref):
    k = PEER_TOPK
    tile = (SUBLANES, LANES)
    xn = (_rms(x_ref[...]) * n_ref[...]).astype(BF16)
    xn_ref[...] = xn
    q = _dot(xn, wq_ref[...]).astype(BF16)
    for hc in range(2 * PEER_HEADS):
        q_ref[hc] = q[:, hc * PEER_HALF:(hc + 1) * PEER_HALF]

    def head(h, carry):
        for c, z_ref in ((0, z1_ref), (1, z2_ref)):
            keys = keys_ref[2 * h + c]
            for j in range(SUBLANES):
                z_ref[j * ROW_PITCH:j * ROW_PITCH + PEER_KEYS, :] = _dot_nt(
                    keys, q_ref[2 * h + c, j * LANES:(j + 1) * LANES, :])

        def top_keys(z_ref):
            load = lambda key: (lambda: z_ref[pl.ds(key, SUBLANES, stride=ROW_PITCH), :])
            return _top16_of_lists([load(key) for key in range(PEER_KEYS)],
                                   [jnp.full(tile, float(key), F32) for key in range(PEER_KEYS)])

        t1, i1 = top_keys(z1_ref)
        t2, i2 = top_keys(z2_ref)
        i1s = [v * float(PEER_KEYS) for v in i1]

        def pair(a, b):
            return (t1[a] + t2[b], jnp.full(tile, float(a * k + b), F32), i1s[a] + i2[b])

        pad = (jnp.full(tile, -jnp.inf, F32), jnp.full(tile, float(k * k), F32), jnp.zeros(tile, F32))
        row = lambda a: [pair(a, b) for b in range(k // (a + 1))]
        first = _merge_top16(row(0), row(1) + [pad] * (k - len(row(1))))
        second = _sort16(row(2) + row(3) + row(4) + row(5) + row(6))
        third_items = row(7) + [pair(a, 0) for a in range(8, k)]
        third = _sort16(third_items + [pad] * (k - len(third_items)))
        best = _merge_top16(_merge_top16(first, second), third)

        e = [jnp.exp(item[0] - best[0][0]) for item in best]
        z = e[0]
        for v in e[1:]:
            z = z + v
        inv = 1.0 / z
        for r in range(k):
            sel = h * k + r
            expert = best[r][2]
            a_idx = jnp.floor(expert * (1.0 / PEER_KEYS))
            za_ref[pl.ds(sel, SUBLANES, stride=ROW_PITCH), :] = a_idx
            zb_ref[pl.ds(sel, SUBLANES, stride=ROW_PITCH), :] = expert - a_idx * float(PEER_KEYS)
            zg_ref[pl.ds(sel, SUBLANES, stride=ROW_PITCH), :] = e[r] * inv
        return carry

    lax.fori_loop(0, PEER_HEADS, head, 0)
    for j in range(SUBLANES):
        rows = slice(j * LANES, (j + 1) * LANES)
        src = slice(j * ROW_PITCH, j * ROW_PITCH + PEER_HEADS * k)
        ia_ref[rows, :] = za_ref[src, :].T
        ib_ref[rows, :] = zb_ref[src, :].T
        gate_ref[rows, :] = zg_ref[src, :].T


def _peer_route(x, n2, wq, keys):
    t = x.shape[0]
    tm = ROUTE_TILE
    sel_spec = pl.BlockSpec((tm, PEER_HEADS * PEER_TOPK), lambda i: (i, 0))
    sel_shape = jax.ShapeDtypeStruct((t, PEER_HEADS * PEER_TOPK), F32)
    z_scratch = pltpu.VMEM((SUBLANES * ROW_PITCH, LANES), F32)
    return pl.pallas_call(
        _peer_route_kernel,
        grid=(pl.cdiv(t, tm),),
        in_specs=[pl.BlockSpec((tm, D_MODEL), lambda i: (i, 0)),
                  _const_spec((1, D_MODEL)),
                  _const_spec(wq.shape),
                  _const_spec(keys.shape)],
        out_specs=(pl.BlockSpec((tm, D_MODEL), lambda i: (i, 0)), sel_spec, sel_spec, sel_spec),
        out_shape=(jax.ShapeDtypeStruct((t, D_MODEL), BF16), sel_shape, sel_shape, sel_shape),
        scratch_shapes=[pltpu.VMEM((2 * PEER_HEADS, tm, PEER_HALF), BF16),
                        z_scratch, z_scratch, z_scratch, z_scratch, z_scratch],
        compiler_params=_params(("arbitrary",)),
        name="peer_route",
    )(x, n2, wq, keys)


def _peer_mlp_kernel(xn_ref, u_ref, v_ref, ia_ref, ib_ref, gate_ref, x_ref, fw_ref, *rest,
                     final, n_prompt_tiles, n_sample):
    if final:
        o_ref, os_ref, acc_ref, act_ref, w_ref, g_ref = rest
    else:
        o_ref, acc_ref, act_ref, w_ref, g_ref = rest
    e = pl.program_id(1)
    tm = xn_ref.shape[0]
    rows_per_tile = EXPERT_TILE // PEER_KEYS

    @pl.when(e == 0)
    def _():
        acc_ref[...] = jnp.zeros_like(acc_ref)
        key = lax.broadcasted_iota(jnp.int32, (PEER_KEYS, PEER_HEADS * PEER_TOPK), 0).astype(F32)

        def build(t, carry):
            gate = gate_ref[pl.ds(t, 1), :]
            sel_gate = jnp.where(key == ia_ref[pl.ds(t, 1), :], gate, 0.0)
            hi = sel_gate.astype(BF16)
            lo = (sel_gate - hi.astype(F32)).astype(BF16)
            onehot = jnp.where(key == ib_ref[pl.ds(t, 1), :], 1.0, 0.0).astype(BF16)
            g_ref[pl.ds(pl.multiple_of(t * ROW_PITCH, SUBLANES), PEER_KEYS), :] = _dot_nt(
                jnp.concatenate([hi, lo], axis=1), jnp.concatenate([onehot, onehot], axis=1))
            return carry

        lax.fori_loop(0, tm, build, 0, unroll=BUILD_UNROLL)

    act_ref[...] = _dot_nt(xn_ref[...], u_ref[...])
    for j in range(rows_per_tile):
        i1 = e * rows_per_tile + j
        cols = slice(j * PEER_KEYS, (j + 1) * PEER_KEYS)
        gate = g_ref[pl.ds(i1, tm, stride=ROW_PITCH), :]
        w_ref[:, cols] = (gate * _gelu(act_ref[:, cols])).astype(BF16)
    acc_ref[...] += _dot(w_ref[...], v_ref[...])

    @pl.when(e == pl.num_programs(1) - 1)
    def _():
        y = x_ref[...] + acc_ref[...]
        if not final:
            o_ref[...] = y
            return
        y = _rms(y) * fw_ref[...]
        tile = pl.program_id(0)

        @pl.when(tile < n_prompt_tiles)
        def _():
            o_ref[...] = y

        @pl.when(tile == pl.num_programs(0) - 1)
        def _():
            os_ref[...] = y[tm - n_sample:, :]


def _peer_mlp(xn, u_bf, v_bf, route, x, fw, tm, final, n_prompt):
    t = x.shape[0]
    n_sample = t - n_prompt
    assert n_sample <= tm and n_sample % SUBLANES == 0
    n_prompt_tiles = pl.cdiv(n_prompt, tm)
    ia, ib, gate = route
    tok_spec = lambda width: pl.BlockSpec((tm, width), lambda i, e: (i, 0))
    if final:
        out_specs = (pl.BlockSpec((tm, D_MODEL), lambda i, e: (jnp.minimum(i, n_prompt_tiles - 1), 0)),
                     pl.BlockSpec((n_sample, D_MODEL), lambda i, e: (0, 0)))
        out_shape = (jax.ShapeDtypeStruct((n_prompt, D_MODEL), F32),
                     jax.ShapeDtypeStruct((n_sample, D_MODEL), F32))
    else:
        out_specs = tok_spec(D_MODEL)
        out_shape = jax.ShapeDtypeStruct((t, D_MODEL), F32)
    return pl.pallas_call(
        functools.partial(_peer_mlp_kernel, final=final, n_prompt_tiles=n_prompt_tiles, n_sample=n_sample),
        grid=(t // tm, PEER_EXPERTS // EXPERT_TILE),
        in_specs=[tok_spec(D_MODEL),
                  pl.BlockSpec((EXPERT_TILE, D_MODEL), lambda i, e: (e, 0)),
                  pl.BlockSpec((EXPERT_TILE, D_MODEL), lambda i, e: (e, 0)),
                  tok_spec(PEER_HEADS * PEER_TOPK), tok_spec(PEER_HEADS * PEER_TOPK),
                  tok_spec(PEER_HEADS * PEER_TOPK),
                  tok_spec(D_MODEL),
                  _const_spec((1, D_MODEL))],
        out_specs=out_specs,
        out_shape=out_shape,
        scratch_shapes=[pltpu.VMEM((tm, D_MODEL), F32),
                        pltpu.VMEM((tm, EXPERT_TILE), F32),
                        pltpu.VMEM((tm, EXPERT_TILE), BF16),
                        pltpu.VMEM((tm * ROW_PITCH, PEER_KEYS), F32)],
        compiler_params=_params(("arbitrary", "arbitrary")),
        name="peer_mlp",
    )(xn, u_bf, v_bf, ia, ib, gate, x, fw)


def _token_tile(t):
    for tm in (512, 384, 256, 128):
        if t % tm == 0:
            return tm
    raise ValueError(f"token count {t} must be a multiple of {LANES}")


def _rope_tables(pos):
    half = RET_HEAD_DIM // 2
    inv = ROPE_BASE ** (-jnp.arange(half, dtype=F32) / half)
    ang = pos.astype(F32)[:, None] * inv[None, :]
    cos = jnp.cos(ang)
    sin = jnp.sin(ang)
    return jnp.concatenate([cos, cos], axis=1), jnp.concatenate([-sin, sin], axis=1)


def _block_diag(w):
    eye = jnp.eye(LRU_BLOCKS, dtype=w.dtype)
    return jnp.einsum('hij,hg->higj', w, eye).reshape(LRU_WIDTH, LRU_WIDTH)


def kernel(x_prompt, x_sample, state_ssd, state_ssd_conv, state_ret, state_lru, state_lru_conv, norm1_w, w_in, ssd_conv_w, ssd_conv_b, ssd_dt_bias, ssd_a_log, ssd_d, ssd_norm_w, lru_conv_w, lru_conv_b, lru_wa, lru_ba, lru_wi, lru_bi, lru_lambda, w_out, norm2_w, peer_wq, peer_keys, peer_u, peer_v, final_norm_w):
    batch, seq, _ = x_prompt.shape
    n_sample, dec_seq, _ = x_sample.shape
    depth = w_in.shape[0]
    assert dec_seq == 1 and seq % SCAN_CHUNK == 0 and n_sample % SAMPLE_TILE == 0
    n_prompt = batch * seq
    t = n_prompt + n_sample
    tm = _token_tile(t)

    x = jnp.concatenate([x_prompt.reshape(n_prompt, D_MODEL), x_sample.reshape(n_sample, D_MODEL)], axis=0)
    cos_p, sin_p = _rope_tables(jnp.arange(seq))
    cos_s, sin_s = _rope_tables(PAST_LEN + jnp.arange(dec_seq))
    row = lambda a: a.reshape(1, -1).astype(F32)
    per_channel = lambda a: jnp.repeat(a.astype(F32), SSD_HEAD_DIM).reshape(1, SSD_WIDTH)
    per_head = lambda a: jnp.pad(a.astype(F32), (0, LANES - SSD_HEADS)).reshape(1, LANES)
    off = SSD_WIDTH + SSD_CONV_DIM

    carried = (state_ssd, state_ssd_conv, state_ret, state_lru, state_lru_conv)
    new_p, st_s = [], None
    for i in range(depth):
        wi_ = w_in[i]
        dt_cols = wi_[:, off:off + SSD_HEADS]
        w_in_p = jnp.concatenate(
            [wi_[:, :off],
             jnp.pad(dt_cols, ((0, 0), (0, LANES - SSD_HEADS))),
             jnp.repeat(dt_cols, SSD_HEAD_DIM, axis=1),
             wi_[:, off + SSD_HEADS:]], axis=1).astype(BF16)
        wts = (ssd_conv_w[i], row(ssd_conv_b[i]), per_head(ssd_dt_bias[i]), per_head(ssd_a_log[i]),
               per_channel(ssd_dt_bias[i]), per_channel(ssd_a_log[i]), per_channel(ssd_d[i]),
               row(ssd_norm_w[i]), lru_conv_w[i], row(lru_conv_b[i]),
               _block_diag(lru_wa[i]).astype(BF16), row(lru_ba[i]),
               _block_diag(lru_wi[i]).astype(BF16), row(lru_bi[i]), row(lru_lambda[i]),
               w_out[i].astype(BF16))

        proj = _inproj(x, row(norm1_w[i]), w_in_p, tm)
        x_mixed, *st_p = _prompt_mixer(proj, x, cos_p, sin_p, wts, batch, seq)
        st_p[3] = st_p[3].reshape(batch, LRU_WIDTH)
        x, *st_s = _sample_mixer(proj, x, x_mixed, cos_s, sin_s, carried, st_s, wts, i, n_prompt, n_sample)
        new_p.append(st_p)

        keys_bf = peer_keys[i].reshape(PEER_HEADS * 2, PEER_KEYS, PEER_HALF).astype(BF16)
        xn, *route = _peer_route(x, row(norm2_w[i]), peer_wq[i].astype(BF16), keys_bf)
        x = _peer_mlp(xn, peer_u[i].astype(BF16), peer_v[i].astype(BF16), route, x,
                      row(final_norm_w), tm, i == depth - 1, n_prompt)

    y_prompt, y_sample = x
    stk = lambda states, j: jnp.stack([s[j] for s in states])
    return (y_prompt.reshape(batch, seq, D_MODEL), y_sample.reshape(n_sample, dec_seq, D_MODEL),
            stk(new_p, 0), stk(new_p, 1), stk(new_p, 2), stk(new_p, 3), stk(new_p, 4), *st_s)
```

```python
import functools

import numpy as np
import jax
import jax.numpy as jnp
from jax import lax
from jax.experimental import pallas as pl
from jax.experimental.pallas import tpu as pltpu

F32 = jnp.float32
BF16 = jnp.bfloat16

D_MODEL = 1024
PAST_LEN = 16384
D_MIX = 2 * D_MODEL
CONV_WIDTH = 4
NORM_EPS = 1e-6
SSD_WIDTH = D_MIX // 2
SSD_HEAD_DIM = 64
SSD_HEADS = SSD_WIDTH // SSD_HEAD_DIM
SSD_GROUPS = 2
SSD_HPG = SSD_HEADS // SSD_GROUPS
SSD_STATE = 128
SSD_CONV_DIM = SSD_WIDTH + 2 * SSD_GROUPS * SSD_STATE
RET_WIDTH = D_MIX // 4
RET_HEADS = 4
RET_HEAD_DIM = RET_WIDTH // RET_HEADS
ROPE_BASE = 10000.0
LRU_WIDTH = D_MIX // 4
LRU_BLOCKS = 8
LRU_BLOCK_DIM = LRU_WIDTH // LRU_BLOCKS
LRU_C = 8.0
PEER_KEYS = 128
PEER_EXPERTS = PEER_KEYS * PEER_KEYS
PEER_HEADS = 8
PEER_TOPK = 16
PEER_QUERY_DIM = 256
PEER_HALF = PEER_QUERY_DIM // 2

LANES = 128
SUBLANES = 8
VMEM_LIMIT = 56 * 1024 * 1024

COL_Z = 0
COL_XBC = COL_Z + SSD_WIDTH
COL_DT16 = COL_XBC + SSD_CONV_DIM
COL_DTX = COL_DT16 + LANES
COL_Q = COL_DTX + SSD_WIDTH
COL_K = COL_Q + RET_WIDTH
COL_V = COL_K + RET_WIDTH
COL_G = COL_V + RET_WIDTH
COL_GATE = COL_G + RET_WIDTH
COL_XR = COL_GATE + LRU_WIDTH
PROJ_WIDTH = COL_XR + LRU_WIDTH

SCAN_CHUNK = 128
SAMPLE_TILE = SUBLANES
SAMPLE_HEAD_BLOCK = 4
EXPERT_TILE = 1024
BUILD_UNROLL = 16
LOG_GAMMA = [float(np.log(1.0 - 2.0 ** (-5.0 - h))) for h in range(RET_HEADS)]

_NT = (((1,), (1,)), ((), ()))


def _rms(x):
    return x * lax.rsqrt(jnp.mean(x * x, axis=-1, keepdims=True) + NORM_EPS)


def _softplus(x):
    return jnp.maximum(x, 0.0) + jnp.log1p(jnp.exp(-jnp.abs(x)))


def _silu(x):
    return x * jax.nn.sigmoid(x)


def _gelu(x):
    return 0.5 * x * (1.0 + lax.erf(x * np.float32(np.sqrt(0.5))))


def _neg_expm1(x):
    return -jnp.tanh(0.5 * x) * (jnp.exp(x) + 1.0)


def _dot(a, b):
    return jnp.dot(a, b, preferred_element_type=F32)


def _dot_nt(a, b):
    return lax.dot_general(a, b, _NT, preferred_element_type=F32)


def _dot_exact(a, b):
    return jnp.dot(a, b, preferred_element_type=F32, precision=lax.Precision.HIGHEST)


def _params(semantics):
    return pltpu.CompilerParams(dimension_semantics=semantics, vmem_limit_bytes=VMEM_LIMIT)


def _const_spec(shape):
    zeros = (0,) * len(shape)
    return pl.BlockSpec(shape, lambda *_: zeros)


def _inproj_kernel(x_ref, n_ref, w_ref, o_ref):
    h = (_rms(x_ref[...]) * n_ref[...]).astype(BF16)
    step = 8 * LANES
    for j in range(0, PROJ_WIDTH, step):
        hi = min(j + step, PROJ_WIDTH)
        o_ref[:, j:hi] = _dot(h, w_ref[:, j:hi])


def _inproj(x, n1, w_in_p, tm):
    t = x.shape[0]
    return pl.pallas_call(
        _inproj_kernel,
        grid=(t // tm,),
        in_specs=[pl.BlockSpec((tm, D_MODEL), lambda i: (i, 0)),
                  _const_spec((1, D_MODEL)),
                  pl.BlockSpec((D_MODEL, PROJ_WIDTH), lambda i: (0, 0), pipeline_mode=pl.Buffered(1))],
        out_specs=pl.BlockSpec((tm, PROJ_WIDTH), lambda i: (i, 0)),
        out_shape=jax.ShapeDtypeStruct((t, PROJ_WIDTH), F32),
        compiler_params=_params(("arbitrary",)),
        name="inproj",
    )(x, n1, w_in_p)


def _prompt_mixer_kernel(proj_ref, x_ref, cos_ref, sin_ref, scw_ref, scb_ref, dtb16_ref, alog16_ref,
                         dtbx_ref, alogx_ref, dskx_ref, snw_ref, lcw_ref, lcb_ref, wa_ref, ba_ref,
                         wi_ref, bi_ref, lam_ref, wout_ref,
                         xo_ref, ssd_ref, sbuf_ref, ret_ref, lruh_ref, lbuf_ref,
                         xps_ref, xpl_ref, mix_ref):
    c = SCAN_CHUNK
    step = pl.program_id(1)
    pad = SUBLANES
    keep = CONV_WIDTH - 1

    @pl.when(step == 0)
    def _():
        ssd_ref[...] = jnp.zeros_like(ssd_ref)
        ret_ref[...] = jnp.zeros_like(ret_ref)
        lruh_ref[...] = jnp.zeros_like(lruh_ref)
        xps_ref[0:pad, :] = jnp.zeros((pad, SSD_CONV_DIM), F32)
        xpl_ref[0:pad, :] = jnp.zeros((pad, LRU_WIDTH), F32)

    def conv(xp_ref, x_new, w_ref, b_ref, buf_out_ref):
        xp_ref[pad:pad + c, :] = x_new
        y = b_ref[...] + xp_ref[pad - keep:pad - keep + c, :] * w_ref[0:1, :]
        for tap in range(1, CONV_WIDTH):
            y = y + xp_ref[pad - keep + tap:pad - keep + tap + c, :] * w_ref[tap:tap + 1, :]
        tail = xp_ref[pad + c - keep:pad + c, :]
        xp_ref[pad - keep:pad, :] = tail
        buf_out_ref[0] = tail
        return y

    row = lax.broadcasted_iota(jnp.int32, (c, c), 0)
    col = lax.broadcasted_iota(jnp.int32, (c, c), 1)
    causal = row >= col
    tri = causal.astype(F32)
    rowf = lax.broadcasted_iota(jnp.int32, (c, LANES), 0).astype(F32)

    xbc = _silu(conv(xps_ref, proj_ref[:, COL_XBC:COL_XBC + SSD_CONV_DIM], scw_ref, scb_ref, sbuf_ref))
    xs = xbc[:, :SSD_WIDTH]
    bm = xbc[:, SSD_WIDTH:SSD_WIDTH + SSD_GROUPS * SSD_STATE]
    cm = xbc[:, SSD_WIDTH + SSD_GROUPS * SSD_STATE:]
    dtx = _softplus(proj_ref[:, COL_DTX:COL_DTX + SSD_WIDTH] + dtbx_ref[...])
    cumx = _dot_exact(tri, -dtx * jnp.exp(alogx_ref[...]))
    dt16 = _softplus(proj_ref[:, COL_DT16:COL_DT16 + LANES] + dtb16_ref[...])
    cum16 = _dot_exact(tri, -dt16 * jnp.exp(alog16_ref[...]))
    cum16_t = cum16.T
    v = xs * dtx
    v_bf = v.astype(BF16)
    cum_last = cumx[c - 1:c, :]
    vte_bf = (v * jnp.exp(cum_last - cumx)).astype(BF16)
    ecum = jnp.exp(cumx)
    chunk_decay = jnp.exp(cum_last)
    for g in range(SSD_GROUPS):
        bmg = bm[:, g * SSD_STATE:(g + 1) * SSD_STATE]
        cmg_bf = cm[:, g * SSD_STATE:(g + 1) * SSD_STATE].astype(BF16)
        scores = _dot_nt(cmg_bf, bmg.astype(BF16))
        bmg_t_bf = bmg.T.astype(BF16)
        for r in range(SSD_HPG):
            h = g * SSD_HPG + r
            hs = slice(h * SSD_HEAD_DIM, (h + 1) * SSD_HEAD_DIM)
            seg = cum16[:, h:h + 1] - cum16_t[h:h + 1, :]
            decay = jnp.exp(jnp.where(causal, seg, -jnp.inf))
            y_intra = _dot((scores * decay).astype(BF16), v_bf[:, hs])
            s_in = ssd_ref[0, h]
            y_inter = _dot(cmg_bf, s_in.astype(BF16)) * ecum[:, hs]
            ssd_ref[0, h] = s_in * chunk_decay[:, hs] + _dot(bmg_t_bf, vte_bf[:, hs])
            mix_ref[:, hs] = y_intra + y_inter
    y = mix_ref[:, :SSD_WIDTH] + dskx_ref[...] * xs
    y = y * _silu(proj_ref[:, COL_Z:COL_Z + SSD_WIDTH])
    mix_ref[:, :SSD_WIDTH] = _rms(y) * snw_ref[...]

    cosf = cos_ref[...]
    sinf = sin_ref[...]
    rowcol = (row - col).astype(F32)
    for h in range(RET_HEADS):
        lg = LOG_GAMMA[h]
        hs = slice(h * RET_HEAD_DIM, (h + 1) * RET_HEAD_DIM)
        qh = proj_ref[:, COL_Q + h * RET_HEAD_DIM:COL_Q + (h + 1) * RET_HEAD_DIM]
        kh = proj_ref[:, COL_K + h * RET_HEAD_DIM:COL_K + (h + 1) * RET_HEAD_DIM]
        vh = proj_ref[:, COL_V + h * RET_HEAD_DIM:COL_V + (h + 1) * RET_HEAD_DIM]
        gh = proj_ref[:, COL_G + h * RET_HEAD_DIM:COL_G + (h + 1) * RET_HEAD_DIM]
        qh = qh * cosf + pltpu.roll(qh, RET_HEAD_DIM // 2, 1) * sinf
        kh = (kh * cosf + pltpu.roll(kh, RET_HEAD_DIM // 2, 1) * sinf) * (RET_HEAD_DIM ** -0.5)
        qh_bf = qh.astype(BF16)
        scores = _dot_nt(qh_bf, kh.astype(BF16))
        decay = jnp.exp(jnp.where(causal, rowcol * lg, -jnp.inf))
        y_intra = _dot((scores * decay).astype(BF16), vh.astype(BF16))
        s_in = ret_ref[0, h]
        y_inter = _dot(qh_bf, s_in.astype(BF16)) * jnp.exp((rowf + 1.0) * lg)
        v_to_end = (vh * jnp.exp((c - 1.0 - rowf) * lg)).astype(BF16)
        ret_ref[0, h] = s_in * float(np.exp(c * lg)) + _dot(kh.T.astype(BF16), v_to_end)
        mix_ref[:, SSD_WIDTH + h * RET_HEAD_DIM:SSD_WIDTH + (h + 1) * RET_HEAD_DIM] = (
            _rms(y_intra + y_inter) * _silu(gh))

    xc = conv(xpl_ref, proj_ref[:, COL_XR:COL_XR + LRU_WIDTH], lcw_ref, lcb_ref, lbuf_ref)
    xc_bf = xc.astype(BF16)
    r_gate = jax.nn.sigmoid(_dot(xc_bf, wa_ref[...]) + ba_ref[...])
    i_gate = jax.nn.sigmoid(_dot(xc_bf, wi_ref[...]) + bi_ref[...])
    log_a = -LRU_C * r_gate * _softplus(-lam_ref[...])
    a_run = jnp.exp(log_a)
    b_run = jnp.sqrt(_neg_expm1(2.0 * log_a)) * (i_gate * xc)
    rowl = lax.broadcasted_iota(jnp.int32, (c, LRU_WIDTH), 0)
    shift = 1
    while shift < c:
        valid = rowl >= shift
        a_prev = pltpu.roll(a_run, shift, 0)
        b_prev = pltpu.roll(b_run, shift, 0)
        b_run = jnp.where(valid, a_run * b_prev + b_run, b_run)
        a_run = jnp.where(valid, a_run * a_prev, a_run)
        shift *= 2
    h_all = b_run + a_run * lruh_ref[0]
    lruh_ref[0] = h_all[c - 1:c, :]
    mix_ref[:, SSD_WIDTH + RET_WIDTH:] = h_all * _gelu(proj_ref[:, COL_GATE:COL_GATE + LRU_WIDTH])

    xo_ref[...] = x_ref[...] + _dot(mix_ref[...].astype(BF16), wout_ref[...])


def _prompt_mixer(proj, x_all, cosf, sinf, wts, batch, seq):
    c = SCAN_CHUNK
    nc = seq // c
    t = x_all.shape[0]
    row_spec = lambda width: pl.BlockSpec((c, width), lambda b, s: (b * nc + s, 0))
    state_spec = lambda shape: pl.BlockSpec((1,) + shape, lambda b, s: (b,) + (0,) * len(shape))
    in_specs = [row_spec(PROJ_WIDTH), row_spec(D_MODEL),
                pl.BlockSpec((c, LANES), lambda b, s: (s, 0)),
                pl.BlockSpec((c, LANES), lambda b, s: (s, 0))]
    in_specs += [_const_spec(w.shape) for w in wts]
    out_shapes = (jax.ShapeDtypeStruct((t, D_MODEL), F32),
                  jax.ShapeDtypeStruct((batch, SSD_HEADS, SSD_STATE, SSD_HEAD_DIM), F32),
                  jax.ShapeDtypeStruct((batch, CONV_WIDTH - 1, SSD_CONV_DIM), F32),
                  jax.ShapeDtypeStruct((batch, RET_HEADS, RET_HEAD_DIM, RET_HEAD_DIM), F32),
                  jax.ShapeDtypeStruct((batch, 1, LRU_WIDTH), F32),
                  jax.ShapeDtypeStruct((batch, CONV_WIDTH - 1, LRU_WIDTH), F32))
    out_specs = (row_spec(D_MODEL),
                 state_spec((SSD_HEADS, SSD_STATE, SSD_HEAD_DIM)),
                 state_spec((CONV_WIDTH - 1, SSD_CONV_DIM)),
                 state_spec((RET_HEADS, RET_HEAD_DIM, RET_HEAD_DIM)),
                 state_spec((1, LRU_WIDTH)),
                 state_spec((CONV_WIDTH - 1, LRU_WIDTH)))
    return pl.pallas_call(
        _prompt_mixer_kernel,
        grid=(batch, nc),
        in_specs=in_specs,
        out_specs=out_specs,
        out_shape=out_shapes,
        scratch_shapes=[pltpu.VMEM((SUBLANES + c, SSD_CONV_DIM), F32),
                        pltpu.VMEM((SUBLANES + c, LRU_WIDTH), F32),
                        pltpu.VMEM((c, D_MIX), F32)],
        compiler_params=_params(("arbitrary", "arbitrary")),
        name="prompt_mixer",
    )(proj, x_all, cosf, sinf, *wts)


def _sample_mixer_kernel(proj_ref, x_ref, cos_ref, sin_ref, ssd_in_ref, sbuf_in_ref, ret_in_ref,
                         lruh_in_ref, lbuf_in_ref, scw_ref, scb_ref, dtb16_ref, alog16_ref,
                         dtbx_ref, alogx_ref, dskx_ref, snw_ref, lcw_ref, lcb_ref, wa_ref, ba_ref,
                         wi_ref, bi_ref, lam_ref, wout_ref, *rest, n_aliased):
    (xo_ref, ssd_ref, sbuf_ref, ret_ref, lruh_ref, lbuf_ref,
     vh_ref, ah_ref, bm_ref, cm_ref, yh_ref, xs_ref, mix_ref) = rest[n_aliased:]
    del dtb16_ref, alog16_ref
    bt = SAMPLE_TILE
    hb = pl.program_id(1)
    nhb = pl.num_programs(1)
    keep = CONV_WIDTH - 1

    def conv(x_new, buf_in_ref, w_ref, b_ref, buf_out_ref):
        y = b_ref[...] + x_new * w_ref[keep:keep + 1, :]
        for tap in range(keep):
            y = y + buf_in_ref[:, tap, :] * w_ref[tap:tap + 1, :]
        for tap in range(keep - 1):
            buf_out_ref[:, tap, :] = buf_in_ref[:, tap + 1, :]
        buf_out_ref[:, keep - 1, :] = x_new
        return y

    @pl.when(hb == 0)
    def _():
        xbc = _silu(conv(proj_ref[:, COL_XBC:COL_XBC + SSD_CONV_DIM], sbuf_in_ref, scw_ref, scb_ref, sbuf_ref))
        xs = xbc[:, :SSD_WIDTH]
        xs_ref[...] = xs
        dtx = _softplus(proj_ref[:, COL_DTX:COL_DTX + SSD_WIDTH] + dtbx_ref[...])
        a = jnp.exp(-dtx * jnp.exp(alogx_ref[...]))
        v = xs * dtx
        for h in range(SSD_HEADS):
            hs = slice(h * SSD_HEAD_DIM, (h + 1) * SSD_HEAD_DIM)
            vh_ref[h] = v[:, hs].T
            ah_ref[h] = a[:, hs].T
        for g in range(SSD_GROUPS):
            bm_ref[g] = xbc[:, SSD_WIDTH + g * SSD_STATE:SSD_WIDTH + (g + 1) * SSD_STATE]
            off = SSD_WIDTH + SSD_GROUPS * SSD_STATE
            cm_ref[g] = xbc[:, off + g * SSD_STATE:off + (g + 1) * SSD_STATE]

    g = (hb * SAMPLE_HEAD_BLOCK) // SSD_HPG
    bm = bm_ref[g]
    cm = cm_ref[g]
    lane = lax.broadcasted_iota(jnp.int32, (SSD_HEAD_DIM, SSD_STATE), 1)
    for r in range(SAMPLE_HEAD_BLOCK):
        h = hb * SAMPLE_HEAD_BLOCK + r
        vh_t = vh_ref[h]
        ah_t = ah_ref[h]
        y_t = jnp.zeros((SSD_HEAD_DIM, SSD_STATE), F32)
        for b in range(bt):
            s_new = ssd_in_ref[b, r] * ah_t[:, b:b + 1] + vh_t[:, b:b + 1] * bm[b:b + 1, :]
            ssd_ref[b, r] = s_new
            y_col = jnp.sum(s_new * cm[b:b + 1, :], axis=1, keepdims=True)
            y_t = jnp.where(lane == b, y_col, y_t)
        yh_ref[h] = y_t.T[:bt, :]

    @pl.when(hb == nhb - 1)
    def _():
        for h in range(SSD_HEADS):
            mix_ref[:, h * SSD_HEAD_DIM:(h + 1) * SSD_HEAD_DIM] = yh_ref[h]
        y = mix_ref[:, :SSD_WIDTH] + dskx_ref[...] * xs_ref[...]
        y = y * _silu(proj_ref[:, COL_Z:COL_Z + SSD_WIDTH])
        mix_ref[:, :SSD_WIDTH] = _rms(y) * snw_ref[...]

        cosf = cos_ref[...]
        sinf = sin_ref[...]
        for h in range(RET_HEADS):
            lg = LOG_GAMMA[h]
            qh = proj_ref[:, COL_Q + h * RET_HEAD_DIM:COL_Q + (h + 1) * RET_HEAD_DIM]
            kh = proj_ref[:, COL_K + h * RET_HEAD_DIM:COL_K + (h + 1) * RET_HEAD_DIM]
            vh = proj_ref[:, COL_V + h * RET_HEAD_DIM:COL_V + (h + 1) * RET_HEAD_DIM]
            gh = proj_ref[:, COL_G + h * RET_HEAD_DIM:COL_G + (h + 1) * RET_HEAD_DIM]
            qh = qh * cosf + pltpu.roll(qh, RET_HEAD_DIM // 2, 1) * sinf
            kh = (kh * cosf + pltpu.roll(kh, RET_HEAD_DIM // 2, 1) * sinf) * (RET_HEAD_DIM ** -0.5)
            qt = qh.T
            kt = kh.T
            hs = slice(SSD_WIDTH + h * RET_HEAD_DIM, SSD_WIDTH + (h + 1) * RET_HEAD_DIM)
            for b in range(bt):
                s_new = ret_in_ref[b, h] * float(np.exp(lg)) + kt[:, b:b + 1] * vh[b:b + 1, :]
                ret_ref[b, h] = s_new
                mix_ref[b:b + 1, hs] = jnp.sum(qt[:, b:b + 1] * s_new, axis=0, keepdims=True)
            mix_ref[:, hs] = _rms(mix_ref[:, hs]) * _silu(gh)

        xc = conv(proj_ref[:, COL_XR:COL_XR + LRU_WIDTH], lbuf_in_ref, lcw_ref, lcb_ref, lbuf_ref)
        xc_bf = xc.astype(BF16)
        r_gate = jax.nn.sigmoid(_dot(xc_bf, wa_ref[...]) + ba_ref[...])
        i_gate = jax.nn.sigmoid(_dot(xc_bf, wi_ref[...]) + bi_ref[...])
        log_a = -LRU_C * r_gate * _softplus(-lam_ref[...])
        h_new = (jnp.exp(log_a) * lruh_in_ref[...]
                 + jnp.sqrt(_neg_expm1(2.0 * log_a)) * (i_gate * xc))
        lruh_ref[...] = h_new
        mix_ref[:, SSD_WIDTH + RET_WIDTH:] = h_new * _gelu(proj_ref[:, COL_GATE:COL_GATE + LRU_WIDTH])

        xo_ref[...] = x_ref[...] + _dot(mix_ref[...].astype(BF16), wout_ref[...])


def _sample_mixer(proj, x_all, x_mixed, cosf, sinf, states, new_states, wts, layer, n_prompt, n_sample):
    bt = SAMPLE_TILE
    hbk = SAMPLE_HEAD_BLOCK
    base = n_prompt // bt
    t = x_all.shape[0]
    row_spec = lambda width: pl.BlockSpec((bt, width), lambda i, j: (base + i, 0))
    ssd_spec = pl.BlockSpec((None, bt, hbk, SSD_HEAD_DIM, SSD_STATE), lambda i, j: (layer, i, j, 0, 0))
    sbuf_spec = pl.BlockSpec((None, bt, CONV_WIDTH - 1, SSD_CONV_DIM), lambda i, j: (layer, i, 0, 0))
    ret_spec = pl.BlockSpec((None, bt, RET_HEADS, RET_HEAD_DIM, RET_HEAD_DIM), lambda i, j: (layer, i, 0, 0, 0))
    lruh_spec = pl.BlockSpec((None, bt, LRU_WIDTH), lambda i, j: (layer, i, 0))
    lbuf_spec = pl.BlockSpec((None, bt, CONV_WIDTH - 1, LRU_WIDTH), lambda i, j: (layer, i, 0, 0))
    state_specs = [ssd_spec, sbuf_spec, ret_spec, lruh_spec, lbuf_spec]
    aliased = (x_mixed,) + (tuple(new_states) if new_states is not None else ())
    in_specs = [row_spec(PROJ_WIDTH), row_spec(D_MODEL),
                _const_spec((1, LANES)), _const_spec((1, LANES))] + state_specs
    in_specs += [_const_spec(w.shape) for w in wts]
    n_in = len(in_specs)
    in_specs += [pl.BlockSpec(memory_space=pl.ANY)] * len(aliased)
    out_shapes = (jax.ShapeDtypeStruct((t, D_MODEL), F32),) + tuple(
        jax.ShapeDtypeStruct(s.shape, F32) for s in states)
    out_specs = (row_spec(D_MODEL),) + tuple(state_specs)
    return pl.pallas_call(
        functools.partial(_sample_mixer_kernel, n_aliased=len(aliased)),
        grid=(n_sample // bt, SSD_HEADS // hbk),
        in_specs=in_specs,
        out_specs=out_specs,
        out_shape=out_shapes,
        scratch_shapes=[pltpu.VMEM((SSD_HEADS, SSD_HEAD_DIM, bt), F32),
                        pltpu.VMEM((SSD_HEADS, SSD_HEAD_DIM, bt), F32),
                        pltpu.VMEM((SSD_GROUPS, bt, SSD_STATE), F32),
                        pltpu.VMEM((SSD_GROUPS, bt, SSD_STATE), F32),
                        pltpu.VMEM((SSD_HEADS, bt, SSD_HEAD_DIM), F32),
                        pltpu.VMEM((bt, SSD_WIDTH), F32),
                        pltpu.VMEM((bt, D_MIX), F32)],
        input_output_aliases={n_in + k: k for k in range(len(aliased))},
        compiler_params=_params(("arbitrary", "arbitrary")),
        name="sample_mixer",
    )(proj, x_all, cosf, sinf, *states, *wts, *aliased)


def _batcher_pairs(n):
    pairs = []

    def merge(lo, hi, r):
        step = r * 2
        if step < hi - lo:
            merge(lo, hi, step)
            merge(lo + r, hi, step)
            pairs.extend((i, i + r) for i in range(lo + r, hi - r, step))
        else:
            pairs.append((lo, lo + r))

    def sort(lo, hi):
        if hi - lo >= 1:
            mid = lo + (hi - lo) // 2
            sort(lo, mid)
            sort(mid + 1, hi)
            merge(lo, hi, 1)

    sort(0, n - 1)
    return pairs


_SORT16_PAIRS = _batcher_pairs(PEER_TOPK)


def _first(a, b):
    return (a[0] > b[0]) | ((a[0] == b[0]) & (a[1] < b[1]))


def _exchange(a, b):
    a_first = _first(a, b)
    hi = (jnp.maximum(a[0], b[0]),) + tuple(jnp.where(a_first, pa, pb) for pa, pb in zip(a[1:], b[1:]))
    lo = (jnp.minimum(a[0], b[0]),) + tuple(jnp.where(a_first, pb, pa) for pa, pb in zip(a[1:], b[1:]))
    return hi, lo


def _better(a, b):
    a_first = _first(a, b)
    return (jnp.maximum(a[0], b[0]),) + tuple(jnp.where(a_first, pa, pb) for pa, pb in zip(a[1:], b[1:]))


def _sort16(items):
    items = list(items)
    for i, j in _SORT16_PAIRS:
        items[i], items[j] = _exchange(items[i], items[j])
    return items


def _merge_top16(a, b):
    k = PEER_TOPK
    c = [_better(a[i], b[k - 1 - i]) for i in range(k)]
    d = k // 2
    while d >= 1:
        for i in range(k):
            if i & d == 0:
                c[i], c[i + d] = _exchange(c[i], c[i + d])
        d //= 2
    return c


def _top16_of_lists(vals, tags):
    k = PEER_TOPK
    best = None
    for g in range(0, len(vals), k):
        group = _sort16([(vals[i]() if callable(vals[i]) else vals[i], tags[i]) for i in range(g, g + k)])
        best = group if best is None else _merge_top16(best, group)
    return [b[0] for b in best], [b[1] for b in best]


ROUTE_TILE = SUBLANES * LANES
ROW_PITCH = PEER_KEYS + SUBLANES


def _peer_route_kernel(x_ref, n_ref, wq_ref, keys_ref,
                       xn_ref, ia_ref, ib_ref, gate_ref,
                       q_ref, z1_ref, z2_ref, za_ref, zb_ref, zg_ref):
    k = PEER_TOPK
    tile = (SUBLANES, LANES)
    xn = (_rms(x_ref[...]) * n_ref[...]).astype(BF16)
    xn_ref[...] = xn
    q = _dot(xn, wq_ref[...]).astype(BF16)
    for hc in range(2 * PEER_HEADS):
        q_ref[hc] = q[:, hc * PEER_HALF:(hc + 1) * PEER_HALF]

    def head(h, carry):
        for c, z_ref in ((0, z1_ref), (1, z2_ref)):
            keys = keys_ref[2 * h + c]
            for j in range(SUBLANES):
                z_ref[j * ROW_PITCH:j * ROW_PITCH + PEER_KEYS, :] = _dot_nt(
                    keys, q_ref[2 * h + c, j * LANES:(j + 1) * LANES, :])

        def top_keys(z_ref):
            load = lambda key: (lambda: z_ref[pl.ds(key, SUBLANES, stride=ROW_PITCH), :])
            return _top16_of_lists([load(key) for key in range(PEER_KEYS)],
                                   [jnp.full(tile, float(key), F32) for key in range(PEER_KEYS)])

        t1, i1 = top_keys(z1_ref)
        t2, i2 = top_keys(z2_ref)
        i1s = [v * float(PEER_KEYS) for v in i1]

        def pair(a, b):
            return (t1[a] + t2[b], jnp.full(tile, float(a * k + b), F32), i1s[a] + i2[b])

        pad = (jnp.full(tile, -jnp.inf, F32), jnp.full(tile, float(k * k), F32), jnp.zeros(tile, F32))
        row = lambda a: [pair(a, b) for b in range(k // (a + 1))]
        first = _merge_top16(row(0), row(1) + [pad] * (k - len(row(1))))
        second = _sort16(row(2) + row(3) + row(4) + row(5) + row(6))
        third_items = row(7) + [pair(a, 0) for a in range(8, k)]
        third = _sort16(third_items + [pad] * (k - len(third_items)))
        best = _merge_top16(_merge_top16(first, second), third)

        e = [jnp.exp(item[0] - best[0][0]) for item in best]
        z = e[0]
        for v in e[1:]:
            z = z + v
        inv = 1.0 / z
        for r in range(k):
            sel = h * k + r
            expert = best[r][2]
            a_idx = jnp.floor(expert * (1.0 / PEER_KEYS))
            za_ref[pl.ds(sel, SUBLANES, stride=ROW_PITCH), :] = a_idx
            zb_ref[pl.ds(sel, SUBLANES, stride=ROW_PITCH), :] = expert - a_idx * float(PEER_KEYS)
            zg_ref[pl.ds(sel, SUBLANES, stride=ROW_PITCH), :] = e[r] * inv
        return carry

    lax.fori_loop(0, PEER_HEADS, head, 0)
    for j in range(SUBLANES):
        rows = slice(j * LANES, (j + 1) * LANES)
        src = slice(j * ROW_PITCH, j * ROW_PITCH + PEER_HEADS * k)
        ia_ref[rows, :] = za_ref[src, :].T
        ib_ref[rows, :] = zb_ref[src, :].T
        gate_ref[rows, :] = zg_ref[src, :].T


def _peer_route(x, n2, wq, keys):
    t = x.shape[0]
    tm = ROUTE_TILE
    sel_spec = pl.BlockSpec((tm, PEER_HEADS * PEER_TOPK), lambda i: (i, 0))
    sel_shape = jax.ShapeDtypeStruct((t, PEER_HEADS * PEER_TOPK), F32)
    z_scratch = pltpu.VMEM((SUBLANES * ROW_PITCH, LANES), F32)
    return pl.pallas_call(
        _peer_route_kernel,
        grid=(pl.cdiv(t, tm),),
        in_specs=[pl.BlockSpec((tm, D_MODEL), lambda i: (i, 0)),
                  _const_spec((1, D_MODEL)),
                  _const_spec(wq.shape),
                  _const_spec(keys.shape)],
        out_specs=(pl.BlockSpec((tm, D_MODEL), lambda i: (i, 0)), sel_spec, sel_spec, sel_spec),
        out_shape=(jax.ShapeDtypeStruct((t, D_MODEL), BF16), sel_shape, sel_shape, sel_shape),
        scratch_shapes=[pltpu.VMEM((2 * PEER_HEADS, tm, PEER_HALF), BF16),
                        z_scratch, z_scratch, z_scratch, z_scratch, z_scratch],
        compiler_params=_params(("arbitrary",)),
        name="peer_route",
    )(x, n2, wq, keys)


def _peer_mlp_kernel(xn_ref, u_ref, v_ref, ia_ref, ib_ref, gate_ref, x_ref, fw_ref, *rest,
                     final, n_prompt_tiles, n_sample):
    if final:
        o_ref, os_ref, acc_ref, act_ref, w_ref, g_ref = rest
    else:
        o_ref, acc_ref, act_ref, w_ref, g_ref = rest
    e = pl.program_id(1)
    tm = xn_ref.shape[0]
    rows_per_tile = EXPERT_TILE // PEER_KEYS

    @pl.when(e == 0)
    def _():
        acc_ref[...] = jnp.zeros_like(acc_ref)
        key = lax.broadcasted_iota(jnp.int32, (PEER_KEYS, PEER_HEADS * PEER_TOPK), 0).astype(F32)

        def build(t, carry):
            gate = gate_ref[pl.ds(t, 1), :]
            sel_gate = jnp.where(key == ia_ref[pl.ds(t, 1), :], gate, 0.0)
            hi = sel_gate.astype(BF16)
            lo = (sel_gate - hi.astype(F32)).astype(BF16)
            onehot = jnp.where(key == ib_ref[pl.ds(t, 1), :], 1.0, 0.0).astype(BF16)
            g_ref[pl.ds(pl.multiple_of(t * ROW_PITCH, SUBLANES), PEER_KEYS), :] = _dot_nt(
                jnp.concatenate([hi, lo], axis=1), jnp.concatenate([onehot, onehot], axis=1))
            return carry

        lax.fori_loop(0, tm, build, 0, unroll=BUILD_UNROLL)

    act_ref[...] = _dot_nt(xn_ref[...], u_ref[...])
    for j in range(rows_per_tile):
        i1 = e * rows_per_tile + j
        cols = slice(j * PEER_KEYS, (j + 1) * PEER_KEYS)
        gate = g_ref[pl.ds(i1, tm, stride=ROW_PITCH), :]
        w_ref[:, cols] = (gate * _gelu(act_ref[:, cols])).astype(BF16)
    acc_ref[...] += _dot(w_ref[...], v_ref[...])

    @pl.when(e == pl.num_programs(1) - 1)
    def _():
        y = x_ref[...] + acc_ref[...]
        if not final:
            o_ref[...] = y
            return
        y = _rms(y) * fw_ref[...]
        tile = pl.program_id(0)

        @pl.when(tile < n_prompt_tiles)
        def _():
            o_ref[...] = y

        @pl.when(tile == pl.num_programs(0) - 1)
        def _():
            os_ref[...] = y[tm - n_sample:, :]


def _peer_mlp(xn, u_bf, v_bf, route, x, fw, tm, layer, final, n_prompt):
    t = x.shape[0]
    n_sample = t - n_prompt
    assert n_sample <= tm and n_sample % SUBLANES == 0
    n_prompt_tiles = pl.cdiv(n_prompt, tm)
    ia, ib, gate = route
    tok_spec = lambda width: pl.BlockSpec((tm, width), lambda i, e: (i, 0))
    if final:
        out_specs = (pl.BlockSpec((tm, D_MODEL), lambda i, e: (jnp.minimum(i, n_prompt_tiles - 1), 0)),
                     pl.BlockSpec((n_sample, D_MODEL), lambda i, e: (0, 0)))
        out_shape = (jax.ShapeDtypeStruct((n_prompt, D_MODEL), F32),
                     jax.ShapeDtypeStruct((n_sample, D_MODEL), F32))
    else:
        out_specs = tok_spec(D_MODEL)
        out_shape = jax.ShapeDtypeStruct((t, D_MODEL), F32)
    return pl.pallas_call(
        functools.partial(_peer_mlp_kernel, final=final, n_prompt_tiles=n_prompt_tiles, n_sample=n_sample),
        grid=(t // tm, PEER_EXPERTS // EXPERT_TILE),
        in_specs=[tok_spec(D_MODEL),
                  pl.BlockSpec((None, EXPERT_TILE, D_MODEL), lambda i, e: (layer, e, 0)),
                  pl.BlockSpec((None, EXPERT_TILE, D_MODEL), lambda i, e: (layer, e, 0)),
                  tok_spec(PEER_HEADS * PEER_TOPK), tok_spec(PEER_HEADS * PEER_TOPK),
                  tok_spec(PEER_HEADS * PEER_TOPK),
                  tok_spec(D_MODEL),
                  _const_spec((1, D_MODEL))],
        out_specs=out_specs,
        out_shape=out_shape,
        scratch_shapes=[pltpu.VMEM((tm, D_MODEL), F32),
                        pltpu.VMEM((tm, EXPERT_TILE), F32),
                        pltpu.VMEM((tm, EXPERT_TILE), BF16),
                        pltpu.VMEM((tm * ROW_PITCH, PEER_KEYS), F32)],
        compiler_params=_params(("arbitrary", "arbitrary")),
        name="peer_mlp",
    )(xn, u_bf, v_bf, ia, ib, gate, x, fw)


def _token_tile(t):
    for tm in (512, 384, 256, 128):
        if t % tm == 0:
            return tm
    raise ValueError(f"token count {t} must be a multiple of {LANES}")


def _rope_tables(pos):
    half = RET_HEAD_DIM // 2
    inv = ROPE_BASE ** (-jnp.arange(half, dtype=F32) / half)
    ang = pos.astype(F32)[:, None] * inv[None, :]
    cos = jnp.cos(ang)
    sin = jnp.sin(ang)
    return jnp.concatenate([cos, cos], axis=1), jnp.concatenate([-sin, sin], axis=1)


def _block_diag(w):
    eye = jnp.eye(LRU_BLOCKS, dtype=w.dtype)
    return jnp.einsum('hij,hg->higj', w, eye).reshape(LRU_WIDTH, LRU_WIDTH)


def kernel(x_prompt, x_sample, state_ssd, state_ssd_conv, state_ret, state_lru, state_lru_conv, norm1_w, w_in, ssd_conv_w, ssd_conv_b, ssd_dt_bias, ssd_a_log, ssd_d, ssd_norm_w, lru_conv_w, lru_conv_b, lru_wa, lru_ba, lru_wi, lru_bi, lru_lambda, w_out, norm2_w, peer_wq, peer_keys, peer_u, peer_v, final_norm_w):
    batch, seq, _ = x_prompt.shape
    n_sample, dec_seq, _ = x_sample.shape
    depth = w_in.shape[0]
    assert dec_seq == 1 and seq % SCAN_CHUNK == 0 and n_sample % SAMPLE_TILE == 0
    n_prompt = batch * seq
    t = n_prompt + n_sample
    tm = _token_tile(t)

    x = jnp.concatenate([x_prompt.reshape(n_prompt, D_MODEL), x_sample.reshape(n_sample, D_MODEL)], axis=0)
    cos_p, sin_p = _rope_tables(jnp.arange(seq))
    cos_s, sin_s = _rope_tables(PAST_LEN + jnp.arange(dec_seq))
    row = lambda a: a.reshape(1, -1).astype(F32)
    per_channel = lambda a: jnp.repeat(a.astype(F32), SSD_HEAD_DIM).reshape(1, SSD_WIDTH)
    per_head = lambda a: jnp.pad(a.astype(F32), (0, LANES - SSD_HEADS)).reshape(1, LANES)
    off = SSD_WIDTH + SSD_CONV_DIM

    carried = (jnp.swapaxes(state_ssd, -1, -2), state_ssd_conv, state_ret, state_lru, state_lru_conv)
    u_bf = peer_u.astype(BF16)
    v_bf = peer_v.astype(BF16)
    new_p, st_s = [], None
    for i in range(depth):
        wi_ = w_in[i]
        dt_cols = wi_[:, off:off + SSD_HEADS]
        w_in_p = jnp.concatenate(
            [wi_[:, :off],
             jnp.pad(dt_cols, ((0, 0), (0, LANES - SSD_HEADS))),
             jnp.repeat(dt_cols, SSD_HEAD_DIM, axis=1),
             wi_[:, off + SSD_HEADS:]], axis=1).astype(BF16)
        wts = (ssd_conv_w[i], row(ssd_conv_b[i]), per_head(ssd_dt_bias[i]), per_head(ssd_a_log[i]),
               per_channel(ssd_dt_bias[i]), per_channel(ssd_a_log[i]), per_channel(ssd_d[i]),
               row(ssd_norm_w[i]), lru_conv_w[i], row(lru_conv_b[i]),
               _block_diag(lru_wa[i]).astype(BF16), row(lru_ba[i]),
               _block_diag(lru_wi[i]).astype(BF16), row(lru_bi[i]), row(lru_lambda[i]),
               w_out[i].astype(BF16))

        proj = _inproj(x, row(norm1_w[i]), w_in_p, tm)
        x_mixed, *st_p = _prompt_mixer(proj, x, cos_p, sin_p, wts, batch, seq)
        st_p[3] = st_p[3].reshape(batch, LRU_WIDTH)
        x, *st_s = _sample_mixer(proj, x, x_mixed, cos_s, sin_s, carried, st_s, wts, i, n_prompt, n_sample)
        new_p.append(st_p)

        keys_bf = peer_keys[i].reshape(PEER_HEADS * 2, PEER_KEYS, PEER_HALF).astype(BF16)
        xn, *route = _peer_route(x, row(norm2_w[i]), peer_wq[i].astype(BF16), keys_bf)
        x = _peer_mlp(xn, u_bf, v_bf, route, x, row(final_norm_w), tm, i, i == depth - 1, n_prompt)

    y_prompt, y_sample = x
    stk = lambda states, j: jnp.stack([s[j] for s in states])
    return (y_prompt.reshape(batch, seq, D_MODEL), y_sample.reshape(n_sample, dec_seq, D_MODEL),
            stk(new_p, 0), stk(new_p, 1), stk(new_p, 2), stk(new_p, 3), stk(new_p, 4),
            jnp.swapaxes(st_s[0], -1, -2), *st_s[1:])
```

```python
import functools

import numpy as np
import jax
import jax.numpy as jnp
from jax import lax
from jax.experimental import pallas as pl
from jax.experimental.pallas import tpu as pltpu

F32 = jnp.float32
BF16 = jnp.bfloat16

D_MODEL = 1024
PAST_LEN = 16384
D_MIX = 2 * D_MODEL
CONV_WIDTH = 4
NORM_EPS = 1e-6
SSD_WIDTH = D_MIX // 2
SSD_HEAD_DIM = 64
SSD_HEADS = SSD_WIDTH // SSD_HEAD_DIM
SSD_GROUPS = 2
SSD_HPG = SSD_HEADS // SSD_GROUPS
SSD_STATE = 128
SSD_CONV_DIM = SSD_WIDTH + 2 * SSD_GROUPS * SSD_STATE
RET_WIDTH = D_MIX // 4
RET_HEADS = 4
RET_HEAD_DIM = RET_WIDTH // RET_HEADS
ROPE_BASE = 10000.0
LRU_WIDTH = D_MIX // 4
LRU_BLOCKS = 8
LRU_BLOCK_DIM = LRU_WIDTH // LRU_BLOCKS
LRU_C = 8.0
PEER_KEYS = 128
PEER_EXPERTS = PEER_KEYS * PEER_KEYS
PEER_HEADS = 8
PEER_TOPK = 16
PEER_QUERY_DIM = 256
PEER_HALF = PEER_QUERY_DIM // 2

LANES = 128
SUBLANES = 8
VMEM_LIMIT = 56 * 1024 * 1024

COL_Z = 0
COL_XBC = COL_Z + SSD_WIDTH
COL_DT16 = COL_XBC + SSD_CONV_DIM
COL_DTX = COL_DT16 + LANES
COL_Q = COL_DTX + SSD_WIDTH
COL_K = COL_Q + RET_WIDTH
COL_V = COL_K + RET_WIDTH
COL_G = COL_V + RET_WIDTH
COL_GATE = COL_G + RET_WIDTH
COL_XR = COL_GATE + LRU_WIDTH
PROJ_WIDTH = COL_XR + LRU_WIDTH

SCAN_CHUNK = 128
SAMPLE_TILE = SUBLANES
SAMPLE_HEAD_BLOCK = 4
EXPERT_TILE = 1024
BUILD_UNROLL = 16
LOG_GAMMA = [float(np.log(1.0 - 2.0 ** (-5.0 - h))) for h in range(RET_HEADS)]

_NT = (((1,), (1,)), ((), ()))


def _rms(x):
    return x * lax.rsqrt(jnp.mean(x * x, axis=-1, keepdims=True) + NORM_EPS)


def _softplus(x):
    return jnp.maximum(x, 0.0) + jnp.log1p(jnp.exp(-jnp.abs(x)))


def _silu(x):
    return x * jax.nn.sigmoid(x)


def _gelu(x):
    return 0.5 * x * (1.0 + lax.erf(x * np.float32(np.sqrt(0.5))))


def _neg_expm1(x):
    return -jnp.tanh(0.5 * x) * (jnp.exp(x) + 1.0)


def _dot(a, b):
    return jnp.dot(a, b, preferred_element_type=F32)


def _dot_nt(a, b):
    return lax.dot_general(a, b, _NT, preferred_element_type=F32)


def _dot_exact(a, b):
    return jnp.dot(a, b, preferred_element_type=F32, precision=lax.Precision.HIGHEST)


def _params(semantics):
    return pltpu.CompilerParams(dimension_semantics=semantics, vmem_limit_bytes=VMEM_LIMIT)


def _const_spec(shape):
    zeros = (0,) * len(shape)
    return pl.BlockSpec(shape, lambda *_: zeros)


def _inproj_kernel(x_ref, n_ref, w_ref, o_ref):
    h = (_rms(x_ref[...]) * n_ref[...]).astype(BF16)
    step = 8 * LANES
    for j in range(0, PROJ_WIDTH, step):
        hi = min(j + step, PROJ_WIDTH)
        o_ref[:, j:hi] = _dot(h, w_ref[:, j:hi])


def _inproj(x, n1, w_in_p, tm):
    t = x.shape[0]
    return pl.pallas_call(
        _inproj_kernel,
        grid=(t // tm,),
        in_specs=[pl.BlockSpec((tm, D_MODEL), lambda i: (i, 0)),
                  _const_spec((1, D_MODEL)),
                  pl.BlockSpec((D_MODEL, PROJ_WIDTH), lambda i: (0, 0), pipeline_mode=pl.Buffered(1))],
        out_specs=pl.BlockSpec((tm, PROJ_WIDTH), lambda i: (i, 0)),
        out_shape=jax.ShapeDtypeStruct((t, PROJ_WIDTH), F32),
        compiler_params=_params(("arbitrary",)),
        name="inproj",
    )(x, n1, w_in_p)


def _prompt_mixer_kernel(proj_ref, x_ref, cos_ref, sin_ref, scw_ref, scb_ref, dtb16_ref, alog16_ref,
                         dtbx_ref, alogx_ref, dskx_ref, snw_ref, lcw_ref, lcb_ref, wa_ref, ba_ref,
                         wi_ref, bi_ref, lam_ref, wout_ref,
                         xo_ref, ssd_ref, sbuf_ref, ret_ref, lruh_ref, lbuf_ref,
                         xps_ref, xpl_ref, mix_ref):
    c = SCAN_CHUNK
    step = pl.program_id(1)
    pad = SUBLANES
    keep = CONV_WIDTH - 1

    @pl.when(step == 0)
    def _():
        ssd_ref[...] = jnp.zeros_like(ssd_ref)
        ret_ref[...] = jnp.zeros_like(ret_ref)
        lruh_ref[...] = jnp.zeros_like(lruh_ref)
        xps_ref[0:pad, :] = jnp.zeros((pad, SSD_CONV_DIM), F32)
        xpl_ref[0:pad, :] = jnp.zeros((pad, LRU_WIDTH), F32)

    def conv(xp_ref, x_new, w_ref, b_ref, buf_out_ref):
        xp_ref[pad:pad + c, :] = x_new
        y = b_ref[...] + xp_ref[pad - keep:pad - keep + c, :] * w_ref[0:1, :]
        for tap in range(1, CONV_WIDTH):
            y = y + xp_ref[pad - keep + tap:pad - keep + tap + c, :] * w_ref[tap:tap + 1, :]
        tail = xp_ref[pad + c - keep:pad + c, :]
        xp_ref[pad - keep:pad, :] = tail
        buf_out_ref[0] = tail
        return y

    row = lax.broadcasted_iota(jnp.int32, (c, c), 0)
    col = lax.broadcasted_iota(jnp.int32, (c, c), 1)
    causal = row >= col
    tri = causal.astype(F32)
    rowf = lax.broadcasted_iota(jnp.int32, (c, LANES), 0).astype(F32)

    xbc = _silu(conv(xps_ref, proj_ref[:, COL_XBC:COL_XBC + SSD_CONV_DIM], scw_ref, scb_ref, sbuf_ref))
    xs = xbc[:, :SSD_WIDTH]
    bm = xbc[:, SSD_WIDTH:SSD_WIDTH + SSD_GROUPS * SSD_STATE]
    cm = xbc[:, SSD_WIDTH + SSD_GROUPS * SSD_STATE:]
    dtx = _softplus(proj_ref[:, COL_DTX:COL_DTX + SSD_WIDTH] + dtbx_ref[...])
    cumx = _dot_exact(tri, -dtx * jnp.exp(alogx_ref[...]))
    dt16 = _softplus(proj_ref[:, COL_DT16:COL_DT16 + LANES] + dtb16_ref[...])
    cum16 = _dot_exact(tri, -dt16 * jnp.exp(alog16_ref[...]))
    cum16_t = cum16.T
    v = xs * dtx
    v_bf = v.astype(BF16)
    cum_last = cumx[c - 1:c, :]
    vte_bf = (v * jnp.exp(cum_last - cumx)).astype(BF16)
    ecum = jnp.exp(cumx)
    chunk_decay = jnp.exp(cum_last)
    for g in range(SSD_GROUPS):
        bmg = bm[:, g * SSD_STATE:(g + 1) * SSD_STATE]
        cmg_bf = cm[:, g * SSD_STATE:(g + 1) * SSD_STATE].astype(BF16)
        scores = _dot_nt(cmg_bf, bmg.astype(BF16))
        bmg_t_bf = bmg.T.astype(BF16)
        for r in range(SSD_HPG):
            h = g * SSD_HPG + r
            hs = slice(h * SSD_HEAD_DIM, (h + 1) * SSD_HEAD_DIM)
            seg = cum16[:, h:h + 1] - cum16_t[h:h + 1, :]
            decay = jnp.exp(jnp.where(causal, seg, -jnp.inf))
            y_intra = _dot((scores * decay).astype(BF16), v_bf[:, hs])
            s_in = ssd_ref[0, h]
            y_inter = _dot(cmg_bf, s_in.astype(BF16)) * ecum[:, hs]
            ssd_ref[0, h] = s_in * chunk_decay[:, hs] + _dot(bmg_t_bf, vte_bf[:, hs])
            mix_ref[:, hs] = y_intra + y_inter
    y = mix_ref[:, :SSD_WIDTH] + dskx_ref[...] * xs
    y = y * _silu(proj_ref[:, COL_Z:COL_Z + SSD_WIDTH])
    mix_ref[:, :SSD_WIDTH] = _rms(y) * snw_ref[...]

    cosf = cos_ref[...]
    sinf = sin_ref[...]
    rowcol = (row - col).astype(F32)
    for h in range(RET_HEADS):
        lg = LOG_GAMMA[h]
        hs = slice(h * RET_HEAD_DIM, (h + 1) * RET_HEAD_DIM)
        qh = proj_ref[:, COL_Q + h * RET_HEAD_DIM:COL_Q + (h + 1) * RET_HEAD_DIM]
        kh = proj_ref[:, COL_K + h * RET_HEAD_DIM:COL_K + (h + 1) * RET_HEAD_DIM]
        vh = proj_ref[:, COL_V + h * RET_HEAD_DIM:COL_V + (h + 1) * RET_HEAD_DIM]
        gh = proj_ref[:, COL_G + h * RET_HEAD_DIM:COL_G + (h + 1) * RET_HEAD_DIM]
        qh = qh * cosf + pltpu.roll(qh, RET_HEAD_DIM // 2, 1) * sinf
        kh = (kh * cosf + pltpu.roll(kh, RET_HEAD_DIM // 2, 1) * sinf) * (RET_HEAD_DIM ** -0.5)
        qh_bf = qh.astype(BF16)
        scores = _dot_nt(qh_bf, kh.astype(BF16))
        decay = jnp.exp(jnp.where(causal, rowcol * lg, -jnp.inf))
        y_intra = _dot((scores * decay).astype(BF16), vh.astype(BF16))
        s_in = ret_ref[0, h]
        y_inter = _dot(qh_bf, s_in.astype(BF16)) * jnp.exp((rowf + 1.0) * lg)
        v_to_end = (vh * jnp.exp((c - 1.0 - rowf) * lg)).astype(BF16)
        ret_ref[0, h] = s_in * float(np.exp(c * lg)) + _dot(kh.T.astype(BF16), v_to_end)
        mix_ref[:, SSD_WIDTH + h * RET_HEAD_DIM:SSD_WIDTH + (h + 1) * RET_HEAD_DIM] = (
            _rms(y_intra + y_inter) * _silu(gh))

    xc = conv(xpl_ref, proj_ref[:, COL_XR:COL_XR + LRU_WIDTH], lcw_ref, lcb_ref, lbuf_ref)
    xc_bf = xc.astype(BF16)
    r_gate = jax.nn.sigmoid(_dot(xc_bf, wa_ref[...]) + ba_ref[...])
    i_gate = jax.nn.sigmoid(_dot(xc_bf, wi_ref[...]) + bi_ref[...])
    log_a = -LRU_C * r_gate * _softplus(-lam_ref[...])
    a_run = jnp.exp(log_a)
    b_run = jnp.sqrt(_neg_expm1(2.0 * log_a)) * (i_gate * xc)
    rowl = lax.broadcasted_iota(jnp.int32, (c, LRU_WIDTH), 0)
    shift = 1
    while shift < c:
        valid = rowl >= shift
        a_prev = pltpu.roll(a_run, shift, 0)
        b_prev = pltpu.roll(b_run, shift, 0)
        b_run = jnp.where(valid, a_run * b_prev + b_run, b_run)
        a_run = jnp.where(valid, a_run * a_prev, a_run)
        shift *= 2
    h_all = b_run + a_run * lruh_ref[0]
    lruh_ref[0] = h_all[c - 1:c, :]
    mix_ref[:, SSD_WIDTH + RET_WIDTH:] = h_all * _gelu(proj_ref[:, COL_GATE:COL_GATE + LRU_WIDTH])

    xo_ref[...] = x_ref[...] + _dot(mix_ref[...].astype(BF16), wout_ref[...])


def _prompt_mixer(proj, x_all, cosf, sinf, wts, batch, seq):
    c = SCAN_CHUNK
    nc = seq // c
    t = x_all.shape[0]
    row_spec = lambda width: pl.BlockSpec((c, width), lambda b, s: (b * nc + s, 0))
    state_spec = lambda shape: pl.BlockSpec((1,) + shape, lambda b, s: (b,) + (0,) * len(shape))
    in_specs = [row_spec(PROJ_WIDTH), row_spec(D_MODEL),
                pl.BlockSpec((c, LANES), lambda b, s: (s, 0)),
                pl.BlockSpec((c, LANES), lambda b, s: (s, 0))]
    in_specs += [_const_spec(w.shape) for w in wts]
    out_shapes = (jax.ShapeDtypeStruct((t, D_MODEL), F32),
                  jax.ShapeDtypeStruct((batch, SSD_HEADS, SSD_STATE, SSD_HEAD_DIM), F32),
                  jax.ShapeDtypeStruct((batch, CONV_WIDTH - 1, SSD_CONV_DIM), F32),
                  jax.ShapeDtypeStruct((batch, RET_HEADS, RET_HEAD_DIM, RET_HEAD_DIM), F32),
                  jax.ShapeDtypeStruct((batch, 1, LRU_WIDTH), F32),
                  jax.ShapeDtypeStruct((batch, CONV_WIDTH - 1, LRU_WIDTH), F32))
    out_specs = (row_spec(D_MODEL),
                 state_spec((SSD_HEADS, SSD_STATE, SSD_HEAD_DIM)),
                 state_spec((CONV_WIDTH - 1, SSD_CONV_DIM)),
                 state_spec((RET_HEADS, RET_HEAD_DIM, RET_HEAD_DIM)),
                 state_spec((1, LRU_WIDTH)),
                 state_spec((CONV_WIDTH - 1, LRU_WIDTH)))
    return pl.pallas_call(
        _prompt_mixer_kernel,
        grid=(batch, nc),
        in_specs=in_specs,
        out_specs=out_specs,
        out_shape=out_shapes,
        scratch_shapes=[pltpu.VMEM((SUBLANES + c, SSD_CONV_DIM), F32),
                        pltpu.VMEM((SUBLANES + c, LRU_WIDTH), F32),
                        pltpu.VMEM((c, D_MIX), F32)],
        compiler_params=_params(("arbitrary", "arbitrary")),
        name="prompt_mixer",
    )(proj, x_all, cosf, sinf, *wts)


def _sample_mixer_kernel(proj_ref, x_ref, cos_ref, sin_ref, ssd_in_ref, sbuf_in_ref, ret_in_ref,
                         lruh_in_ref, lbuf_in_ref, scw_ref, scb_ref, dtb16_ref, alog16_ref,
                         dtbx_ref, alogx_ref, dskx_ref, snw_ref, lcw_ref, lcb_ref, wa_ref, ba_ref,
                         wi_ref, bi_ref, lam_ref, wout_ref, *rest, n_aliased):
    (xo_ref, ssd_ref, sbuf_ref, ret_ref, lruh_ref, lbuf_ref,
     vh_ref, ah_ref, bm_ref, cm_ref, yh_ref, xs_ref, mix_ref) = rest[n_aliased:]
    del dtb16_ref, alog16_ref
    bt = SAMPLE_TILE
    hb = pl.program_id(1)
    nhb = pl.num_programs(1)
    keep = CONV_WIDTH - 1

    def conv(x_new, buf_in_ref, w_ref, b_ref, buf_out_ref):
        y = b_ref[...] + x_new * w_ref[keep:keep + 1, :]
        for tap in range(keep):
            y = y + buf_in_ref[:, tap, :] * w_ref[tap:tap + 1, :]
        for tap in range(keep - 1):
            buf_out_ref[:, tap, :] = buf_in_ref[:, tap + 1, :]
        buf_out_ref[:, keep - 1, :] = x_new
        return y

    @pl.when(hb == 0)
    def _():
        xbc = _silu(conv(proj_ref[:, COL_XBC:COL_XBC + SSD_CONV_DIM], sbuf_in_ref, scw_ref, scb_ref, sbuf_ref))
        xs = xbc[:, :SSD_WIDTH]
        xs_ref[...] = xs
        dtx = _softplus(proj_ref[:, COL_DTX:COL_DTX + SSD_WIDTH] + dtbx_ref[...])
        a = jnp.exp(-dtx * jnp.exp(alogx_ref[...]))
        v = xs * dtx
        for h in range(SSD_HEADS):
            hs = slice(h * SSD_HEAD_DIM, (h + 1) * SSD_HEAD_DIM)
            vh_ref[h] = v[:, hs].T
            ah_ref[h] = a[:, hs].T
        for g in range(SSD_GROUPS):
            bm_ref[g] = xbc[:, SSD_WIDTH + g * SSD_STATE:SSD_WIDTH + (g + 1) * SSD_STATE]
            off = SSD_WIDTH + SSD_GROUPS * SSD_STATE
            cm_ref[g] = xbc[:, off + g * SSD_STATE:off + (g + 1) * SSD_STATE]

    g = (hb * SAMPLE_HEAD_BLOCK) // SSD_HPG
    bm = bm_ref[g]
    cm = cm_ref[g]
    lane = lax.broadcasted_iota(jnp.int32, (SSD_HEAD_DIM, SSD_STATE), 1)
    for r in range(SAMPLE_HEAD_BLOCK):
        h = hb * SAMPLE_HEAD_BLOCK + r
        vh_t = vh_ref[h]
        ah_t = ah_ref[h]
        y_t = jnp.zeros((SSD_HEAD_DIM, SSD_STATE), F32)
        for b in range(bt):
            s_new = ssd_in_ref[b, r] * ah_t[:, b:b + 1] + vh_t[:, b:b + 1] * bm[b:b + 1, :]
            ssd_ref[b, r] = s_new
            y_col = jnp.sum(s_new * cm[b:b + 1, :], axis=1, keepdims=True)
            y_t = jnp.where(lane == b, y_col, y_t)
        yh_ref[h] = y_t.T[:bt, :]

    @pl.when(hb == nhb - 1)
    def _():
        for h in range(SSD_HEADS):
            mix_ref[:, h * SSD_HEAD_DIM:(h + 1) * SSD_HEAD_DIM] = yh_ref[h]
        y = mix_ref[:, :SSD_WIDTH] + dskx_ref[...] * xs_ref[...]
        y = y * _silu(proj_ref[:, COL_Z:COL_Z + SSD_WIDTH])
        mix_ref[:, :SSD_WIDTH] = _rms(y) * snw_ref[...]

        cosf = cos_ref[...]
        sinf = sin_ref[...]
        for h in range(RET_HEADS):
            lg = LOG_GAMMA[h]
            qh = proj_ref[:, COL_Q + h * RET_HEAD_DIM:COL_Q + (h + 1) * RET_HEAD_DIM]
            kh = proj_ref[:, COL_K + h * RET_HEAD_DIM:COL_K + (h + 1) * RET_HEAD_DIM]
            vh = proj_ref[:, COL_V + h * RET_HEAD_DIM:COL_V + (h + 1) * RET_HEAD_DIM]
            gh = proj_ref[:, COL_G + h * RET_HEAD_DIM:COL_G + (h + 1) * RET_HEAD_DIM]
            qh = qh * cosf + pltpu.roll(qh, RET_HEAD_DIM // 2, 1) * sinf
            kh = (kh * cosf + pltpu.roll(kh, RET_HEAD_DIM // 2, 1) * sinf) * (RET_HEAD_DIM ** -0.5)
            qt = qh.T
            kt = kh.T
            hs = slice(SSD_WIDTH + h * RET_HEAD_DIM, SSD_WIDTH + (h + 1) * RET_HEAD_DIM)
            for b in range(bt):
                s_new = ret_in_ref[b, h] * float(np.exp(lg)) + kt[:, b:b + 1] * vh[b:b + 1, :]
                ret_ref[b, h] = s_new
                mix_ref[b:b + 1, hs] = jnp.sum(qt[:, b:b + 1] * s_new, axis=0, keepdims=True)
            mix_ref[:, hs] = _rms(mix_ref[:, hs]) * _silu(gh)

        xc = conv(proj_ref[:, COL_XR:COL_XR + LRU_WIDTH], lbuf_in_ref, lcw_ref, lcb_ref, lbuf_ref)
        xc_bf = xc.astype(BF16)
        r_gate = jax.nn.sigmoid(_dot(xc_bf, wa_ref[...]) + ba_ref[...])
        i_gate = jax.nn.sigmoid(_dot(xc_bf, wi_ref[...]) + bi_ref[...])
        log_a = -LRU_C * r_gate * _softplus(-lam_ref[...])
        h_new = (jnp.exp(log_a) * lruh_in_ref[...]
                 + jnp.sqrt(_neg_expm1(2.0 * log_a)) * (i_gate * xc))
        lruh_ref[...] = h_new
        mix_ref[:, SSD_WIDTH + RET_WIDTH:] = h_new * _gelu(proj_ref[:, COL_GATE:COL_GATE + LRU_WIDTH])

        xo_ref[...] = x_ref[...] + _dot(mix_ref[...].astype(BF16), wout_ref[...])


def _sample_mixer(proj, x_all, x_mixed, cosf, sinf, states, new_states, wts, layer, n_prompt, n_sample):
    bt = SAMPLE_TILE
    hbk = SAMPLE_HEAD_BLOCK
    base = n_prompt // bt
    t = x_all.shape[0]
    row_spec = lambda width: pl.BlockSpec((bt, width), lambda i, j: (base + i, 0))
    ssd_spec = pl.BlockSpec((None, bt, hbk, SSD_HEAD_DIM, SSD_STATE), lambda i, j: (layer, i, j, 0, 0))
    sbuf_spec = pl.BlockSpec((None, bt, CONV_WIDTH - 1, SSD_CONV_DIM), lambda i, j: (layer, i, 0, 0))
    ret_spec = pl.BlockSpec((None, bt, RET_HEADS, RET_HEAD_DIM, RET_HEAD_DIM), lambda i, j: (layer, i, 0, 0, 0))
    lruh_spec = pl.BlockSpec((None, bt, LRU_WIDTH), lambda i, j: (layer, i, 0))
    lbuf_spec = pl.BlockSpec((None, bt, CONV_WIDTH - 1, LRU_WIDTH), lambda i, j: (layer, i, 0, 0))
    state_specs = [ssd_spec, sbuf_spec, ret_spec, lruh_spec, lbuf_spec]
    aliased = (x_mixed,) + (tuple(new_states) if new_states is not None else ())
    in_specs = [row_spec(PROJ_WIDTH), row_spec(D_MODEL),
                _const_spec((1, LANES)), _const_spec((1, LANES))] + state_specs
    in_specs += [_const_spec(w.shape) for w in wts]
    n_in = len(in_specs)
    in_specs += [pl.BlockSpec(memory_space=pl.ANY)] * len(aliased)
    out_shapes = (jax.ShapeDtypeStruct((t, D_MODEL), F32),) + tuple(
        jax.ShapeDtypeStruct(s.shape, F32) for s in states)
    out_specs = (row_spec(D_MODEL),) + tuple(state_specs)
    return pl.pallas_call(
        functools.partial(_sample_mixer_kernel, n_aliased=len(aliased)),
        grid=(n_sample // bt, SSD_HEADS // hbk),
        in_specs=in_specs,
        out_specs=out_specs,
        out_shape=out_shapes,
        scratch_shapes=[pltpu.VMEM((SSD_HEADS, SSD_HEAD_DIM, bt), F32),
                        pltpu.VMEM((SSD_HEADS, SSD_HEAD_DIM, bt), F32),
                        pltpu.VMEM((SSD_GROUPS, bt, SSD_STATE), F32),
                        pltpu.VMEM((SSD_GROUPS, bt, SSD_STATE), F32),
                        pltpu.VMEM((SSD_HEADS, bt, SSD_HEAD_DIM), F32),
                        pltpu.VMEM((bt, SSD_WIDTH), F32),
                        pltpu.VMEM((bt, D_MIX), F32)],
        input_output_aliases={n_in + k: k for k in range(len(aliased))},
        compiler_params=_params(("arbitrary", "arbitrary")),
        name="sample_mixer",
    )(proj, x_all, cosf, sinf, *states, *wts, *aliased)


def _batcher_pairs(n):
    pairs = []

    def merge(lo, hi, r):
        step = r * 2
        if step < hi - lo:
            merge(lo, hi, step)
            merge(lo + r, hi, step)
            pairs.extend((i, i + r) for i in range(lo + r, hi - r, step))
        else:
            pairs.append((lo, lo + r))

    def sort(lo, hi):
        if hi - lo >= 1:
            mid = lo + (hi - lo) // 2
            sort(lo, mid)
            sort(mid + 1, hi)
            merge(lo, hi, 1)

    sort(0, n - 1)
    return pairs


_SORT16_PAIRS = _batcher_pairs(PEER_TOPK)


def _first(a, b):
    return (a[0] > b[0]) | ((a[0] == b[0]) & (a[1] < b[1]))


def _exchange(a, b):
    a_first = _first(a, b)
    hi = (jnp.maximum(a[0], b[0]),) + tuple(jnp.where(a_first, pa, pb) for pa, pb in zip(a[1:], b[1:]))
    lo = (jnp.minimum(a[0], b[0]),) + tuple(jnp.where(a_first, pb, pa) for pa, pb in zip(a[1:], b[1:]))
    return hi, lo


def _better(a, b):
    a_first = _first(a, b)
    return (jnp.maximum(a[0], b[0]),) + tuple(jnp.where(a_first, pa, pb) for pa, pb in zip(a[1:], b[1:]))


def _sort16(items):
    items = list(items)
    for i, j in _SORT16_PAIRS:
        items[i], items[j] = _exchange(items[i], items[j])
    return items


def _merge_top16(a, b):
    k = PEER_TOPK
    c = [_better(a[i], b[k - 1 - i]) for i in range(k)]
    d = k // 2
    while d >= 1:
        for i in range(k):
            if i & d == 0:
                c[i], c[i + d] = _exchange(c[i], c[i + d])
        d //= 2
    return c


def _top16_of_lists(vals, tags):
    k = PEER_TOPK
    best = None
    for g in range(0, len(vals), k):
        group = _sort16([(vals[i]() if callable(vals[i]) else vals[i], tags[i]) for i in range(g, g + k)])
        best = group if best is None else _merge_top16(best, group)
    return [b[0] for b in best], [b[1] for b in best]


ROUTE_TILE = SUBLANES * LANES
ROW_PITCH = PEER_KEYS + SUBLANES


def _peer_route_kernel(x_ref, n_ref, wq_ref, keys_ref,
                       xn_ref, ia_ref, ib_ref, gate_ref,
                       q_ref, z1_ref, z2_ref, za_ref, zb_ref, zg_ref):
    k = PEER_TOPK
    tile = (SUBLANES, LANES)
    xn = (_rms(x_ref[...]) * n_ref[...]).astype(BF16)
    xn_ref[...] = xn
    q = _dot(xn, wq_ref[...]).astype(BF16)
    for hc in range(2 * PEER_HEADS):
        q_ref[hc] = q[:, hc * PEER_HALF:(hc + 1) * PEER_HALF]

    def head(h, carry):
        for c, z_ref in ((0, z1_ref), (1, z2_ref)):
            keys = keys_ref[2 * h + c]
            for j in range(SUBLANES):
                z_ref[j * ROW_PITCH:j * ROW_PITCH + PEER_KEYS, :] = _dot_nt(
                    keys, q_ref[2 * h + c, j * LANES:(j + 1) * LANES, :])

        def top_keys(z_ref):
            load = lambda key: (lambda: z_ref[pl.ds(key, SUBLANES, stride=ROW_PITCH), :])
            return _top16_of_lists([load(key) for key in range(PEER_KEYS)],
                                   [jnp.full(tile, float(key), F32) for key in range(PEER_KEYS)])

        t1, i1 = top_keys(z1_ref)
        t2, i2 = top_keys(z2_ref)
        i1s = [v * float(PEER_KEYS) for v in i1]

        def pair(a, b):
            return (t1[a] + t2[b], jnp.full(tile, float(a * k + b), F32), i1s[a] + i2[b])

        pad = (jnp.full(tile, -jnp.inf, F32), jnp.full(tile, float(k * k), F32), jnp.zeros(tile, F32))
        row = lambda a: [pair(a, b) for b in range(k // (a + 1))]
        first = _merge_top16(row(0), row(1) + [pad] * (k - len(row(1))))
        second = _sort16(row(2) + row(3) + row(4) + row(5) + row(6))
        third_items = row(7) + [pair(a, 0) for a in range(8, k)]
        third = _sort16(third_items + [pad] * (k - len(third_items)))
        best = _merge_top16(_merge_top16(first, second), third)

        e = [jnp.exp(item[0] - best[0][0]) for item in best]
        z = e[0]
        for v in e[1:]:
            z = z + v
        inv = 1.0 / z
        for r in range(k):
            sel = h * k + r
            expert = best[r][2]
            a_idx = jnp.floor(expert * (1.0 / PEER_KEYS))
            za_ref[pl.ds(sel, SUBLANES, stride=ROW_PITCH), :] = a_idx
            zb_ref[pl.ds(sel, SUBLANES, stride=ROW_PITCH), :] = expert - a_idx * float(PEER_KEYS)
            zg_ref[pl.ds(sel, SUBLANES, stride=ROW_PITCH), :] = e[r] * inv
        return carry

    lax.fori_loop(0, PEER_HEADS, head, 0)
    for j in range(SUBLANES):
        rows = slice(j * LANES, (j + 1) * LANES)
        src = slice(j * ROW_PITCH, j * ROW_PITCH + PEER_HEADS * k)
        ia_ref[rows, :] = za_ref[src, :].T
        ib_ref[rows, :] = zb_ref[src, :].T
        gate_ref[rows, :] = zg_ref[src, :].T


def _peer_route(x, n2, wq, keys):
    t = x.shape[0]
    tm = ROUTE_TILE
    sel_spec = pl.BlockSpec((tm, PEER_HEADS * PEER_TOPK), lambda i: (i, 0))
    sel_shape = jax.ShapeDtypeStruct((t, PEER_HEADS * PEER_TOPK), F32)
    z_scratch = pltpu.VMEM((SUBLANES * ROW_PITCH, LANES), F32)
    return pl.pallas_call(
        _peer_route_kernel,
        grid=(pl.cdiv(t, tm),),
        in_specs=[pl.BlockSpec((tm, D_MODEL), lambda i: (i, 0)),
                  _const_spec((1, D_MODEL)),
                  _const_spec(wq.shape),
                  _const_spec(keys.shape)],
        out_specs=(pl.BlockSpec((tm, D_MODEL), lambda i: (i, 0)), sel_spec, sel_spec, sel_spec),
        out_shape=(jax.ShapeDtypeStruct((t, D_MODEL), BF16), sel_shape, sel_shape, sel_shape),
        scratch_shapes=[pltpu.VMEM((2 * PEER_HEADS, tm, PEER_HALF), BF16),
                        z_scratch, z_scratch, z_scratch, z_scratch, z_scratch],
        compiler_params=_params(("arbitrary",)),
        name="peer_route",
    )(x, n2, wq, keys)


def _peer_mlp_kernel(xn_ref, u0_ref, u_ref, v_ref, ia_ref, ib_ref, gate_ref, x_ref, fw_ref, *rest,
                     final, n_prompt_tiles, n_sample):
    if final:
        o_ref, os_ref, acc_ref, act_a_ref, act_b_ref, w_ref, g_ref = rest
    else:
        o_ref, acc_ref, act_a_ref, act_b_ref, w_ref, g_ref = rest
    e = pl.program_id(1)
    tm = xn_ref.shape[0]
    rows_per_tile = EXPERT_TILE // PEER_KEYS

    @pl.when(e == 0)
    def _():
        acc_ref[...] = jnp.zeros_like(acc_ref)
        key = lax.broadcasted_iota(jnp.int32, (PEER_KEYS, PEER_HEADS * PEER_TOPK), 0).astype(F32)

        def build(t, carry):
            gate = gate_ref[pl.ds(t, 1), :]
            sel_gate = jnp.where(key == ia_ref[pl.ds(t, 1), :], gate, 0.0)
            hi = sel_gate.astype(BF16)
            lo = (sel_gate - hi.astype(F32)).astype(BF16)
            onehot = jnp.where(key == ib_ref[pl.ds(t, 1), :], 1.0, 0.0).astype(BF16)
            g_ref[pl.ds(pl.multiple_of(t * ROW_PITCH, SUBLANES), PEER_KEYS), :] = _dot_nt(
                jnp.concatenate([hi, lo], axis=1), jnp.concatenate([onehot, onehot], axis=1))
            return carry

        lax.fori_loop(0, tm, build, 0, unroll=BUILD_UNROLL)
        act_a_ref[...] = _dot_nt(xn_ref[...], u0_ref[...])

    def step(act_ref, act_next_ref):
        act_next_ref[...] = _dot_nt(xn_ref[...], u_ref[...])
        for j in range(rows_per_tile):
            i1 = e * rows_per_tile + j
            cols = slice(j * PEER_KEYS, (j + 1) * PEER_KEYS)
            gate = g_ref[pl.ds(i1, tm, stride=ROW_PITCH), :]
            w_ref[:, cols] = (gate * _gelu(act_ref[:, cols])).astype(BF16)
        acc_ref[...] += _dot(w_ref[...], v_ref[...])

    @pl.when(e % 2 == 0)
    def _():
        step(act_a_ref, act_b_ref)

    @pl.when(e % 2 == 1)
    def _():
        step(act_b_ref, act_a_ref)

    @pl.when(e == pl.num_programs(1) - 1)
    def _():
        y = x_ref[...] + acc_ref[...]
        if not final:
            o_ref[...] = y
            return
        y = _rms(y) * fw_ref[...]
        tile = pl.program_id(0)

        @pl.when(tile < n_prompt_tiles)
        def _():
            o_ref[...] = y

        @pl.when(tile == pl.num_programs(0) - 1)
        def _():
            os_ref[...] = y[tm - n_sample:, :]


def _peer_mlp(xn, u_bf, v_bf, route, x, fw, tm, layer, final, n_prompt):
    t = x.shape[0]
    n_sample = t - n_prompt
    assert n_sample <= tm and n_sample % SUBLANES == 0
    n_prompt_tiles = pl.cdiv(n_prompt, tm)
    n_tiles = PEER_EXPERTS // EXPERT_TILE
    assert n_tiles % 2 == 0
    ia, ib, gate = route
    tok_spec = lambda width: pl.BlockSpec((tm, width), lambda i, e: (i, 0))
    if final:
        out_specs = (pl.BlockSpec((tm, D_MODEL), lambda i, e: (jnp.minimum(i, n_prompt_tiles - 1), 0)),
                     pl.BlockSpec((n_sample, D_MODEL), lambda i, e: (0, 0)))
        out_shape = (jax.ShapeDtypeStruct((n_prompt, D_MODEL), F32),
                     jax.ShapeDtypeStruct((n_sample, D_MODEL), F32))
    else:
        out_specs = tok_spec(D_MODEL)
        out_shape = jax.ShapeDtypeStruct((t, D_MODEL), F32)
    return pl.pallas_call(
        functools.partial(_peer_mlp_kernel, final=final, n_prompt_tiles=n_prompt_tiles, n_sample=n_sample),
        grid=(t // tm, n_tiles),
        in_specs=[tok_spec(D_MODEL),
                  pl.BlockSpec((None, EXPERT_TILE, D_MODEL), lambda i, e: (layer, 0, 0),
                               pipeline_mode=pl.Buffered(1)),
                  pl.BlockSpec((None, EXPERT_TILE, D_MODEL), lambda i, e: (layer, jnp.minimum(e + 1, n_tiles - 1), 0)),
                  pl.BlockSpec((None, EXPERT_TILE, D_MODEL), lambda i, e: (layer, e, 0)),
                  tok_spec(PEER_HEADS * PEER_TOPK), tok_spec(PEER_HEADS * PEER_TOPK),
                  tok_spec(PEER_HEADS * PEER_TOPK),
                  tok_spec(D_MODEL),
                  _const_spec((1, D_MODEL))],
        out_specs=out_specs,
        out_shape=out_shape,
        scratch_shapes=[pltpu.VMEM((tm, D_MODEL), F32),
                        pltpu.VMEM((tm, EXPERT_TILE), F32),
                        pltpu.VMEM((tm, EXPERT_TILE), F32),
                        pltpu.VMEM((tm, EXPERT_TILE), BF16),
                        pltpu.VMEM((tm * ROW_PITCH, PEER_KEYS), F32)],
        compiler_params=_params(("arbitrary", "arbitrary")),
        name="peer_mlp",
    )(xn, u_bf, u_bf, v_bf, ia, ib, gate, x, fw)


def _token_tile(t):
    for tm in (512, 384, 256, 128):
        if t % tm == 0:
            return tm
    raise ValueError(f"token count {t} must be a multiple of {LANES}")


def _rope_tables(pos):
    half = RET_HEAD_DIM // 2
    inv = ROPE_BASE ** (-jnp.arange(half, dtype=F32) / half)
    ang = pos.astype(F32)[:, None] * inv[None, :]
    cos = jnp.cos(ang)
    sin = jnp.sin(ang)
    return jnp.concatenate([cos, cos], axis=1), jnp.concatenate([-sin, sin], axis=1)


def _block_diag(w):
    eye = jnp.eye(LRU_BLOCKS, dtype=w.dtype)
    return jnp.einsum('hij,hg->higj', w, eye).reshape(LRU_WIDTH, LRU_WIDTH)


def kernel(x_prompt, x_sample, state_ssd, state_ssd_conv, state_ret, state_lru, state_lru_conv, norm1_w, w_in, ssd_conv_w, ssd_conv_b, ssd_dt_bias, ssd_a_log, ssd_d, ssd_norm_w, lru_conv_w, lru_conv_b, lru_wa, lru_ba, lru_wi, lru_bi, lru_lambda, w_out, norm2_w, peer_wq, peer_keys, peer_u, peer_v, final_norm_w):
    batch, seq, _ = x_prompt.shape
    n_sample, dec_seq, _ = x_sample.shape
    depth = w_in.shape[0]
    assert dec_seq == 1 and seq % SCAN_CHUNK == 0 and n_sample % SAMPLE_TILE == 0
    n_prompt = batch * seq
    t = n_prompt + n_sample
    tm = _token_tile(t)

    x = jnp.concatenate([x_prompt.reshape(n_prompt, D_MODEL), x_sample.reshape(n_sample, D_MODEL)], axis=0)
    cos_p, sin_p = _rope_tables(jnp.arange(seq))
    cos_s, sin_s = _rope_tables(PAST_LEN + jnp.arange(dec_seq))
    row = lambda a: a.reshape(1, -1).astype(F32)
    per_channel = lambda a: jnp.repeat(a.astype(F32), SSD_HEAD_DIM).reshape(1, SSD_WIDTH)
    per_head = lambda a: jnp.pad(a.astype(F32), (0, LANES - SSD_HEADS)).reshape(1, LANES)
    off = SSD_WIDTH + SSD_CONV_DIM

    carried = (jnp.swapaxes(state_ssd, -1, -2), state_ssd_conv, state_ret, state_lru, state_lru_conv)
    u_bf = peer_u.astype(BF16)
    v_bf = peer_v.astype(BF16)
    new_p, st_s = [], None
    for i in range(depth):
        wi_ = w_in[i]
        dt_cols = wi_[:, off:off + SSD_HEADS]
        w_in_p = jnp.concatenate(
            [wi_[:, :off],
             jnp.pad(dt_cols, ((0, 0), (0, LANES - SSD_HEADS))),
             jnp.repeat(dt_cols, SSD_HEAD_DIM, axis=1),
             wi_[:, off + SSD_HEADS:]], axis=1).astype(BF16)
        wts = (ssd_conv_w[i], row(ssd_conv_b[i]), per_head(ssd_dt_bias[i]), per_head(ssd_a_log[i]),
               per_channel(ssd_dt_bias[i]), per_channel(ssd_a_log[i]), per_channel(ssd_d[i]),
               row(ssd_norm_w[i]), lru_conv_w[i], row(lru_conv_b[i]),
               _block_diag(lru_wa[i]).astype(BF16), row(lru_ba[i]),
               _block_diag(lru_wi[i]).astype(BF16), row(lru_bi[i]), row(lru_lambda[i]),
               w_out[i].astype(BF16))

        proj = _inproj(x, row(norm1_w[i]), w_in_p, tm)
        x_mixed, *st_p = _prompt_mixer(proj, x, cos_p, sin_p, wts, batch, seq)
        st_p[3] = st_p[3].reshape(batch, LRU_WIDTH)
        x, *st_s = _sample_mixer(proj, x, x_mixed, cos_s, sin_s, carried, st_s, wts, i, n_prompt, n_sample)
        new_p.append(st_p)

        keys_bf = peer_keys[i].reshape(PEER_HEADS * 2, PEER_KEYS, PEER_HALF).astype(BF16)
        xn, *route = _peer_route(x, row(norm2_w[i]), peer_wq[i].astype(BF16), keys_bf)
        x = _peer_mlp(xn, u_bf, v_bf, route, x, row(final_norm_w), tm, i, i == depth - 1, n_prompt)

    y_prompt, y_sample = x
    stk = lambda states, j: jnp.stack([s[j] for s in states])
    return (y_prompt.reshape(batch, seq, D_MODEL), y_sample.reshape(n_sample, dec_seq, D_MODEL),
            stk(new_p, 0), stk(new_p, 1), stk(new_p, 2), stk(new_p, 3), stk(new_p, 4),
            jnp.swapaxes(st_s[0], -1, -2), *st_s[1:])
```

```python
import functools

import numpy as np
import jax
import jax.numpy as jnp
from jax import lax
from jax.experimental import pallas as pl
from jax.experimental.pallas import tpu as pltpu

F32 = jnp.float32
BF16 = jnp.bfloat16

D_MODEL = 1024
PAST_LEN = 16384
D_MIX = 2 * D_MODEL
CONV_WIDTH = 4
NORM_EPS = 1e-6
SSD_WIDTH = D_MIX // 2
SSD_HEAD_DIM = 64
SSD_HEADS = SSD_WIDTH // SSD_HEAD_DIM
SSD_GROUPS = 2
SSD_HPG = SSD_HEADS // SSD_GROUPS
SSD_STATE = 128
SSD_CONV_DIM = SSD_WIDTH + 2 * SSD_GROUPS * SSD_STATE
RET_WIDTH = D_MIX // 4
RET_HEADS = 4
RET_HEAD_DIM = RET_WIDTH // RET_HEADS
ROPE_BASE = 10000.0
LRU_WIDTH = D_MIX // 4
LRU_BLOCKS = 8
LRU_BLOCK_DIM = LRU_WIDTH // LRU_BLOCKS
LRU_C = 8.0
PEER_KEYS = 128
PEER_EXPERTS = PEER_KEYS * PEER_KEYS
PEER_HEADS = 8
PEER_TOPK = 16
PEER_QUERY_DIM = 256
PEER_HALF = PEER_QUERY_DIM // 2

LANES = 128
SUBLANES = 8
VMEM_LIMIT = 56 * 1024 * 1024

COL_Z = 0
COL_XBC = COL_Z + SSD_WIDTH
COL_DT16 = COL_XBC + SSD_CONV_DIM
COL_Q = COL_DT16 + LANES
COL_K = COL_Q + RET_WIDTH
COL_V = COL_K + RET_WIDTH
COL_G = COL_V + RET_WIDTH
COL_GATE = COL_G + RET_WIDTH
COL_XR = COL_GATE + LRU_WIDTH
PROJ_WIDTH = COL_XR + LRU_WIDTH

SCAN_CHUNK = 128
SAMPLE_TILE = SUBLANES
SAMPLE_HEAD_BLOCK = 4
EXPERT_TILE = 1024
BUILD_UNROLL = 16
LOG_GAMMA = [float(np.log(1.0 - 2.0 ** (-5.0 - h))) for h in range(RET_HEADS)]

_NT = (((1,), (1,)), ((), ()))


def _rms(x):
    return x * lax.rsqrt(jnp.mean(x * x, axis=-1, keepdims=True) + NORM_EPS)


def _softplus(x):
    return jnp.maximum(x, 0.0) + jnp.log1p(jnp.exp(-jnp.abs(x)))


def _silu(x):
    return x * jax.nn.sigmoid(x)


def _gelu(x):
    return 0.5 * x * (1.0 + lax.erf(x * np.float32(np.sqrt(0.5))))


def _neg_expm1(x):
    return -jnp.tanh(0.5 * x) * (jnp.exp(x) + 1.0)


def _dot(a, b):
    return jnp.dot(a, b, preferred_element_type=F32)


def _dot_nt(a, b):
    return lax.dot_general(a, b, _NT, preferred_element_type=F32)


def _dot_exact(a, b):
    return jnp.dot(a, b, preferred_element_type=F32, precision=lax.Precision.HIGHEST)


def _params(semantics):
    return pltpu.CompilerParams(dimension_semantics=semantics, vmem_limit_bytes=VMEM_LIMIT)


def _const_spec(shape):
    zeros = (0,) * len(shape)
    return pl.BlockSpec(shape, lambda *_: zeros)


def _inproj_kernel(x_ref, n_ref, w_ref, o_ref):
    h = (_rms(x_ref[...]) * n_ref[...]).astype(BF16)
    step = 8 * LANES
    for j in range(0, PROJ_WIDTH, step):
        hi = min(j + step, PROJ_WIDTH)
        o_ref[:, j:hi] = _dot(h, w_ref[:, j:hi])


def _inproj(x, n1, w_in_p, tm):
    t = x.shape[0]
    return pl.pallas_call(
        _inproj_kernel,
        grid=(t // tm,),
        in_specs=[pl.BlockSpec((tm, D_MODEL), lambda i: (i, 0)),
                  _const_spec((1, D_MODEL)),
                  pl.BlockSpec((D_MODEL, PROJ_WIDTH), lambda i: (0, 0), pipeline_mode=pl.Buffered(1))],
        out_specs=pl.BlockSpec((tm, PROJ_WIDTH), lambda i: (i, 0)),
        out_shape=jax.ShapeDtypeStruct((t, PROJ_WIDTH), F32),
        compiler_params=_params(("arbitrary",)),
        name="inproj",
    )(x, n1, w_in_p)


def _prompt_mixer_kernel(proj_ref, x_ref, cos_ref, sin_ref, scw_ref, scb_ref, dtb16_ref, alog16_ref,
                         expand_ref, dskx_ref, snw_ref, lcw_ref, lcb_ref, wa_ref, ba_ref,
                         wi_ref, bi_ref, lam_ref, wout_ref,
                         xo_ref, ssd_ref, sbuf_ref, ret_ref, lruh_ref, lbuf_ref,
                         xps_ref, xpl_ref, mix_ref):
    c = SCAN_CHUNK
    step = pl.program_id(1)
    pad = SUBLANES
    keep = CONV_WIDTH - 1

    @pl.when(step == 0)
    def _():
        ssd_ref[...] = jnp.zeros_like(ssd_ref)
        ret_ref[...] = jnp.zeros_like(ret_ref)
        lruh_ref[...] = jnp.zeros_like(lruh_ref)
        xps_ref[0:pad, :] = jnp.zeros((pad, SSD_CONV_DIM), F32)
        xpl_ref[0:pad, :] = jnp.zeros((pad, LRU_WIDTH), F32)

    def conv(xp_ref, x_new, w_ref, b_ref, buf_out_ref):
        xp_ref[pad:pad + c, :] = x_new
        y = b_ref[...] + xp_ref[pad - keep:pad - keep + c, :] * w_ref[0:1, :]
        for tap in range(1, CONV_WIDTH):
            y = y + xp_ref[pad - keep + tap:pad - keep + tap + c, :] * w_ref[tap:tap + 1, :]
        tail = xp_ref[pad + c - keep:pad + c, :]
        xp_ref[pad - keep:pad, :] = tail
        buf_out_ref[0] = tail
        return y

    row = lax.broadcasted_iota(jnp.int32, (c, c), 0)
    col = lax.broadcasted_iota(jnp.int32, (c, c), 1)
    causal = row >= col
    tri = causal.astype(F32)
    rowf = lax.broadcasted_iota(jnp.int32, (c, LANES), 0).astype(F32)

    xbc = _silu(conv(xps_ref, proj_ref[:, COL_XBC:COL_XBC + SSD_CONV_DIM], scw_ref, scb_ref, sbuf_ref))
    xs = xbc[:, :SSD_WIDTH]
    bm = xbc[:, SSD_WIDTH:SSD_WIDTH + SSD_GROUPS * SSD_STATE]
    cm = xbc[:, SSD_WIDTH + SSD_GROUPS * SSD_STATE:]
    dt16 = _softplus(proj_ref[:, COL_DT16:COL_DT16 + LANES] + dtb16_ref[...])
    cum16 = _dot_exact(tri, -dt16 * jnp.exp(alog16_ref[...]))
    cum16_t = cum16.T
    wide = _dot_exact(jnp.concatenate([dt16, cum16], axis=0), expand_ref[...])
    dtx = wide[:c]
    cumx = wide[c:]
    v = xs * dtx
    v_bf = v.astype(BF16)
    cum_last = cumx[c - 1:c, :]
    vte_bf = (v * jnp.exp(cum_last - cumx)).astype(BF16)
    ecum = jnp.exp(cumx)
    chunk_decay = jnp.exp(cum_last)
    for g in range(SSD_GROUPS):
        bmg = bm[:, g * SSD_STATE:(g + 1) * SSD_STATE]
        cmg_bf = cm[:, g * SSD_STATE:(g + 1) * SSD_STATE].astype(BF16)
        scores = _dot_nt(cmg_bf, bmg.astype(BF16))
        bmg_t_bf = bmg.T.astype(BF16)
        for r in range(SSD_HPG):
            h = g * SSD_HPG + r
            hs = slice(h * SSD_HEAD_DIM, (h + 1) * SSD_HEAD_DIM)
            seg = cum16[:, h:h + 1] - cum16_t[h:h + 1, :]
            decay = jnp.exp(jnp.where(causal, seg, -jnp.inf))
            y_intra = _dot((scores * decay).astype(BF16), v_bf[:, hs])
            s_in = ssd_ref[0, h]
            y_inter = _dot(cmg_bf, s_in.astype(BF16)) * ecum[:, hs]
            ssd_ref[0, h] = s_in * chunk_decay[:, hs] + _dot(bmg_t_bf, vte_bf[:, hs])
            mix_ref[:, hs] = y_intra + y_inter
    y = mix_ref[:, :SSD_WIDTH] + dskx_ref[...] * xs
    y = y * _silu(proj_ref[:, COL_Z:COL_Z + SSD_WIDTH])
    mix_ref[:, :SSD_WIDTH] = _rms(y) * snw_ref[...]

    cosf = cos_ref[...]
    sinf = sin_ref[...]
    rowcol = (row - col).astype(F32)
    for h in range(RET_HEADS):
        lg = LOG_GAMMA[h]
        hs = slice(h * RET_HEAD_DIM, (h + 1) * RET_HEAD_DIM)
        qh = proj_ref[:, COL_Q + h * RET_HEAD_DIM:COL_Q + (h + 1) * RET_HEAD_DIM]
        kh = proj_ref[:, COL_K + h * RET_HEAD_DIM:COL_K + (h + 1) * RET_HEAD_DIM]
        vh = proj_ref[:, COL_V + h * RET_HEAD_DIM:COL_V + (h + 1) * RET_HEAD_DIM]
        gh = proj_ref[:, COL_G + h * RET_HEAD_DIM:COL_G + (h + 1) * RET_HEAD_DIM]
        qh = qh * cosf + pltpu.roll(qh, RET_HEAD_DIM // 2, 1) * sinf
        kh = (kh * cosf + pltpu.roll(kh, RET_HEAD_DIM // 2, 1) * sinf) * (RET_HEAD_DIM ** -0.5)
        qh_bf = qh.astype(BF16)
        scores = _dot_nt(qh_bf, kh.astype(BF16))
        decay = jnp.exp(jnp.where(causal, rowcol * lg, -jnp.inf))
        y_intra = _dot((scores * decay).astype(BF16), vh.astype(BF16))
        s_in = ret_ref[0, h]
        y_inter = _dot(qh_bf, s_in.astype(BF16)) * jnp.exp((rowf + 1.0) * lg)
        v_to_end = (vh * jnp.exp((c - 1.0 - rowf) * lg)).astype(BF16)
        ret_ref[0, h] = s_in * float(np.exp(c * lg)) + _dot(kh.T.astype(BF16), v_to_end)
        mix_ref[:, SSD_WIDTH + h * RET_HEAD_DIM:SSD_WIDTH + (h + 1) * RET_HEAD_DIM] = (
            _rms(y_intra + y_inter) * _silu(gh))

    xc = conv(xpl_ref, proj_ref[:, COL_XR:COL_XR + LRU_WIDTH], lcw_ref, lcb_ref, lbuf_ref)
    xc_bf = xc.astype(BF16)
    r_gate = jax.nn.sigmoid(_dot(xc_bf, wa_ref[...]) + ba_ref[...])
    i_gate = jax.nn.sigmoid(_dot(xc_bf, wi_ref[...]) + bi_ref[...])
    log_a = -LRU_C * r_gate * _softplus(-lam_ref[...])
    a_run = jnp.exp(log_a)
    b_run = jnp.sqrt(_neg_expm1(2.0 * log_a)) * (i_gate * xc)
    rowl = lax.broadcasted_iota(jnp.int32, (c, LRU_WIDTH), 0)
    shift = 1
    while shift < c:
        valid = rowl >= shift
        a_prev = pltpu.roll(a_run, shift, 0)
        b_prev = pltpu.roll(b_run, shift, 0)
        b_run = jnp.where(valid, a_run * b_prev + b_run, b_run)
        a_run = jnp.where(valid, a_run * a_prev, a_run)
        shift *= 2
    h_all = b_run + a_run * lruh_ref[0]
    lruh_ref[0] = h_all[c - 1:c, :]
    mix_ref[:, SSD_WIDTH + RET_WIDTH:] = h_all * _gelu(proj_ref[:, COL_GATE:COL_GATE + LRU_WIDTH])

    xo_ref[...] = x_ref[...] + _dot(mix_ref[...].astype(BF16), wout_ref[...])


def _prompt_mixer(proj, x_all, cosf, sinf, wts, batch, seq):
    c = SCAN_CHUNK
    nc = seq // c
    t = x_all.shape[0]
    row_spec = lambda width: pl.BlockSpec((c, width), lambda b, s: (b * nc + s, 0))
    state_spec = lambda shape: pl.BlockSpec((1,) + shape, lambda b, s: (b,) + (0,) * len(shape))
    in_specs = [row_spec(PROJ_WIDTH), row_spec(D_MODEL),
                pl.BlockSpec((c, LANES), lambda b, s: (s, 0)),
                pl.BlockSpec((c, LANES), lambda b, s: (s, 0))]
    in_specs += [_const_spec(w.shape) for w in wts]
    out_shapes = (jax.ShapeDtypeStruct((t, D_MODEL), F32),
                  jax.ShapeDtypeStruct((batch, SSD_HEADS, SSD_STATE, SSD_HEAD_DIM), F32),
                  jax.ShapeDtypeStruct((batch, CONV_WIDTH - 1, SSD_CONV_DIM), F32),
                  jax.ShapeDtypeStruct((batch, RET_HEADS, RET_HEAD_DIM, RET_HEAD_DIM), F32),
                  jax.ShapeDtypeStruct((batch, 1, LRU_WIDTH), F32),
                  jax.ShapeDtypeStruct((batch, CONV_WIDTH - 1, LRU_WIDTH), F32))
    out_specs = (row_spec(D_MODEL),
                 state_spec((SSD_HEADS, SSD_STATE, SSD_HEAD_DIM)),
                 state_spec((CONV_WIDTH - 1, SSD_CONV_DIM)),
                 state_spec((RET_HEADS, RET_HEAD_DIM, RET_HEAD_DIM)),
                 state_spec((1, LRU_WIDTH)),
                 state_spec((CONV_WIDTH - 1, LRU_WIDTH)))
    return pl.pallas_call(
        _prompt_mixer_kernel,
        grid=(batch, nc),
        in_specs=in_specs,
        out_specs=out_specs,
        out_shape=out_shapes,
        scratch_shapes=[pltpu.VMEM((SUBLANES + c, SSD_CONV_DIM), F32),
                        pltpu.VMEM((SUBLANES + c, LRU_WIDTH), F32),
                        pltpu.VMEM((c, D_MIX), F32)],
        compiler_params=_params(("arbitrary", "arbitrary")),
        name="prompt_mixer",
    )(proj, x_all, cosf, sinf, *wts)


def _sample_mixer_kernel(proj_ref, x_ref, cos_ref, sin_ref, ssd_in_ref, sbuf_in_ref, ret_in_ref,
                         lruh_in_ref, lbuf_in_ref, scw_ref, scb_ref, dtb16_ref, alog16_ref,
                         expand_ref, dskx_ref, snw_ref, lcw_ref, lcb_ref, wa_ref, ba_ref,
                         wi_ref, bi_ref, lam_ref, wout_ref, *rest, n_aliased):
    (xo_ref, ssd_ref, sbuf_ref, ret_ref, lruh_ref, lbuf_ref,
     vh_ref, ah_ref, bm_ref, cm_ref, yh_ref, xs_ref, mix_ref) = rest[n_aliased:]
    bt = SAMPLE_TILE
    hb = pl.program_id(1)
    nhb = pl.num_programs(1)
    keep = CONV_WIDTH - 1

    def conv(x_new, buf_in_ref, w_ref, b_ref, buf_out_ref):
        y = b_ref[...] + x_new * w_ref[keep:keep + 1, :]
        for tap in range(keep):
            y = y + buf_in_ref[:, tap, :] * w_ref[tap:tap + 1, :]
        for tap in range(keep - 1):
            buf_out_ref[:, tap, :] = buf_in_ref[:, tap + 1, :]
        buf_out_ref[:, keep - 1, :] = x_new
        return y

    @pl.when(hb == 0)
    def _():
        xbc = _silu(conv(proj_ref[:, COL_XBC:COL_XBC + SSD_CONV_DIM], sbuf_in_ref, scw_ref, scb_ref, sbuf_ref))
        xs = xbc[:, :SSD_WIDTH]
        xs_ref[...] = xs
        dt16 = _softplus(proj_ref[:, COL_DT16:COL_DT16 + LANES] + dtb16_ref[...])
        a16 = jnp.exp(-dt16 * jnp.exp(alog16_ref[...]))
        wide = _dot_exact(jnp.concatenate([dt16, a16], axis=0), expand_ref[...])
        dtx = wide[:bt]
        a = wide[bt:]
        v = xs * dtx
        for h in range(SSD_HEADS):
            hs = slice(h * SSD_HEAD_DIM, (h + 1) * SSD_HEAD_DIM)
            vh_ref[h] = v[:, hs].T
            ah_ref[h] = a[:, hs].T
        for g in range(SSD_GROUPS):
            bm_ref[g] = xbc[:, SSD_WIDTH + g * SSD_STATE:SSD_WIDTH + (g + 1) * SSD_STATE]
            off = SSD_WIDTH + SSD_GROUPS * SSD_STATE
            cm_ref[g] = xbc[:, off + g * SSD_STATE:off + (g + 1) * SSD_STATE]

    g = (hb * SAMPLE_HEAD_BLOCK) // SSD_HPG
    bm = bm_ref[g]
    cm = cm_ref[g]
    cm_bf = cm.astype(BF16)
    lane = lax.broadcasted_iota(jnp.int32, (SSD_HEAD_DIM, bt), 1)
    for r in range(SAMPLE_HEAD_BLOCK):
        h = hb * SAMPLE_HEAD_BLOCK + r
        vh_t = vh_ref[h]
        ah_t = ah_ref[h]
        y_t = jnp.zeros((SSD_HEAD_DIM, bt), F32)
        for b in range(bt):
            s_new = ssd_in_ref[b, r] * ah_t[:, b:b + 1] + vh_t[:, b:b + 1] * bm[b:b + 1, :]
            ssd_ref[b, r] = s_new
            y_all = _dot_nt(s_new.astype(BF16), cm_bf)
            y_t = jnp.where(lane == b, y_all, y_t)
        yh_ref[h] = y_t.T

    @pl.when(hb == nhb - 1)
    def _():
        for h in range(SSD_HEADS):
            mix_ref[:, h * SSD_HEAD_DIM:(h + 1) * SSD_HEAD_DIM] = yh_ref[h]
        y = mix_ref[:, :SSD_WIDTH] + dskx_ref[...] * xs_ref[...]
        y = y * _silu(proj_ref[:, COL_Z:COL_Z + SSD_WIDTH])
        mix_ref[:, :SSD_WIDTH] = _rms(y) * snw_ref[...]

        cosf = cos_ref[...]
        sinf = sin_ref[...]
        for h in range(RET_HEADS):
            lg = LOG_GAMMA[h]
            qh = proj_ref[:, COL_Q + h * RET_HEAD_DIM:COL_Q + (h + 1) * RET_HEAD_DIM]
            kh = proj_ref[:, COL_K + h * RET_HEAD_DIM:COL_K + (h + 1) * RET_HEAD_DIM]
            vh = proj_ref[:, COL_V + h * RET_HEAD_DIM:COL_V + (h + 1) * RET_HEAD_DIM]
            gh = proj_ref[:, COL_G + h * RET_HEAD_DIM:COL_G + (h + 1) * RET_HEAD_DIM]
            qh = qh * cosf + pltpu.roll(qh, RET_HEAD_DIM // 2, 1) * sinf
            kh = (kh * cosf + pltpu.roll(kh, RET_HEAD_DIM // 2, 1) * sinf) * (RET_HEAD_DIM ** -0.5)
            qt = qh.T
            kt = kh.T
            hs = slice(SSD_WIDTH + h * RET_HEAD_DIM, SSD_WIDTH + (h + 1) * RET_HEAD_DIM)
            for b in range(bt):
                s_new = ret_in_ref[b, h] * float(np.exp(lg)) + kt[:, b:b + 1] * vh[b:b + 1, :]
                ret_ref[b, h] = s_new
                mix_ref[b:b + 1, hs] = jnp.sum(qt[:, b:b + 1] * s_new, axis=0, keepdims=True)
            mix_ref[:, hs] = _rms(mix_ref[:, hs]) * _silu(gh)

        xc = conv(proj_ref[:, COL_XR:COL_XR + LRU_WIDTH], lbuf_in_ref, lcw_ref, lcb_ref, lbuf_ref)
        xc_bf = xc.astype(BF16)
        r_gate = jax.nn.sigmoid(_dot(xc_bf, wa_ref[...]) + ba_ref[...])
        i_gate = jax.nn.sigmoid(_dot(xc_bf, wi_ref[...]) + bi_ref[...])
        log_a = -LRU_C * r_gate * _softplus(-lam_ref[...])
        h_new = (jnp.exp(log_a) * lruh_in_ref[...]
                 + jnp.sqrt(_neg_expm1(2.0 * log_a)) * (i_gate * xc))
        lruh_ref[...] = h_new
        mix_ref[:, SSD_WIDTH + RET_WIDTH:] = h_new * _gelu(proj_ref[:, COL_GATE:COL_GATE + LRU_WIDTH])

        xo_ref[...] = x_ref[...] + _dot(mix_ref[...].astype(BF16), wout_ref[...])


def _sample_mixer(proj, x_all, x_mixed, cosf, sinf, states, new_states, wts, layer, n_prompt, n_sample):
    bt = SAMPLE_TILE
    hbk = SAMPLE_HEAD_BLOCK
    base = n_prompt // bt
    t = x_all.shape[0]
    row_spec = lambda width: pl.BlockSpec((bt, width), lambda i, j: (base + i, 0))
    ssd_spec = pl.BlockSpec((None, bt, hbk, SSD_HEAD_DIM, SSD_STATE), lambda i, j: (layer, i, j, 0, 0))
    sbuf_spec = pl.BlockSpec((None, bt, CONV_WIDTH - 1, SSD_CONV_DIM), lambda i, j: (layer, i, 0, 0))
    ret_spec = pl.BlockSpec((None, bt, RET_HEADS, RET_HEAD_DIM, RET_HEAD_DIM), lambda i, j: (layer, i, 0, 0, 0))
    lruh_spec = pl.BlockSpec((None, bt, LRU_WIDTH), lambda i, j: (layer, i, 0))
    lbuf_spec = pl.BlockSpec((None, bt, CONV_WIDTH - 1, LRU_WIDTH), lambda i, j: (layer, i, 0, 0))
    state_specs = [ssd_spec, sbuf_spec, ret_spec, lruh_spec, lbuf_spec]
    aliased = (x_mixed,) + (tuple(new_states) if new_states is not None else ())
    in_specs = [row_spec(PROJ_WIDTH), row_spec(D_MODEL),
                _const_spec((1, LANES)), _const_spec((1, LANES))] + state_specs
    in_specs += [_const_spec(w.shape) for w in wts]
    n_in = len(in_specs)
    in_specs += [pl.BlockSpec(memory_space=pl.ANY)] * len(aliased)
    out_shapes = (jax.ShapeDtypeStruct((t, D_MODEL), F32),) + tuple(
        jax.ShapeDtypeStruct(s.shape, F32) for s in states)
    out_specs = (row_spec(D_MODEL),) + tuple(state_specs)
    return pl.pallas_call(
        functools.partial(_sample_mixer_kernel, n_aliased=len(aliased)),
        grid=(n_sample // bt, SSD_HEADS // hbk),
        in_specs=in_specs,
        out_specs=out_specs,
        out_shape=out_shapes,
        scratch_shapes=[pltpu.VMEM((SSD_HEADS, SSD_HEAD_DIM, bt), F32),
                        pltpu.VMEM((SSD_HEADS, SSD_HEAD_DIM, bt), F32),
                        pltpu.VMEM((SSD_GROUPS, bt, SSD_STATE), F32),
                        pltpu.VMEM((SSD_GROUPS, bt, SSD_STATE), F32),
                        pltpu.VMEM((SSD_HEADS, bt, SSD_HEAD_DIM), F32),
                        pltpu.VMEM((bt, SSD_WIDTH), F32),
                        pltpu.VMEM((bt, D_MIX), F32)],
        input_output_aliases={n_in + k: k for k in range(len(aliased))},
        compiler_params=_params(("arbitrary", "arbitrary")),
        name="sample_mixer",
    )(proj, x_all, cosf, sinf, *states, *wts, *aliased)


def _batcher_pairs(n):
    pairs = []

    def merge(lo, hi, r):
        step = r * 2
        if step < hi - lo:
            merge(lo, hi, step)
            merge(lo + r, hi, step)
            pairs.extend((i, i + r) for i in range(lo + r, hi - r, step))
        else:
            pairs.append((lo, lo + r))

    def sort(lo, hi):
        if hi - lo >= 1:
            mid = lo + (hi - lo) // 2
            sort(lo, mid)
            sort(mid + 1, hi)
            merge(lo, hi, 1)

    sort(0, n - 1)
    return pairs


_SORT16_PAIRS = _batcher_pairs(PEER_TOPK)


def _first(a, b):
    return (a[0] > b[0]) | ((a[0] == b[0]) & (a[1] < b[1]))


def _exchange(a, b):
    a_first = _first(a, b)
    hi = (jnp.maximum(a[0], b[0]),) + tuple(jnp.where(a_first, pa, pb) for pa, pb in zip(a[1:], b[1:]))
    lo = (jnp.minimum(a[0], b[0]),) + tuple(jnp.where(a_first, pb, pa) for pa, pb in zip(a[1:], b[1:]))
    return hi, lo


def _better(a, b):
    a_first = _first(a, b)
    return (jnp.maximum(a[0], b[0]),) + tuple(jnp.where(a_first, pa, pb) for pa, pb in zip(a[1:], b[1:]))


def _sort16(items):
    items = list(items)
    for i, j in _SORT16_PAIRS:
        items[i], items[j] = _exchange(items[i], items[j])
    return items


def _merge_top16(a, b):
    k = PEER_TOPK
    c = [_better(a[i], b[k - 1 - i]) for i in range(k)]
    d = k // 2
    while d >= 1:
        for i in range(k):
            if i & d == 0:
                c[i], c[i + d] = _exchange(c[i], c[i + d])
        d //= 2
    return c


def _top16_of_lists(vals, tags):
    k = PEER_TOPK
    best = None
    for g in range(0, len(vals), k):
        group = _sort16([(vals[i]() if callable(vals[i]) else vals[i], tags[i]) for i in range(g, g + k)])
        best = group if best is None else _merge_top16(best, group)
    return [b[0] for b in best], [b[1] for b in best]


ROUTE_TILE = SUBLANES * LANES
ROW_PITCH = PEER_KEYS + SUBLANES


def _peer_route_kernel(x_ref, n_ref, wq_ref, keys_ref,
                       xn_ref, ia_ref, ib_ref, gate_ref,
                       q_ref, z1_ref, z2_ref, za_ref, zb_ref, zg_ref):
    k = PEER_TOPK
    tile = (SUBLANES, LANES)
    xn = (_rms(x_ref[...]) * n_ref[...]).astype(BF16)
    xn_ref[...] = xn
    q = _dot(xn, wq_ref[...]).astype(BF16)
    for hc in range(2 * PEER_HEADS):
        q_ref[hc] = q[:, hc * PEER_HALF:(hc + 1) * PEER_HALF]

    def head(h, carry):
        for c, z_ref in ((0, z1_ref), (1, z2_ref)):
            keys = keys_ref[2 * h + c]
            for j in range(SUBLANES):
                z_ref[j * ROW_PITCH:j * ROW_PITCH + PEER_KEYS, :] = _dot_nt(
                    keys, q_ref[2 * h + c, j * LANES:(j + 1) * LANES, :])

        def top_keys(z_ref):
            load = lambda key: (lambda: z_ref[pl.ds(key, SUBLANES, stride=ROW_PITCH), :])
            return _top16_of_lists([load(key) for key in range(PEER_KEYS)],
                                   [jnp.full(tile, float(key), F32) for key in range(PEER_KEYS)])

        t1, i1 = top_keys(z1_ref)
        t2, i2 = top_keys(z2_ref)
        i1s = [v * float(PEER_KEYS) for v in i1]

        def pair(a, b):
            return (t1[a] + t2[b], jnp.full(tile, float(a * k + b), F32), i1s[a] + i2[b])

        pad = (jnp.full(tile, -jnp.inf, F32), jnp.full(tile, float(k * k), F32), jnp.zeros(tile, F32))
        row = lambda a: [pair(a, b) for b in range(k // (a + 1))]
        first = _merge_top16(row(0), row(1) + [pad] * (k - len(row(1))))
        second = _sort16(row(2) + row(3) + row(4) + row(5) + row(6))
        third_items = row(7) + [pair(a, 0) for a in range(8, k)]
        third = _sort16(third_items + [pad] * (k - len(third_items)))
        best = _merge_top16(_merge_top16(first, second), third)

        e = [jnp.exp(item[0] - best[0][0]) for item in best]
        z = e[0]
        for v in e[1:]:
            z = z + v
        inv = 1.0 / z
        for r in range(k):
            sel = h * k + r
            expert = best[r][2]
            a_idx = jnp.floor(expert * (1.0 / PEER_KEYS))
            za_ref[pl.ds(sel, SUBLANES, stride=ROW_PITCH), :] = a_idx
            zb_ref[pl.ds(sel, SUBLANES, stride=ROW_PITCH), :] = expert - a_idx * float(PEER_KEYS)
            zg_ref[pl.ds(sel, SUBLANES, stride=ROW_PITCH), :] = e[r] * inv
        return carry

    lax.fori_loop(0, PEER_HEADS, head, 0)
    for j in range(SUBLANES):
        rows = slice(j * LANES, (j + 1) * LANES)
        src = slice(j * ROW_PITCH, j * ROW_PITCH + PEER_HEADS * k)
        ia_ref[rows, :] = za_ref[src, :].T
        ib_ref[rows, :] = zb_ref[src, :].T
        gate_ref[rows, :] = zg_ref[src, :].T


def _peer_route(x, n2, wq, keys):
    t = x.shape[0]
    tm = ROUTE_TILE
    sel_spec = pl.BlockSpec((tm, PEER_HEADS * PEER_TOPK), lambda i: (i, 0))
    sel_shape = jax.ShapeDtypeStruct((t, PEER_HEADS * PEER_TOPK), F32)
    z_scratch = pltpu.VMEM((SUBLANES * ROW_PITCH, LANES), F32)
    return pl.pallas_call(
        _peer_route_kernel,
        grid=(pl.cdiv(t, tm),),
        in_specs=[pl.BlockSpec((tm, D_MODEL), lambda i: (i, 0)),
                  _const_spec((1, D_MODEL)),
                  _const_spec(wq.shape),
                  _const_spec(keys.shape)],
        out_specs=(pl.BlockSpec((tm, D_MODEL), lambda i: (i, 0)), sel_spec, sel_spec, sel_spec),
        out_shape=(jax.ShapeDtypeStruct((t, D_MODEL), BF16), sel_shape, sel_shape, sel_shape),
        scratch_shapes=[pltpu.VMEM((2 * PEER_HEADS, tm, PEER_HALF), BF16),
                        z_scratch, z_scratch, z_scratch, z_scratch, z_scratch],
        compiler_params=_params(("arbitrary",)),
        name="peer_route",
    )(x, n2, wq, keys)


def _peer_mlp_kernel(xn_ref, u0_ref, u_ref, v_ref, ia_ref, ib_ref, gate_ref, x_ref, fw_ref, *rest,
                     final, n_prompt_tiles, n_sample):
    if final:
        o_ref, os_ref, acc_ref, act_a_ref, act_b_ref, w_ref, g_ref = rest
    else:
        o_ref, acc_ref, act_a_ref, act_b_ref, w_ref, g_ref = rest
    e = pl.program_id(1)
    tm = xn_ref.shape[0]
    rows_per_tile = EXPERT_TILE // PEER_KEYS

    @pl.when(e == 0)
    def _():
        acc_ref[...] = jnp.zeros_like(acc_ref)
        key = lax.broadcasted_iota(jnp.int32, (PEER_KEYS, PEER_HEADS * PEER_TOPK), 0).astype(F32)

        def build(t, carry):
            gate = gate_ref[pl.ds(t, 1), :]
            sel_gate = jnp.where(key == ia_ref[pl.ds(t, 1), :], gate, 0.0)
            hi = sel_gate.astype(BF16)
            lo = (sel_gate - hi.astype(F32)).astype(BF16)
            onehot = jnp.where(key == ib_ref[pl.ds(t, 1), :], 1.0, 0.0).astype(BF16)
            g_ref[pl.ds(pl.multiple_of(t * ROW_PITCH, SUBLANES), PEER_KEYS), :] = _dot_nt(
                jnp.concatenate([hi, lo], axis=1), jnp.concatenate([onehot, onehot], axis=1))
            return carry

        lax.fori_loop(0, tm, build, 0, unroll=BUILD_UNROLL)
        act_a_ref[...] = _dot_nt(xn_ref[...], u0_ref[...])

    def step(act_ref, act_next_ref):
        act_next_ref[...] = _dot_nt(xn_ref[...], u_ref[...])
        for j in range(rows_per_tile):
            i1 = e * rows_per_tile + j
            cols = slice(j * PEER_KEYS, (j + 1) * PEER_KEYS)
            gate = g_ref[pl.ds(i1, tm, stride=ROW_PITCH), :]
            w_ref[:, cols] = (gate * _gelu(act_ref[:, cols])).astype(BF16)
        acc_ref[...] += _dot(w_ref[...], v_ref[...])

    @pl.when(e % 2 == 0)
    def _():
        step(act_a_ref, act_b_ref)

    @pl.when(e % 2 == 1)
    def _():
        step(act_b_ref, act_a_ref)

    @pl.when(e == pl.num_programs(1) - 1)
    def _():
        y = x_ref[...] + acc_ref[...]
        if not final:
            o_ref[...] = y
            return
        y = _rms(y) * fw_ref[...]
        tile = pl.program_id(0)

        @pl.when(tile < n_prompt_tiles)
        def _():
            o_ref[...] = y

        @pl.when(tile == pl.num_programs(0) - 1)
        def _():
            os_ref[...] = y[tm - n_sample:, :]


def _peer_mlp(xn, u_bf, v_bf, route, x, fw, tm, layer, final, n_prompt):
    t = x.shape[0]
    n_sample = t - n_prompt
    assert n_sample <= tm and n_sample % SUBLANES == 0
    n_prompt_tiles = pl.cdiv(n_prompt, tm)
    n_tiles = PEER_EXPERTS // EXPERT_TILE
    assert n_tiles % 2 == 0
    ia, ib, gate = route
    tok_spec = lambda width: pl.BlockSpec((tm, width), lambda i, e: (i, 0))
    if final:
        out_specs = (pl.BlockSpec((tm, D_MODEL), lambda i, e: (jnp.minimum(i, n_prompt_tiles - 1), 0)),
                     pl.BlockSpec((n_sample, D_MODEL), lambda i, e: (0, 0)))
        out_shape = (jax.ShapeDtypeStruct((n_prompt, D_MODEL), F32),
                     jax.ShapeDtypeStruct((n_sample, D_MODEL), F32))
    else:
        out_specs = tok_spec(D_MODEL)
        out_shape = jax.ShapeDtypeStruct((t, D_MODEL), F32)
    return pl.pallas_call(
        functools.partial(_peer_mlp_kernel, final=final, n_prompt_tiles=n_prompt_tiles, n_sample=n_sample),
        grid=(t // tm, n_tiles),
        in_specs=[tok_spec(D_MODEL),
                  pl.BlockSpec((None, EXPERT_TILE, D_MODEL), lambda i, e: (layer, 0, 0),
                               pipeline_mode=pl.Buffered(1)),
                  pl.BlockSpec((None, EXPERT_TILE, D_MODEL), lambda i, e: (layer, jnp.minimum(e + 1, n_tiles - 1), 0)),
                  pl.BlockSpec((None, EXPERT_TILE, D_MODEL), lambda i, e: (layer, e, 0)),
                  tok_spec(PEER_HEADS * PEER_TOPK), tok_spec(PEER_HEADS * PEER_TOPK),
                  tok_spec(PEER_HEADS * PEER_TOPK),
                  tok_spec(D_MODEL),
                  _const_spec((1, D_MODEL))],
        out_specs=out_specs,
        out_shape=out_shape,
        scratch_shapes=[pltpu.VMEM((tm, D_MODEL), F32),
                        pltpu.VMEM((tm, EXPERT_TILE), F32),
                        pltpu.VMEM((tm, EXPERT_TILE), F32),
                        pltpu.VMEM((tm, EXPERT_TILE), BF16),
                        pltpu.VMEM((tm * ROW_PITCH, PEER_KEYS), F32)],
        compiler_params=_params(("arbitrary", "arbitrary")),
        name="peer_mlp",
    )(xn, u_bf, u_bf, v_bf, ia, ib, gate, x, fw)


def _token_tile(t):
    for tm in (512, 384, 256, 128):
        if t % tm == 0:
            return tm
    raise ValueError(f"token count {t} must be a multiple of {LANES}")


def _rope_tables(pos):
    half = RET_HEAD_DIM // 2
    inv = ROPE_BASE ** (-jnp.arange(half, dtype=F32) / half)
    ang = pos.astype(F32)[:, None] * inv[None, :]
    cos = jnp.cos(ang)
    sin = jnp.sin(ang)
    return jnp.concatenate([cos, cos], axis=1), jnp.concatenate([-sin, sin], axis=1)


def _block_diag(w):
    eye = jnp.eye(LRU_BLOCKS, dtype=w.dtype)
    return jnp.einsum('hij,hg->higj', w, eye).reshape(LRU_WIDTH, LRU_WIDTH)


def kernel(x_prompt, x_sample, state_ssd, state_ssd_conv, state_ret, state_lru, state_lru_conv, norm1_w, w_in, ssd_conv_w, ssd_conv_b, ssd_dt_bias, ssd_a_log, ssd_d, ssd_norm_w, lru_conv_w, lru_conv_b, lru_wa, lru_ba, lru_wi, lru_bi, lru_lambda, w_out, norm2_w, peer_wq, peer_keys, peer_u, peer_v, final_norm_w):
    batch, seq, _ = x_prompt.shape
    n_sample, dec_seq, _ = x_sample.shape
    depth = w_in.shape[0]
    assert dec_seq == 1 and seq % SCAN_CHUNK == 0 and n_sample % SAMPLE_TILE == 0
    n_prompt = batch * seq
    t = n_prompt + n_sample
    tm = _token_tile(t)

    x = jnp.concatenate([x_prompt.reshape(n_prompt, D_MODEL), x_sample.reshape(n_sample, D_MODEL)], axis=0)
    cos_p, sin_p = _rope_tables(jnp.arange(seq))
    cos_s, sin_s = _rope_tables(PAST_LEN + jnp.arange(dec_seq))
    row = lambda a: a.reshape(1, -1).astype(F32)
    per_channel = lambda a: jnp.repeat(a.astype(F32), SSD_HEAD_DIM).reshape(1, SSD_WIDTH)
    per_head = lambda a: jnp.pad(a.astype(F32), (0, LANES - SSD_HEADS)).reshape(1, LANES)
    expand = (jnp.arange(LANES)[:, None] == jnp.arange(SSD_WIDTH)[None, :] // SSD_HEAD_DIM).astype(F32)
    off = SSD_WIDTH + SSD_CONV_DIM

    carried = (jnp.swapaxes(state_ssd, -1, -2), state_ssd_conv, state_ret, state_lru, state_lru_conv)
    u_bf = peer_u.astype(BF16)
    v_bf = peer_v.astype(BF16)
    new_p, st_s = [], None
    for i in range(depth):
        wi_ = w_in[i]
        dt_cols = wi_[:, off:off + SSD_HEADS]
        w_in_p = jnp.concatenate(
            [wi_[:, :off],
             jnp.pad(dt_cols, ((0, 0), (0, LANES - SSD_HEADS))),
             wi_[:, off + SSD_HEADS:]], axis=1).astype(BF16)
        wts = (ssd_conv_w[i], row(ssd_conv_b[i]), per_head(ssd_dt_bias[i]), per_head(ssd_a_log[i]),
               expand, per_channel(ssd_d[i]),
               row(ssd_norm_w[i]), lru_conv_w[i], row(lru_conv_b[i]),
               _block_diag(lru_wa[i]).astype(BF16), row(lru_ba[i]),
               _block_diag(lru_wi[i]).astype(BF16), row(lru_bi[i]), row(lru_lambda[i]),
               w_out[i].astype(BF16))

        proj = _inproj(x, row(norm1_w[i]), w_in_p, tm)
        x_mixed, *st_p = _prompt_mixer(proj, x, cos_p, sin_p, wts, batch, seq)
        st_p[3] = st_p[3].reshape(batch, LRU_WIDTH)
        x, *st_s = _sample_mixer(proj, x, x_mixed, cos_s, sin_s, carried, st_s, wts, i, n_prompt, n_sample)
        new_p.append(st_p)

        keys_bf = peer_keys[i].reshape(PEER_HEADS * 2, PEER_KEYS, PEER_HALF).astype(BF16)
        xn, *route = _peer_route(x, row(norm2_w[i]), peer_wq[i].astype(BF16), keys_bf)
        x = _peer_mlp(xn, u_bf, v_bf, route, x, row(final_norm_w), tm, i, i == depth - 1, n_prompt)

    y_prompt, y_sample = x
    stk = lambda states, j: jnp.stack([s[j] for s in states])
    return (y_prompt.reshape(batch, seq, D_MODEL), y_sample.reshape(n_sample, dec_seq, D_MODEL),
            stk(new_p, 0), stk(new_p, 1), stk(new_p, 2), stk(new_p, 3), stk(new_p, 4),
            jnp.swapaxes(st_s[0], -1, -2), *st_s[1:])
```

```python
import functools

import numpy as np
import jax
import jax.numpy as jnp
from jax import lax
from jax.experimental import pallas as pl
from jax.experimental.pallas import tpu as pltpu

F32 = jnp.float32
BF16 = jnp.bfloat16

D_MODEL = 1024
PAST_LEN = 16384
D_MIX = 2 * D_MODEL
CONV_WIDTH = 4
NORM_EPS = 1e-6
SSD_WIDTH = D_MIX // 2
SSD_HEAD_DIM = 64
SSD_HEADS = SSD_WIDTH // SSD_HEAD_DIM
SSD_GROUPS = 2
SSD_HPG = SSD_HEADS // SSD_GROUPS
SSD_STATE = 128
SSD_CONV_DIM = SSD_WIDTH + 2 * SSD_GROUPS * SSD_STATE
RET_WIDTH = D_MIX // 4
RET_HEADS = 4
RET_HEAD_DIM = RET_WIDTH // RET_HEADS
ROPE_BASE = 10000.0
LRU_WIDTH = D_MIX // 4
LRU_BLOCKS = 8
LRU_BLOCK_DIM = LRU_WIDTH // LRU_BLOCKS
LRU_C = 8.0
PEER_KEYS = 128
PEER_EXPERTS = PEER_KEYS * PEER_KEYS
PEER_HEADS = 8
PEER_TOPK = 16
PEER_QUERY_DIM = 256
PEER_HALF = PEER_QUERY_DIM // 2

LANES = 128
SUBLANES = 8
VMEM_LIMIT = 56 * 1024 * 1024

COL_Z = 0
COL_XBC = COL_Z + SSD_WIDTH
COL_DT16 = COL_XBC + SSD_CONV_DIM
COL_Q = COL_DT16 + LANES
COL_K = COL_Q + RET_WIDTH
COL_V = COL_K + RET_WIDTH
COL_G = COL_V + RET_WIDTH
COL_GATE = COL_G + RET_WIDTH
COL_XR = COL_GATE + LRU_WIDTH
PROJ_WIDTH = COL_XR + LRU_WIDTH

SCAN_CHUNK = 128
SAMPLE_TILE = SUBLANES
SAMPLE_HEAD_BLOCK = 4
EXPERT_TILE = 1024
BUILD_UNROLL = 64
LOG_GAMMA = [float(np.log(1.0 - 2.0 ** (-5.0 - h))) for h in range(RET_HEADS)]

_NT = (((1,), (1,)), ((), ()))


def _rms(x):
    return x * lax.rsqrt(jnp.mean(x * x, axis=-1, keepdims=True) + NORM_EPS)


def _softplus(x):
    return jnp.maximum(x, 0.0) + jnp.log1p(jnp.exp(-jnp.abs(x)))


def _silu(x):
    return x * jax.nn.sigmoid(x)


def _gelu(x):
    return 0.5 * x * (1.0 + lax.erf(x * np.float32(np.sqrt(0.5))))


def _neg_expm1(x):
    return -jnp.tanh(0.5 * x) * (jnp.exp(x) + 1.0)


def _dot(a, b):
    return jnp.dot(a, b, preferred_element_type=F32)


def _dot_nt(a, b):
    return lax.dot_general(a, b, _NT, preferred_element_type=F32)


def _dot_exact(a, b):
    return jnp.dot(a, b, preferred_element_type=F32, precision=lax.Precision.HIGHEST)


def _params(semantics):
    return pltpu.CompilerParams(dimension_semantics=semantics, vmem_limit_bytes=VMEM_LIMIT)


def _const_spec(shape):
    zeros = (0,) * len(shape)
    return pl.BlockSpec(shape, lambda *_: zeros)


def _inproj_kernel(x_ref, n_ref, w_ref, o_ref):
    h = (_rms(x_ref[...]) * n_ref[...]).astype(BF16)
    step = 8 * LANES
    for j in range(0, PROJ_WIDTH, step):
        hi = min(j + step, PROJ_WIDTH)
        o_ref[:, j:hi] = _dot(h, w_ref[:, j:hi])


def _inproj(x, n1, w_in_p, tm):
    t = x.shape[0]
    return pl.pallas_call(
        _inproj_kernel,
        grid=(t // tm,),
        in_specs=[pl.BlockSpec((tm, D_MODEL), lambda i: (i, 0)),
                  _const_spec((1, D_MODEL)),
                  pl.BlockSpec((D_MODEL, PROJ_WIDTH), lambda i: (0, 0), pipeline_mode=pl.Buffered(1))],
        out_specs=pl.BlockSpec((tm, PROJ_WIDTH), lambda i: (i, 0)),
        out_shape=jax.ShapeDtypeStruct((t, PROJ_WIDTH), F32),
        compiler_params=_params(("arbitrary",)),
        name="inproj",
    )(x, n1, w_in_p)


def _prompt_mixer_kernel(proj_ref, x_ref, cos_ref, sin_ref, scw_ref, scb_ref, dtb16_ref, alog16_ref,
                         expand_ref, dskx_ref, snw_ref, lcw_ref, lcb_ref, wa_ref, ba_ref,
                         wi_ref, bi_ref, lam_ref, wout_ref,
                         xo_ref, ssd_ref, sbuf_ref, ret_ref, lruh_ref, lbuf_ref,
                         xps_ref, xpl_ref, mix_ref):
    c = SCAN_CHUNK
    step = pl.program_id(1)
    pad = SUBLANES
    keep = CONV_WIDTH - 1

    @pl.when(step == 0)
    def _():
        ssd_ref[...] = jnp.zeros_like(ssd_ref)
        ret_ref[...] = jnp.zeros_like(ret_ref)
        lruh_ref[...] = jnp.zeros_like(lruh_ref)
        xps_ref[0:pad, :] = jnp.zeros((pad, SSD_CONV_DIM), F32)
        xpl_ref[0:pad, :] = jnp.zeros((pad, LRU_WIDTH), F32)

    def conv(xp_ref, x_new, w_ref, b_ref, buf_out_ref):
        xp_ref[pad:pad + c, :] = x_new
        y = b_ref[...] + xp_ref[pad - keep:pad - keep + c, :] * w_ref[0:1, :]
        for tap in range(1, CONV_WIDTH):
            y = y + xp_ref[pad - keep + tap:pad - keep + tap + c, :] * w_ref[tap:tap + 1, :]
        tail = xp_ref[pad + c - keep:pad + c, :]
        xp_ref[pad - keep:pad, :] = tail
        buf_out_ref[0] = tail
        return y

    row = lax.broadcasted_iota(jnp.int32, (c, c), 0)
    col = lax.broadcasted_iota(jnp.int32, (c, c), 1)
    causal = row >= col
    tri = causal.astype(F32)
    rowf = lax.broadcasted_iota(jnp.int32, (c, LANES), 0).astype(F32)

    xbc = _silu(conv(xps_ref, proj_ref[:, COL_XBC:COL_XBC + SSD_CONV_DIM], scw_ref, scb_ref, sbuf_ref))
    xs = xbc[:, :SSD_WIDTH]
    bm = xbc[:, SSD_WIDTH:SSD_WIDTH + SSD_GROUPS * SSD_STATE]
    cm = xbc[:, SSD_WIDTH + SSD_GROUPS * SSD_STATE:]
    dt16 = _softplus(proj_ref[:, COL_DT16:COL_DT16 + LANES] + dtb16_ref[...])
    cum16 = _dot_exact(tri, -dt16 * jnp.exp(alog16_ref[...]))
    cum16_t = cum16.T
    wide = _dot_exact(jnp.concatenate([dt16, cum16], axis=0), expand_ref[...])
    dtx = wide[:c]
    cumx = wide[c:]
    v = xs * dtx
    v_bf = v.astype(BF16)
    cum_last = cumx[c - 1:c, :]
    vte_bf = (v * jnp.exp(cum_last - cumx)).astype(BF16)
    ecum = jnp.exp(cumx)
    chunk_decay = jnp.exp(cum_last)
    for g in range(SSD_GROUPS):
        bmg = bm[:, g * SSD_STATE:(g + 1) * SSD_STATE]
        cmg_bf = cm[:, g * SSD_STATE:(g + 1) * SSD_STATE].astype(BF16)
        scores = _dot_nt(cmg_bf, bmg.astype(BF16))
        bmg_t_bf = bmg.T.astype(BF16)
        for r in range(SSD_HPG):
            h = g * SSD_HPG + r
            hs = slice(h * SSD_HEAD_DIM, (h + 1) * SSD_HEAD_DIM)
            seg = cum16[:, h:h + 1] - cum16_t[h:h + 1, :]
            decay = jnp.exp(jnp.where(causal, seg, -jnp.inf))
            y_intra = _dot((scores * decay).astype(BF16), v_bf[:, hs])
            s_in = ssd_ref[0, h]
            y_inter = _dot(cmg_bf, s_in.astype(BF16)) * ecum[:, hs]
            ssd_ref[0, h] = s_in * chunk_decay[:, hs] + _dot(bmg_t_bf, vte_bf[:, hs])
            mix_ref[:, hs] = y_intra + y_inter
    y = mix_ref[:, :SSD_WIDTH] + dskx_ref[...] * xs
    y = y * _silu(proj_ref[:, COL_Z:COL_Z + SSD_WIDTH])
    mix_ref[:, :SSD_WIDTH] = _rms(y) * snw_ref[...]

    cosf = cos_ref[...]
    sinf = sin_ref[...]
    rowcol = (row - col).astype(F32)
    for h in range(RET_HEADS):
        lg = LOG_GAMMA[h]
        hs = slice(h * RET_HEAD_DIM, (h + 1) * RET_HEAD_DIM)
        qh = proj_ref[:, COL_Q + h * RET_HEAD_DIM:COL_Q + (h + 1) * RET_HEAD_DIM]
        kh = proj_ref[:, COL_K + h * RET_HEAD_DIM:COL_K + (h + 1) * RET_HEAD_DIM]
        vh = proj_ref[:, COL_V + h * RET_HEAD_DIM:COL_V + (h + 1) * RET_HEAD_DIM]
        gh = proj_ref[:, COL_G + h * RET_HEAD_DIM:COL_G + (h + 1) * RET_HEAD_DIM]
        qh = qh * cosf + pltpu.roll(qh, RET_HEAD_DIM // 2, 1) * sinf
        kh = (kh * cosf + pltpu.roll(kh, RET_HEAD_DIM // 2, 1) * sinf) * (RET_HEAD_DIM ** -0.5)
        qh_bf = qh.astype(BF16)
        scores = _dot_nt(qh_bf, kh.astype(BF16))
        decay = jnp.exp(jnp.where(causal, rowcol * lg, -jnp.inf))
        y_intra = _dot((scores * decay).astype(BF16), vh.astype(BF16))
        s_in = ret_ref[0, h]
        y_inter = _dot(qh_bf, s_in.astype(BF16)) * jnp.exp((rowf + 1.0) * lg)
        v_to_end = (vh * jnp.exp((c - 1.0 - rowf) * lg)).astype(BF16)
        ret_ref[0, h] = s_in * float(np.exp(c * lg)) + _dot(kh.T.astype(BF16), v_to_end)
        mix_ref[:, SSD_WIDTH + h * RET_HEAD_DIM:SSD_WIDTH + (h + 1) * RET_HEAD_DIM] = (
            _rms(y_intra + y_inter) * _silu(gh))

    xc = conv(xpl_ref, proj_ref[:, COL_XR:COL_XR + LRU_WIDTH], lcw_ref, lcb_ref, lbuf_ref)
    xc_bf = xc.astype(BF16)
    r_gate = jax.nn.sigmoid(_dot(xc_bf, wa_ref[...]) + ba_ref[...])
    i_gate = jax.nn.sigmoid(_dot(xc_bf, wi_ref[...]) + bi_ref[...])
    log_a = -LRU_C * r_gate * _softplus(-lam_ref[...])
    a_run = jnp.exp(log_a)
    b_run = jnp.sqrt(_neg_expm1(2.0 * log_a)) * (i_gate * xc)
    rowl = lax.broadcasted_iota(jnp.int32, (c, LRU_WIDTH), 0)
    shift = 1
    while shift < c:
        valid = rowl >= shift
        a_prev = pltpu.roll(a_run, shift, 0)
        b_prev = pltpu.roll(b_run, shift, 0)
        b_run = jnp.where(valid, a_run * b_prev + b_run, b_run)
        a_run = jnp.where(valid, a_run * a_prev, a_run)
        shift *= 2
    h_all = b_run + a_run * lruh_ref[0]
    lruh_ref[0] = h_all[c - 1:c, :]
    mix_ref[:, SSD_WIDTH + RET_WIDTH:] = h_all * _gelu(proj_ref[:, COL_GATE:COL_GATE + LRU_WIDTH])

    xo_ref[...] = x_ref[...] + _dot(mix_ref[...].astype(BF16), wout_ref[...])


def _prompt_mixer(proj, x_all, cosf, sinf, wts, batch, seq):
    c = SCAN_CHUNK
    nc = seq // c
    t = x_all.shape[0]
    row_spec = lambda width: pl.BlockSpec((c, width), lambda b, s: (b * nc + s, 0))
    state_spec = lambda shape: pl.BlockSpec((1,) + shape, lambda b, s: (b,) + (0,) * len(shape))
    in_specs = [row_spec(PROJ_WIDTH), row_spec(D_MODEL),
                pl.BlockSpec((c, LANES), lambda b, s: (s, 0)),
                pl.BlockSpec((c, LANES), lambda b, s: (s, 0))]
    in_specs += [_const_spec(w.shape) for w in wts]
    out_shapes = (jax.ShapeDtypeStruct((t, D_MODEL), F32),
                  jax.ShapeDtypeStruct((batch, SSD_HEADS, SSD_STATE, SSD_HEAD_DIM), F32),
                  jax.ShapeDtypeStruct((batch, CONV_WIDTH - 1, SSD_CONV_DIM), F32),
                  jax.ShapeDtypeStruct((batch, RET_HEADS, RET_HEAD_DIM, RET_HEAD_DIM), F32),
                  jax.ShapeDtypeStruct((batch, 1, LRU_WIDTH), F32),
                  jax.ShapeDtypeStruct((batch, CONV_WIDTH - 1, LRU_WIDTH), F32))
    out_specs = (row_spec(D_MODEL),
                 state_spec((SSD_HEADS, SSD_STATE, SSD_HEAD_DIM)),
                 state_spec((CONV_WIDTH - 1, SSD_CONV_DIM)),
                 state_spec((RET_HEADS, RET_HEAD_DIM, RET_HEAD_DIM)),
                 state_spec((1, LRU_WIDTH)),
                 state_spec((CONV_WIDTH - 1, LRU_WIDTH)))
    return pl.pallas_call(
        _prompt_mixer_kernel,
        grid=(batch, nc),
        in_specs=in_specs,
        out_specs=out_specs,
        out_shape=out_shapes,
        scratch_shapes=[pltpu.VMEM((SUBLANES + c, SSD_CONV_DIM), F32),
                        pltpu.VMEM((SUBLANES + c, LRU_WIDTH), F32),
                        pltpu.VMEM((c, D_MIX), F32)],
        compiler_params=_params(("arbitrary", "arbitrary")),
        name="prompt_mixer",
    )(proj, x_all, cosf, sinf, *wts)


def _sample_mixer_kernel(proj_ref, x_ref, cos_ref, sin_ref, ssd_in_ref, sbuf_in_ref, ret_in_ref,
                         lruh_in_ref, lbuf_in_ref, scw_ref, scb_ref, dtb16_ref, alog16_ref,
                         expand_ref, dskx_ref, snw_ref, lcw_ref, lcb_ref, wa_ref, ba_ref,
                         wi_ref, bi_ref, lam_ref, wout_ref, *rest, n_aliased):
    (xo_ref, ssd_ref, sbuf_ref, ret_ref, lruh_ref, lbuf_ref,
     vh_ref, ah_ref, bm_ref, cm_ref, yh_ref, xs_ref, mix_ref) = rest[n_aliased:]
    bt = SAMPLE_TILE
    hb = pl.program_id(1)
    nhb = pl.num_programs(1)
    keep = CONV_WIDTH - 1

    def conv(x_new, buf_in_ref, w_ref, b_ref, buf_out_ref):
        y = b_ref[...] + x_new * w_ref[keep:keep + 1, :]
        for tap in range(keep):
            y = y + buf_in_ref[:, tap, :] * w_ref[tap:tap + 1, :]
        for tap in range(keep - 1):
            buf_out_ref[:, tap, :] = buf_in_ref[:, tap + 1, :]
        buf_out_ref[:, keep - 1, :] = x_new
        return y

    @pl.when(hb == 0)
    def _():
        xbc = _silu(conv(proj_ref[:, COL_XBC:COL_XBC + SSD_CONV_DIM], sbuf_in_ref, scw_ref, scb_ref, sbuf_ref))
        xs = xbc[:, :SSD_WIDTH]
        xs_ref[...] = xs
        dt16 = _softplus(proj_ref[:, COL_DT16:COL_DT16 + LANES] + dtb16_ref[...])
        a16 = jnp.exp(-dt16 * jnp.exp(alog16_ref[...]))
        wide = _dot_exact(jnp.concatenate([dt16, a16], axis=0), expand_ref[...])
        dtx = wide[:bt]
        a = wide[bt:]
        v = xs * dtx
        for h in range(SSD_HEADS):
            hs = slice(h * SSD_HEAD_DIM, (h + 1) * SSD_HEAD_DIM)
            vh_ref[h] = v[:, hs].T
            ah_ref[h] = a[:, hs].T
        for g in range(SSD_GROUPS):
            bm_ref[g] = xbc[:, SSD_WIDTH + g * SSD_STATE:SSD_WIDTH + (g + 1) * SSD_STATE]
            off = SSD_WIDTH + SSD_GROUPS * SSD_STATE
            cm_ref[g] = xbc[:, off + g * SSD_STATE:off + (g + 1) * SSD_STATE]

    g = (hb * SAMPLE_HEAD_BLOCK) // SSD_HPG
    bm = bm_ref[g]
    cm = cm_ref[g]
    cm_bf = cm.astype(BF16)
    lane = lax.broadcasted_iota(jnp.int32, (SSD_HEAD_DIM, bt), 1)
    for r in range(SAMPLE_HEAD_BLOCK):
        h = hb * SAMPLE_HEAD_BLOCK + r
        vh_t = vh_ref[h]
        ah_t = ah_ref[h]
        y_t = jnp.zeros((SSD_HEAD_DIM, bt), F32)
        for b in range(bt):
            s_new = ssd_in_ref[b, r] * ah_t[:, b:b + 1] + vh_t[:, b:b + 1] * bm[b:b + 1, :]
            ssd_ref[b, r] = s_new
            y_all = _dot_nt(s_new.astype(BF16), cm_bf)
            y_t = jnp.where(lane == b, y_all, y_t)
        yh_ref[h] = y_t.T

    @pl.when(hb == nhb - 1)
    def _():
        for h in range(SSD_HEADS):
            mix_ref[:, h * SSD_HEAD_DIM:(h + 1) * SSD_HEAD_DIM] = yh_ref[h]
        y = mix_ref[:, :SSD_WIDTH] + dskx_ref[...] * xs_ref[...]
        y = y * _silu(proj_ref[:, COL_Z:COL_Z + SSD_WIDTH])
        mix_ref[:, :SSD_WIDTH] = _rms(y) * snw_ref[...]

        cosf = cos_ref[...]
        sinf = sin_ref[...]
        for h in range(RET_HEADS):
            lg = LOG_GAMMA[h]
            qh = proj_ref[:, COL_Q + h * RET_HEAD_DIM:COL_Q + (h + 1) * RET_HEAD_DIM]
            kh = proj_ref[:, COL_K + h * RET_HEAD_DIM:COL_K + (h + 1) * RET_HEAD_DIM]
            vh = proj_ref[:, COL_V + h * RET_HEAD_DIM:COL_V + (h + 1) * RET_HEAD_DIM]
            gh = proj_ref[:, COL_G + h * RET_HEAD_DIM:COL_G + (h + 1) * RET_HEAD_DIM]
            qh = qh * cosf + pltpu.roll(qh, RET_HEAD_DIM // 2, 1) * sinf
            kh = (kh * cosf + pltpu.roll(kh, RET_HEAD_DIM // 2, 1) * sinf) * (RET_HEAD_DIM ** -0.5)
            qt = qh.T
            kt = kh.T
            hs = slice(SSD_WIDTH + h * RET_HEAD_DIM, SSD_WIDTH + (h + 1) * RET_HEAD_DIM)
            for b in range(bt):
                s_new = ret_in_ref[b, h] * float(np.exp(lg)) + kt[:, b:b + 1] * vh[b:b + 1, :]
                ret_ref[b, h] = s_new
                mix_ref[b:b + 1, hs] = jnp.sum(qt[:, b:b + 1] * s_new, axis=0, keepdims=True)
            mix_ref[:, hs] = _rms(mix_ref[:, hs]) * _silu(gh)

        xc = conv(proj_ref[:, COL_XR:COL_XR + LRU_WIDTH], lbuf_in_ref, lcw_ref, lcb_ref, lbuf_ref)
        xc_bf = xc.astype(BF16)
        r_gate = jax.nn.sigmoid(_dot(xc_bf, wa_ref[...]) + ba_ref[...])
        i_gate = jax.nn.sigmoid(_dot(xc_bf, wi_ref[...]) + bi_ref[...])
        log_a = -LRU_C * r_gate * _softplus(-lam_ref[...])
        h_new = (jnp.exp(log_a) * lruh_in_ref[...]
                 + jnp.sqrt(_neg_expm1(2.0 * log_a)) * (i_gate * xc))
        lruh_ref[...] = h_new
        mix_ref[:, SSD_WIDTH + RET_WIDTH:] = h_new * _gelu(proj_ref[:, COL_GATE:COL_GATE + LRU_WIDTH])

        xo_ref[...] = x_ref[...] + _dot(mix_ref[...].astype(BF16), wout_ref[...])


def _sample_mixer(proj, x_all, x_mixed, cosf, sinf, states, new_states, wts, layer, n_prompt, n_sample):
    bt = SAMPLE_TILE
    hbk = SAMPLE_HEAD_BLOCK
    base = n_prompt // bt
    t = x_all.shape[0]
    row_spec = lambda width: pl.BlockSpec((bt, width), lambda i, j: (base + i, 0))
    ssd_spec = pl.BlockSpec((None, bt, hbk, SSD_HEAD_DIM, SSD_STATE), lambda i, j: (layer, i, j, 0, 0))
    sbuf_spec = pl.BlockSpec((None, bt, CONV_WIDTH - 1, SSD_CONV_DIM), lambda i, j: (layer, i, 0, 0))
    ret_spec = pl.BlockSpec((None, bt, RET_HEADS, RET_HEAD_DIM, RET_HEAD_DIM), lambda i, j: (layer, i, 0, 0, 0))
    lruh_spec = pl.BlockSpec((None, bt, LRU_WIDTH), lambda i, j: (layer, i, 0))
    lbuf_spec = pl.BlockSpec((None, bt, CONV_WIDTH - 1, LRU_WIDTH), lambda i, j: (layer, i, 0, 0))
    state_specs = [ssd_spec, sbuf_spec, ret_spec, lruh_spec, lbuf_spec]
    aliased = (x_mixed,) + (tuple(new_states) if new_states is not None else ())
    in_specs = [row_spec(PROJ_WIDTH), row_spec(D_MODEL),
                _const_spec((1, LANES)), _const_spec((1, LANES))] + state_specs
    in_specs += [_const_spec(w.shape) for w in wts]
    n_in = len(in_specs)
    in_specs += [pl.BlockSpec(memory_space=pl.ANY)] * len(aliased)
    out_shapes = (jax.ShapeDtypeStruct((t, D_MODEL), F32),) + tuple(
        jax.ShapeDtypeStruct(s.shape, F32) for s in states)
    out_specs = (row_spec(D_MODEL),) + tuple(state_specs)
    return pl.pallas_call(
        functools.partial(_sample_mixer_kernel, n_aliased=len(aliased)),
        grid=(n_sample // bt, SSD_HEADS // hbk),
        in_specs=in_specs,
        out_specs=out_specs,
        out_shape=out_shapes,
        scratch_shapes=[pltpu.VMEM((SSD_HEADS, SSD_HEAD_DIM, bt), F32),
                        pltpu.VMEM((SSD_HEADS, SSD_HEAD_DIM, bt), F32),
                        pltpu.VMEM((SSD_GROUPS, bt, SSD_STATE), F32),
                        pltpu.VMEM((SSD_GROUPS, bt, SSD_STATE), F32),
                        pltpu.VMEM((SSD_HEADS, bt, SSD_HEAD_DIM), F32),
                        pltpu.VMEM((bt, SSD_WIDTH), F32),
                        pltpu.VMEM((bt, D_MIX), F32)],
        input_output_aliases={n_in + k: k for k in range(len(aliased))},
        compiler_params=_params(("arbitrary", "arbitrary")),
        name="sample_mixer",
    )(proj, x_all, cosf, sinf, *states, *wts, *aliased)


def _batcher_pairs(n):
    pairs = []

    def merge(lo, hi, r):
        step = r * 2
        if step < hi - lo:
            merge(lo, hi, step)
            merge(lo + r, hi, step)
            pairs.extend((i, i + r) for i in range(lo + r, hi - r, step))
        else:
            pairs.append((lo, lo + r))

    def sort(lo, hi):
        if hi - lo >= 1:
            mid = lo + (hi - lo) // 2
            sort(lo, mid)
            sort(mid + 1, hi)
            merge(lo, hi, 1)

    sort(0, n - 1)
    return pairs


_SORT16_PAIRS = _batcher_pairs(PEER_TOPK)


def _first(a, b):
    return (a[0] > b[0]) | ((a[0] == b[0]) & (a[1] < b[1]))


def _exchange(a, b):
    a_first = _first(a, b)
    hi = (jnp.maximum(a[0], b[0]),) + tuple(jnp.where(a_first, pa, pb) for pa, pb in zip(a[1:], b[1:]))
    lo = (jnp.minimum(a[0], b[0]),) + tuple(jnp.where(a_first, pb, pa) for pa, pb in zip(a[1:], b[1:]))
    return hi, lo


def _better(a, b):
    a_first = _first(a, b)
    return (jnp.maximum(a[0], b[0]),) + tuple(jnp.where(a_first, pa, pb) for pa, pb in zip(a[1:], b[1:]))


def _sort16(items):
    items = list(items)
    for i, j in _SORT16_PAIRS:
        items[i], items[j] = _exchange(items[i], items[j])
    return items


def _merge_top16(a, b):
    k = PEER_TOPK
    c = [_better(a[i], b[k - 1 - i]) for i in range(k)]
    d = k // 2
    while d >= 1:
        for i in range(k):
            if i & d == 0:
                c[i], c[i + d] = _exchange(c[i], c[i + d])
        d //= 2
    return c


def _top16_of_lists(vals, tags):
    k = PEER_TOPK
    best = None
    for g in range(0, len(vals), k):
        group = _sort16([(vals[i]() if callable(vals[i]) else vals[i], tags[i]) for i in range(g, g + k)])
        best = group if best is None else _merge_top16(best, group)
    return [b[0] for b in best], [b[1] for b in best]


ROUTE_TILE = SUBLANES * LANES
ROW_PITCH = PEER_KEYS + SUBLANES


def _peer_route_kernel(x_ref, n_ref, wq_ref, keys_ref,
                       xn_ref, ia_ref, ib_ref, gate_ref,
                       q_ref, z1_ref, z2_ref, za_ref, zb_ref, zg_ref):
    k = PEER_TOPK
    tile = (SUBLANES, LANES)
    xn = (_rms(x_ref[...]) * n_ref[...]).astype(BF16)
    xn_ref[...] = xn
    q = _dot(xn, wq_ref[...]).astype(BF16)
    for hc in range(2 * PEER_HEADS):
        q_ref[hc] = q[:, hc * PEER_HALF:(hc + 1) * PEER_HALF]

    def head(h, carry):
        for c, z_ref in ((0, z1_ref), (1, z2_ref)):
            keys = keys_ref[2 * h + c]
            for j in range(SUBLANES):
                z_ref[j * ROW_PITCH:j * ROW_PITCH + PEER_KEYS, :] = _dot_nt(
                    keys, q_ref[2 * h + c, j * LANES:(j + 1) * LANES, :])

        def top_keys(z_ref):
            load = lambda key: (lambda: z_ref[pl.ds(key, SUBLANES, stride=ROW_PITCH), :])
            return _top16_of_lists([load(key) for key in range(PEER_KEYS)],
                                   [jnp.full(tile, float(key), F32) for key in range(PEER_KEYS)])

        t1, i1 = top_keys(z1_ref)
        t2, i2 = top_keys(z2_ref)
        i1s = [v * float(PEER_KEYS) for v in i1]

        def pair(a, b):
            return (t1[a] + t2[b], jnp.full(tile, float(a * k + b), F32), i1s[a] + i2[b])

        pad = (jnp.full(tile, -jnp.inf, F32), jnp.full(tile, float(k * k), F32), jnp.zeros(tile, F32))
        row = lambda a: [pair(a, b) for b in range(k // (a + 1))]
        first = _merge_top16(row(0), row(1) + [pad] * (k - len(row(1))))
        second = _sort16(row(2) + row(3) + row(4) + row(5) + row(6))
        third_items = row(7) + [pair(a, 0) for a in range(8, k)]
        third = _sort16(third_items + [pad] * (k - len(third_items)))
        best = _merge_top16(_merge_top16(first, second), third)

        e = [jnp.exp(item[0] - best[0][0]) for item in best]
        z = e[0]
        for v in e[1:]:
            z = z + v
        inv = 1.0 / z
        for r in range(k):
            sel = h * k + r
            expert = best[r][2]
            a_idx = jnp.floor(expert * (1.0 / PEER_KEYS))
            za_ref[pl.ds(sel, SUBLANES, stride=ROW_PITCH), :] = a_idx
            zb_ref[pl.ds(sel, SUBLANES, stride=ROW_PITCH), :] = expert - a_idx * float(PEER_KEYS)
            zg_ref[pl.ds(sel, SUBLANES, stride=ROW_PITCH), :] = e[r] * inv
        return carry

    lax.fori_loop(0, PEER_HEADS, head, 0)
    for j in range(SUBLANES):
        rows = slice(j * LANES, (j + 1) * LANES)
        src = slice(j * ROW_PITCH, j * ROW_PITCH + PEER_HEADS * k)
        ia_ref[rows, :] = za_ref[src, :].T
        ib_ref[rows, :] = zb_ref[src, :].T
        gate_ref[rows, :] = zg_ref[src, :].T


def _peer_route(x, n2, wq, keys):
    t = x.shape[0]
    tm = ROUTE_TILE
    sel_spec = pl.BlockSpec((tm, PEER_HEADS * PEER_TOPK), lambda i: (i, 0))
    sel_shape = jax.ShapeDtypeStruct((t, PEER_HEADS * PEER_TOPK), F32)
    z_scratch = pltpu.VMEM((SUBLANES * ROW_PITCH, LANES), F32)
    return pl.pallas_call(
        _peer_route_kernel,
        grid=(pl.cdiv(t, tm),),
        in_specs=[pl.BlockSpec((tm, D_MODEL), lambda i: (i, 0)),
                  _const_spec((1, D_MODEL)),
                  _const_spec(wq.shape),
                  _const_spec(keys.shape)],
        out_specs=(pl.BlockSpec((tm, D_MODEL), lambda i: (i, 0)), sel_spec, sel_spec, sel_spec),
        out_shape=(jax.ShapeDtypeStruct((t, D_MODEL), BF16), sel_shape, sel_shape, sel_shape),
        scratch_shapes=[pltpu.VMEM((2 * PEER_HEADS, tm, PEER_HALF), BF16),
                        z_scratch, z_scratch, z_scratch, z_scratch, z_scratch],
        compiler_params=_params(("arbitrary",)),
        name="peer_route",
    )(x, n2, wq, keys)


def _peer_mlp_kernel(xn_ref, u0_ref, u_ref, v_ref, ia_ref, ib_ref, gate_ref, x_ref, fw_ref, *rest,
                     final, n_prompt_tiles, n_sample):
    if final:
        o_ref, os_ref, acc_ref, act_a_ref, act_b_ref, w_ref, g_ref = rest
    else:
        o_ref, acc_ref, act_a_ref, act_b_ref, w_ref, g_ref = rest
    e = pl.program_id(1)
    tm = xn_ref.shape[0]
    rows_per_tile = EXPERT_TILE // PEER_KEYS

    @pl.when(e == 0)
    def _():
        acc_ref[...] = jnp.zeros_like(acc_ref)
        key = lax.broadcasted_iota(jnp.int32, (PEER_KEYS, PEER_HEADS * PEER_TOPK), 0).astype(F32)

        def build(t, carry):
            gate = gate_ref[pl.ds(t, 1), :]
            sel_gate = jnp.where(key == ia_ref[pl.ds(t, 1), :], gate, 0.0)
            hi = sel_gate.astype(BF16)
            lo = (sel_gate - hi.astype(F32)).astype(BF16)
            onehot = jnp.where(key == ib_ref[pl.ds(t, 1), :], 1.0, 0.0).astype(BF16)
            g_ref[pl.ds(pl.multiple_of(t * ROW_PITCH, SUBLANES), PEER_KEYS), :] = _dot_nt(
                jnp.concatenate([hi, lo], axis=1), jnp.concatenate([onehot, onehot], axis=1))
            return carry

        lax.fori_loop(0, tm, build, 0, unroll=BUILD_UNROLL)
        act_a_ref[...] = _dot_nt(xn_ref[...], u0_ref[...])

    def step(act_ref, act_next_ref):
        act_next_ref[...] = _dot_nt(xn_ref[...], u_ref[...])
        for j in range(rows_per_tile):
            i1 = e * rows_per_tile + j
            cols = slice(j * PEER_KEYS, (j + 1) * PEER_KEYS)
            gate = g_ref[pl.ds(i1, tm, stride=ROW_PITCH), :]
            w_ref[:, cols] = (gate * _gelu(act_ref[:, cols])).astype(BF16)
        acc_ref[...] += _dot(w_ref[...], v_ref[...])

    @pl.when(e % 2 == 0)
    def _():
        step(act_a_ref, act_b_ref)

    @pl.when(e % 2 == 1)
    def _():
        step(act_b_ref, act_a_ref)

    @pl.when(e == pl.num_programs(1) - 1)
    def _():
        y = x_ref[...] + acc_ref[...]
        if not final:
            o_ref[...] = y
            return
        y = _rms(y) * fw_ref[...]
        tile = pl.program_id(0)

        @pl.when(tile < n_prompt_tiles)
        def _():
            o_ref[...] = y

        @pl.when(tile == pl.num_programs(0) - 1)
        def _():
            os_ref[...] = y[tm - n_sample:, :]


def _peer_mlp(xn, u_bf, v_bf, route, x, fw, tm, layer, final, n_prompt):
    t = x.shape[0]
    n_sample = t - n_prompt
    assert n_sample <= tm and n_sample % SUBLANES == 0
    n_prompt_tiles = pl.cdiv(n_prompt, tm)
    n_tiles = PEER_EXPERTS // EXPERT_TILE
    assert n_tiles % 2 == 0
    ia, ib, gate = route
    tok_spec = lambda width: pl.BlockSpec((tm, width), lambda i, e: (i, 0))
    if final:
        out_specs = (pl.BlockSpec((tm, D_MODEL), lambda i, e: (jnp.minimum(i, n_prompt_tiles - 1), 0)),
                     pl.BlockSpec((n_sample, D_MODEL), lambda i, e: (0, 0)))
        out_shape = (jax.ShapeDtypeStruct((n_prompt, D_MODEL), F32),
                     jax.ShapeDtypeStruct((n_sample, D_MODEL), F32))
    else:
        out_specs = tok_spec(D_MODEL)
        out_shape = jax.ShapeDtypeStruct((t, D_MODEL), F32)
    return pl.pallas_call(
        functools.partial(_peer_mlp_kernel, final=final, n_prompt_tiles=n_prompt_tiles, n_sample=n_sample),
        grid=(t // tm, n_tiles),
        in_specs=[tok_spec(D_MODEL),
                  pl.BlockSpec((None, EXPERT_TILE, D_MODEL), lambda i, e: (layer, 0, 0),
                               pipeline_mode=pl.Buffered(1)),
                  pl.BlockSpec((None, EXPERT_TILE, D_MODEL), lambda i, e: (layer, jnp.minimum(e + 1, n_tiles - 1), 0)),
                  pl.BlockSpec((None, EXPERT_TILE, D_MODEL), lambda i, e: (layer, e, 0)),
                  tok_spec(PEER_HEADS * PEER_TOPK), tok_spec(PEER_HEADS * PEER_TOPK),
                  tok_spec(PEER_HEADS * PEER_TOPK),
                  tok_spec(D_MODEL),
                  _const_spec((1, D_MODEL))],
        out_specs=out_specs,
        out_shape=out_shape,
        scratch_shapes=[pltpu.VMEM((tm, D_MODEL), F32),
                        pltpu.VMEM((tm, EXPERT_TILE), F32),
                        pltpu.VMEM((tm, EXPERT_TILE), F32),
                        pltpu.VMEM((tm, EXPERT_TILE), BF16),
                        pltpu.VMEM((tm * ROW_PITCH, PEER_KEYS), F32)],
        compiler_params=_params(("arbitrary", "arbitrary")),
        name="peer_mlp",
    )(xn, u_bf, u_bf, v_bf, ia, ib, gate, x, fw)


def _token_tile(t):
    for tm in (512, 384, 256, 128):
        if t % tm == 0:
            return tm
    raise ValueError(f"token count {t} must be a multiple of {LANES}")


def _rope_tables(pos):
    half = RET_HEAD_DIM // 2
    inv = ROPE_BASE ** (-jnp.arange(half, dtype=F32) / half)
    ang = pos.astype(F32)[:, None] * inv[None, :]
    cos = jnp.cos(ang)
    sin = jnp.sin(ang)
    return jnp.concatenate([cos, cos], axis=1), jnp.concatenate([-sin, sin], axis=1)


def _block_diag(w):
    eye = jnp.eye(LRU_BLOCKS, dtype=w.dtype)
    return jnp.einsum('hij,hg->higj', w, eye).reshape(LRU_WIDTH, LRU_WIDTH)


def kernel(x_prompt, x_sample, state_ssd, state_ssd_conv, state_ret, state_lru, state_lru_conv, norm1_w, w_in, ssd_conv_w, ssd_conv_b, ssd_dt_bias, ssd_a_log, ssd_d, ssd_norm_w, lru_conv_w, lru_conv_b, lru_wa, lru_ba, lru_wi, lru_bi, lru_lambda, w_out, norm2_w, peer_wq, peer_keys, peer_u, peer_v, final_norm_w):
    batch, seq, _ = x_prompt.shape
    n_sample, dec_seq, _ = x_sample.shape
    depth = w_in.shape[0]
    assert dec_seq == 1 and seq % SCAN_CHUNK == 0 and n_sample % SAMPLE_TILE == 0
    n_prompt = batch * seq
    t = n_prompt + n_sample
    tm = _token_tile(t)

    x = jnp.concatenate([x_prompt.reshape(n_prompt, D_MODEL), x_sample.reshape(n_sample, D_MODEL)], axis=0)
    cos_p, sin_p = _rope_tables(jnp.arange(seq))
    cos_s, sin_s = _rope_tables(PAST_LEN + jnp.arange(dec_seq))
    row = lambda a: a.reshape(1, -1).astype(F32)
    per_channel = lambda a: jnp.repeat(a.astype(F32), SSD_HEAD_DIM).reshape(1, SSD_WIDTH)
    per_head = lambda a: jnp.pad(a.astype(F32), (0, LANES - SSD_HEADS)).reshape(1, LANES)
    expand = (jnp.arange(LANES)[:, None] == jnp.arange(SSD_WIDTH)[None, :] // SSD_HEAD_DIM).astype(F32)
    off = SSD_WIDTH + SSD_CONV_DIM

    carried = (jnp.swapaxes(state_ssd, -1, -2), state_ssd_conv, state_ret, state_lru, state_lru_conv)
    u_bf = peer_u.astype(BF16)
    v_bf = peer_v.astype(BF16)
    new_p, st_s = [], None
    for i in range(depth):
        wi_ = w_in[i]
        dt_cols = wi_[:, off:off + SSD_HEADS]
        w_in_p = jnp.concatenate(
            [wi_[:, :off],
             jnp.pad(dt_cols, ((0, 0), (0, LANES - SSD_HEADS))),
             wi_[:, off + SSD_HEADS:]], axis=1).astype(BF16)
        wts = (ssd_conv_w[i], row(ssd_conv_b[i]), per_head(ssd_dt_bias[i]), per_head(ssd_a_log[i]),
               expand, per_channel(ssd_d[i]),
               row(ssd_norm_w[i]), lru_conv_w[i], row(lru_conv_b[i]),
               _block_diag(lru_wa[i]).astype(BF16), row(lru_ba[i]),
               _block_diag(lru_wi[i]).astype(BF16), row(lru_bi[i]), row(lru_lambda[i]),
               w_out[i].astype(BF16))

        proj = _inproj(x, row(norm1_w[i]), w_in_p, tm)
        x_mixed, *st_p = _prompt_mixer(proj, x, cos_p, sin_p, wts, batch, seq)
        st_p[3] = st_p[3].reshape(batch, LRU_WIDTH)
        x, *st_s = _sample_mixer(proj, x, x_mixed, cos_s, sin_s, carried, st_s, wts, i, n_prompt, n_sample)
        new_p.append(st_p)

        keys_bf = peer_keys[i].reshape(PEER_HEADS * 2, PEER_KEYS, PEER_HALF).astype(BF16)
        xn, *route = _peer_route(x, row(norm2_w[i]), peer_wq[i].astype(BF16), keys_bf)
        x = _peer_mlp(xn, u_bf, v_bf, route, x, row(final_norm_w), tm, i, i == depth - 1, n_prompt)

    y_prompt, y_sample = x
    stk = lambda states, j: jnp.stack([s[j] for s in states])
    return (y_prompt.reshape(batch, seq, D_MODEL), y_sample.reshape(n_sample, dec_seq, D_MODEL),
            stk(new_p, 0), stk(new_p, 1), stk(new_p, 2), stk(new_p, 3), stk(new_p, 4),
            jnp.swapaxes(st_s[0], -1, -2), *st_s[1:])
```

```python
import functools

import numpy as np
import jax
import jax.numpy as jnp
from jax import lax
from jax.experimental import pallas as pl
from jax.experimental.pallas import tpu as pltpu

F32 = jnp.float32
BF16 = jnp.bfloat16

D_MODEL = 1024
PAST_LEN = 16384
D_MIX = 2 * D_MODEL
CONV_WIDTH = 4
NORM_EPS = 1e-6
SSD_WIDTH = D_MIX // 2
SSD_HEAD_DIM = 64
SSD_HEADS = SSD_WIDTH // SSD_HEAD_DIM
SSD_GROUPS = 2
SSD_HPG = SSD_HEADS // SSD_GROUPS
SSD_STATE = 128
SSD_CONV_DIM = SSD_WIDTH + 2 * SSD_GROUPS * SSD_STATE
RET_WIDTH = D_MIX // 4
RET_HEADS = 4
RET_HEAD_DIM = RET_WIDTH // RET_HEADS
ROPE_BASE = 10000.0
LRU_WIDTH = D_MIX // 4
LRU_BLOCKS = 8
LRU_BLOCK_DIM = LRU_WIDTH // LRU_BLOCKS
LRU_C = 8.0
PEER_KEYS = 128
PEER_EXPERTS = PEER_KEYS * PEER_KEYS
PEER_HEADS = 8
PEER_TOPK = 16
PEER_QUERY_DIM = 256
PEER_HALF = PEER_QUERY_DIM // 2

LANES = 128
SUBLANES = 8
VMEM_LIMIT = 56 * 1024 * 1024

COL_Z = 0
COL_XBC = COL_Z + SSD_WIDTH
COL_DT16 = COL_XBC + SSD_CONV_DIM
COL_Q = COL_DT16 + LANES
COL_K = COL_Q + RET_WIDTH
COL_V = COL_K + RET_WIDTH
COL_G = COL_V + RET_WIDTH
COL_GATE = COL_G + RET_WIDTH
COL_XR = COL_GATE + LRU_WIDTH
PROJ_WIDTH = COL_XR + LRU_WIDTH

SCAN_CHUNK = 128
SAMPLE_TILE = SUBLANES
SAMPLE_HEAD_BLOCK = 4
EXPERT_TILE = 1024
BUILD_UNROLL = 64
LOG_GAMMA = [float(np.log(1.0 - 2.0 ** (-5.0 - h))) for h in range(RET_HEADS)]

_NT = (((1,), (1,)), ((), ()))


def _rms(x):
    return x * lax.rsqrt(jnp.mean(x * x, axis=-1, keepdims=True) + NORM_EPS)


def _softplus(x):
    return jnp.maximum(x, 0.0) + jnp.log1p(jnp.exp(-jnp.abs(x)))


def _silu(x):
    return x * jax.nn.sigmoid(x)


def _gelu(x):
    return 0.5 * x * (1.0 + lax.erf(x * np.float32(np.sqrt(0.5))))


def _neg_expm1(x):
    return -jnp.tanh(0.5 * x) * (jnp.exp(x) + 1.0)


def _dot(a, b):
    return jnp.dot(a, b, preferred_element_type=F32)


def _dot_nt(a, b):
    return lax.dot_general(a, b, _NT, preferred_element_type=F32)


def _dot_exact(a, b):
    return jnp.dot(a, b, preferred_element_type=F32, precision=lax.Precision.HIGHEST)


def _params(semantics):
    return pltpu.CompilerParams(dimension_semantics=semantics, vmem_limit_bytes=VMEM_LIMIT)


def _const_spec(shape):
    zeros = (0,) * len(shape)
    return pl.BlockSpec(shape, lambda *_: zeros)


def _inproj_kernel(x_ref, n_ref, w_ref, o_ref):
    h = (_rms(x_ref[...]) * n_ref[...]).astype(BF16)
    step = 8 * LANES
    for j in range(0, PROJ_WIDTH, step):
        hi = min(j + step, PROJ_WIDTH)
        o_ref[:, j:hi] = _dot(h, w_ref[:, j:hi])


def _inproj(x, n1, w_in_p, tm):
    t = x.shape[0]
    return pl.pallas_call(
        _inproj_kernel,
        grid=(t // tm,),
        in_specs=[pl.BlockSpec((tm, D_MODEL), lambda i: (i, 0)),
                  _const_spec((1, D_MODEL)),
                  pl.BlockSpec((D_MODEL, PROJ_WIDTH), lambda i: (0, 0), pipeline_mode=pl.Buffered(1))],
        out_specs=pl.BlockSpec((tm, PROJ_WIDTH), lambda i: (i, 0)),
        out_shape=jax.ShapeDtypeStruct((t, PROJ_WIDTH), F32),
        compiler_params=_params(("arbitrary",)),
        name="inproj",
    )(x, n1, w_in_p)


def _prompt_mixer_kernel(proj_ref, x_ref, cos_ref, sin_ref, scw_ref, scb_ref, dtb16_ref, alog16_ref,
                         expand_ref, dskx_ref, snw_ref, lcw_ref, lcb_ref, wa_ref, ba_ref,
                         wi_ref, bi_ref, lam_ref, wout_ref,
                         xo_ref, ssd_ref, sbuf_ref, ret_ref, lruh_ref, lbuf_ref,
                         xps_ref, xpl_ref, mix_ref):
    c = SCAN_CHUNK
    step = pl.program_id(1)
    pad = SUBLANES
    keep = CONV_WIDTH - 1

    @pl.when(step == 0)
    def _():
        ssd_ref[...] = jnp.zeros_like(ssd_ref)
        ret_ref[...] = jnp.zeros_like(ret_ref)
        lruh_ref[...] = jnp.zeros_like(lruh_ref)
        xps_ref[0:pad, :] = jnp.zeros((pad, SSD_CONV_DIM), F32)
        xpl_ref[0:pad, :] = jnp.zeros((pad, LRU_WIDTH), F32)

    def conv(xp_ref, x_new, w_ref, b_ref, buf_out_ref):
        xp_ref[pad:pad + c, :] = x_new
        y = b_ref[...] + xp_ref[pad - keep:pad - keep + c, :] * w_ref[0:1, :]
        for tap in range(1, CONV_WIDTH):
            y = y + xp_ref[pad - keep + tap:pad - keep + tap + c, :] * w_ref[tap:tap + 1, :]
        tail = xp_ref[pad + c - keep:pad + c, :]
        xp_ref[pad - keep:pad, :] = tail
        buf_out_ref[0] = tail
        return y

    row = lax.broadcasted_iota(jnp.int32, (c, c), 0)
    col = lax.broadcasted_iota(jnp.int32, (c, c), 1)
    causal = row >= col
    tri = causal.astype(F32)
    rowf = lax.broadcasted_iota(jnp.int32, (c, LANES), 0).astype(F32)

    xbc = _silu(conv(xps_ref, proj_ref[:, COL_XBC:COL_XBC + SSD_CONV_DIM], scw_ref, scb_ref, sbuf_ref))
    xs = xbc[:, :SSD_WIDTH]
    bm = xbc[:, SSD_WIDTH:SSD_WIDTH + SSD_GROUPS * SSD_STATE]
    cm = xbc[:, SSD_WIDTH + SSD_GROUPS * SSD_STATE:]
    dt16 = _softplus(proj_ref[:, COL_DT16:COL_DT16 + LANES] + dtb16_ref[...])
    cum16 = _dot_exact(tri, -dt16 * jnp.exp(alog16_ref[...]))
    cum16_t = cum16.T
    wide = _dot_exact(jnp.concatenate([dt16, cum16], axis=0), expand_ref[...])
    dtx = wide[:c]
    cumx = wide[c:]
    v = xs * dtx
    v_bf = v.astype(BF16)
    cum_last = cumx[c - 1:c, :]
    vte_bf = (v * jnp.exp(cum_last - cumx)).astype(BF16)
    ecum = jnp.exp(cumx)
    chunk_decay = jnp.exp(cum_last)
    for g in range(SSD_GROUPS):
        bmg = bm[:, g * SSD_STATE:(g + 1) * SSD_STATE]
        cmg_bf = cm[:, g * SSD_STATE:(g + 1) * SSD_STATE].astype(BF16)
        scores = _dot_nt(cmg_bf, bmg.astype(BF16))
        bmg_t_bf = bmg.T.astype(BF16)
        for r in range(SSD_HPG):
            h = g * SSD_HPG + r
            hs = slice(h * SSD_HEAD_DIM, (h + 1) * SSD_HEAD_DIM)
            seg = cum16[:, h:h + 1] - cum16_t[h:h + 1, :]
            decay = jnp.exp(jnp.where(causal, seg, -jnp.inf))
            y_intra = _dot((scores * decay).astype(BF16), v_bf[:, hs])
            s_in = ssd_ref[0, h]
            y_inter = _dot(cmg_bf, s_in.astype(BF16)) * ecum[:, hs]
            ssd_ref[0, h] = s_in * chunk_decay[:, hs] + _dot(bmg_t_bf, vte_bf[:, hs])
            mix_ref[:, hs] = y_intra + y_inter
    y = mix_ref[:, :SSD_WIDTH] + dskx_ref[...] * xs
    y = y * _silu(proj_ref[:, COL_Z:COL_Z + SSD_WIDTH])
    mix_ref[:, :SSD_WIDTH] = _rms(y) * snw_ref[...]

    cosf = cos_ref[...]
    sinf = sin_ref[...]
    rowcol = (row - col).astype(F32)
    for h in range(RET_HEADS):
        lg = LOG_GAMMA[h]
        hs = slice(h * RET_HEAD_DIM, (h + 1) * RET_HEAD_DIM)
        qh = proj_ref[:, COL_Q + h * RET_HEAD_DIM:COL_Q + (h + 1) * RET_HEAD_DIM]
        kh = proj_ref[:, COL_K + h * RET_HEAD_DIM:COL_K + (h + 1) * RET_HEAD_DIM]
        vh = proj_ref[:, COL_V + h * RET_HEAD_DIM:COL_V + (h + 1) * RET_HEAD_DIM]
        gh = proj_ref[:, COL_G + h * RET_HEAD_DIM:COL_G + (h + 1) * RET_HEAD_DIM]
        qh = qh * cosf + pltpu.roll(qh, RET_HEAD_DIM // 2, 1) * sinf
        kh = (kh * cosf + pltpu.roll(kh, RET_HEAD_DIM // 2, 1) * sinf) * (RET_HEAD_DIM ** -0.5)
        qh_bf = qh.astype(BF16)
        scores = _dot_nt(qh_bf, kh.astype(BF16))
        decay = jnp.exp(jnp.where(causal, rowcol * lg, -jnp.inf))
        y_intra = _dot((scores * decay).astype(BF16), vh.astype(BF16))
        s_in = ret_ref[0, h]
        y_inter = _dot(qh_bf, s_in.astype(BF16)) * jnp.exp((rowf + 1.0) * lg)
        v_to_end = (vh * jnp.exp((c - 1.0 - rowf) * lg)).astype(BF16)
        ret_ref[0, h] = s_in * float(np.exp(c * lg)) + _dot(kh.T.astype(BF16), v_to_end)
        mix_ref[:, SSD_WIDTH + h * RET_HEAD_DIM:SSD_WIDTH + (h + 1) * RET_HEAD_DIM] = (
            _rms(y_intra + y_inter) * _silu(gh))

    xc = conv(xpl_ref, proj_ref[:, COL_XR:COL_XR + LRU_WIDTH], lcw_ref, lcb_ref, lbuf_ref)
    xc_bf = xc.astype(BF16)
    r_gate = jax.nn.sigmoid(_dot(xc_bf, wa_ref[...]) + ba_ref[...])
    i_gate = jax.nn.sigmoid(_dot(xc_bf, wi_ref[...]) + bi_ref[...])
    log_a = -LRU_C * r_gate * _softplus(-lam_ref[...])
    a_run = jnp.exp(log_a)
    b_run = jnp.sqrt(_neg_expm1(2.0 * log_a)) * (i_gate * xc)
    rowl = lax.broadcasted_iota(jnp.int32, (c, LRU_WIDTH), 0)
    shift = 1
    while shift < c:
        valid = rowl >= shift
        a_prev = pltpu.roll(a_run, shift, 0)
        b_prev = pltpu.roll(b_run, shift, 0)
        b_run = jnp.where(valid, a_run * b_prev + b_run, b_run)
        a_run = jnp.where(valid, a_run * a_prev, a_run)
        shift *= 2
    h_all = b_run + a_run * lruh_ref[0]
    lruh_ref[0] = h_all[c - 1:c, :]
    mix_ref[:, SSD_WIDTH + RET_WIDTH:] = h_all * _gelu(proj_ref[:, COL_GATE:COL_GATE + LRU_WIDTH])

    xo_ref[...] = x_ref[...] + _dot(mix_ref[...].astype(BF16), wout_ref[...])


def _prompt_mixer(proj, x_all, cosf, sinf, wts, batch, seq):
    c = SCAN_CHUNK
    nc = seq // c
    t = x_all.shape[0]
    row_spec = lambda width: pl.BlockSpec((c, width), lambda b, s: (b * nc + s, 0))
    state_spec = lambda shape: pl.BlockSpec((1,) + shape, lambda b, s: (b,) + (0,) * len(shape))
    in_specs = [row_spec(PROJ_WIDTH), row_spec(D_MODEL),
                pl.BlockSpec((c, LANES), lambda b, s: (s, 0)),
                pl.BlockSpec((c, LANES), lambda b, s: (s, 0))]
    in_specs += [_const_spec(w.shape) for w in wts]
    out_shapes = (jax.ShapeDtypeStruct((t, D_MODEL), F32),
                  jax.ShapeDtypeStruct((batch, SSD_HEADS, SSD_STATE, SSD_HEAD_DIM), F32),
                  jax.ShapeDtypeStruct((batch, CONV_WIDTH - 1, SSD_CONV_DIM), F32),
                  jax.ShapeDtypeStruct((batch, RET_HEADS, RET_HEAD_DIM, RET_HEAD_DIM), F32),
                  jax.ShapeDtypeStruct((batch, 1, LRU_WIDTH), F32),
                  jax.ShapeDtypeStruct((batch, CONV_WIDTH - 1, LRU_WIDTH), F32))
    out_specs = (row_spec(D_MODEL),
                 state_spec((SSD_HEADS, SSD_STATE, SSD_HEAD_DIM)),
                 state_spec((CONV_WIDTH - 1, SSD_CONV_DIM)),
                 state_spec((RET_HEADS, RET_HEAD_DIM, RET_HEAD_DIM)),
                 state_spec((1, LRU_WIDTH)),
                 state_spec((CONV_WIDTH - 1, LRU_WIDTH)))
    return pl.pallas_call(
        _prompt_mixer_kernel,
        grid=(batch, nc),
        in_specs=in_specs,
        out_specs=out_specs,
        out_shape=out_shapes,
        scratch_shapes=[pltpu.VMEM((SUBLANES + c, SSD_CONV_DIM), F32),
                        pltpu.VMEM((SUBLANES + c, LRU_WIDTH), F32),
                        pltpu.VMEM((c, D_MIX), F32)],
        compiler_params=_params(("arbitrary", "arbitrary")),
        name="prompt_mixer",
    )(proj, x_all, cosf, sinf, *wts)


def _sample_mixer_kernel(proj_ref, x_ref, cos_ref, sin_ref, ssd_in_ref, sbuf_in_ref, ret_in_ref,
                         lruh_in_ref, lbuf_in_ref, scw_ref, scb_ref, dtb16_ref, alog16_ref,
                         expand_ref, dskx_ref, snw_ref, lcw_ref, lcb_ref, wa_ref, ba_ref,
                         wi_ref, bi_ref, lam_ref, wout_ref, *rest, n_aliased):
    (xo_ref, ssd_ref, sbuf_ref, ret_ref, lruh_ref, lbuf_ref,
     vh_ref, ah_ref, bm_ref, cm_ref, yh_ref, xs_ref, mix_ref) = rest[n_aliased:]
    bt = SAMPLE_TILE
    hb = pl.program_id(1)
    nhb = pl.num_programs(1)
    keep = CONV_WIDTH - 1

    def conv(x_new, buf_in_ref, w_ref, b_ref, buf_out_ref):
        y = b_ref[...] + x_new * w_ref[keep:keep + 1, :]
        for tap in range(keep):
            y = y + buf_in_ref[:, tap, :] * w_ref[tap:tap + 1, :]
        for tap in range(keep - 1):
            buf_out_ref[:, tap, :] = buf_in_ref[:, tap + 1, :]
        buf_out_ref[:, keep - 1, :] = x_new
        return y

    @pl.when(hb == 0)
    def _():
        xbc = _silu(conv(proj_ref[:, COL_XBC:COL_XBC + SSD_CONV_DIM], sbuf_in_ref, scw_ref, scb_ref, sbuf_ref))
        xs = xbc[:, :SSD_WIDTH]
        xs_ref[...] = xs
        dt16 = _softplus(proj_ref[:, COL_DT16:COL_DT16 + LANES] + dtb16_ref[...])
        a16 = jnp.exp(-dt16 * jnp.exp(alog16_ref[...]))
        wide = _dot_exact(jnp.concatenate([dt16, a16], axis=0), expand_ref[...])
        dtx = wide[:bt]
        a = wide[bt:]
        v = xs * dtx
        for h in range(SSD_HEADS):
            hs = slice(h * SSD_HEAD_DIM, (h + 1) * SSD_HEAD_DIM)
            vh_ref[h] = v[:, hs].T
            ah_ref[h] = a[:, hs].T
        for g in range(SSD_GROUPS):
            bm_ref[g] = xbc[:, SSD_WIDTH + g * SSD_STATE:SSD_WIDTH + (g + 1) * SSD_STATE]
            off = SSD_WIDTH + SSD_GROUPS * SSD_STATE
            cm_ref[g] = xbc[:, off + g * SSD_STATE:off + (g + 1) * SSD_STATE]

    g = (hb * SAMPLE_HEAD_BLOCK) // SSD_HPG
    bm = bm_ref[g]
    cm = cm_ref[g]
    cm_bf = cm.astype(BF16)
    lane = lax.broadcasted_iota(jnp.int32, (SSD_HEAD_DIM, bt), 1)
    for r in range(SAMPLE_HEAD_BLOCK):
        h = hb * SAMPLE_HEAD_BLOCK + r
        vh_t = vh_ref[h]
        ah_t = ah_ref[h]
        y_t = jnp.zeros((SSD_HEAD_DIM, bt), F32)
        for b in range(bt):
            s_new = ssd_in_ref[b, r] * ah_t[:, b:b + 1] + vh_t[:, b:b + 1] * bm[b:b + 1, :]
            ssd_ref[b, r] = s_new
            y_all = _dot_nt(s_new.astype(BF16), cm_bf)
            y_t = jnp.where(lane == b, y_all, y_t)
        yh_ref[h] = y_t.T

    @pl.when(hb == nhb - 1)
    def _():
        for h in range(SSD_HEADS):
            mix_ref[:, h * SSD_HEAD_DIM:(h + 1) * SSD_HEAD_DIM] = yh_ref[h]
        y = mix_ref[:, :SSD_WIDTH] + dskx_ref[...] * xs_ref[...]
        y = y * _silu(proj_ref[:, COL_Z:COL_Z + SSD_WIDTH])
        mix_ref[:, :SSD_WIDTH] = _rms(y) * snw_ref[...]

        cosf = cos_ref[...]
        sinf = sin_ref[...]
        for h in range(RET_HEADS):
            lg = LOG_GAMMA[h]
            qh = proj_ref[:, COL_Q + h * RET_HEAD_DIM:COL_Q + (h + 1) * RET_HEAD_DIM]
            kh = proj_ref[:, COL_K + h * RET_HEAD_DIM:COL_K + (h + 1) * RET_HEAD_DIM]
            vh = proj_ref[:, COL_V + h * RET_HEAD_DIM:COL_V + (h + 1) * RET_HEAD_DIM]
            gh = proj_ref[:, COL_G + h * RET_HEAD_DIM:COL_G + (h + 1) * RET_HEAD_DIM]
            qh = qh * cosf + pltpu.roll(qh, RET_HEAD_DIM // 2, 1) * sinf
            kh = (kh * cosf + pltpu.roll(kh, RET_HEAD_DIM // 2, 1) * sinf) * (RET_HEAD_DIM ** -0.5)
            qt = qh.T
            kt = kh.T
            hs = slice(SSD_WIDTH + h * RET_HEAD_DIM, SSD_WIDTH + (h + 1) * RET_HEAD_DIM)
            for b in range(bt):
                s_new = ret_in_ref[b, h] * float(np.exp(lg)) + kt[:, b:b + 1] * vh[b:b + 1, :]
                ret_ref[b, h] = s_new
                mix_ref[b:b + 1, hs] = jnp.sum(qt[:, b:b + 1] * s_new, axis=0, keepdims=True)
            mix_ref[:, hs] = _rms(mix_ref[:, hs]) * _silu(gh)

        xc = conv(proj_ref[:, COL_XR:COL_XR + LRU_WIDTH], lbuf_in_ref, lcw_ref, lcb_ref, lbuf_ref)
        xc_bf = xc.astype(BF16)
        r_gate = jax.nn.sigmoid(_dot(xc_bf, wa_ref[...]) + ba_ref[...])
        i_gate = jax.nn.sigmoid(_dot(xc_bf, wi_ref[...]) + bi_ref[...])
        log_a = -LRU_C * r_gate * _softplus(-lam_ref[...])
        h_new = (jnp.exp(log_a) * lruh_in_ref[...]
                 + jnp.sqrt(_neg_expm1(2.0 * log_a)) * (i_gate * xc))
        lruh_ref[...] = h_new
        mix_ref[:, SSD_WIDTH + RET_WIDTH:] = h_new * _gelu(proj_ref[:, COL_GATE:COL_GATE + LRU_WIDTH])

        xo_ref[...] = x_ref[...] + _dot(mix_ref[...].astype(BF16), wout_ref[...])


def _sample_mixer(proj, x_all, x_mixed, cosf, sinf, states, new_states, wts, layer, n_prompt, n_sample):
    bt = SAMPLE_TILE
    hbk = SAMPLE_HEAD_BLOCK
    base = n_prompt // bt
    t = x_all.shape[0]
    row_spec = lambda width: pl.BlockSpec((bt, width), lambda i, j: (base + i, 0))
    ssd_spec = pl.BlockSpec((None, bt, hbk, SSD_HEAD_DIM, SSD_STATE), lambda i, j: (layer, i, j, 0, 0))
    sbuf_spec = pl.BlockSpec((None, bt, CONV_WIDTH - 1, SSD_CONV_DIM), lambda i, j: (layer, i, 0, 0))
    ret_spec = pl.BlockSpec((None, bt, RET_HEADS, RET_HEAD_DIM, RET_HEAD_DIM), lambda i, j: (layer, i, 0, 0, 0))
    lruh_spec = pl.BlockSpec((None, bt, LRU_WIDTH), lambda i, j: (layer, i, 0))
    lbuf_spec = pl.BlockSpec((None, bt, CONV_WIDTH - 1, LRU_WIDTH), lambda i, j: (layer, i, 0, 0))
    state_specs = [ssd_spec, sbuf_spec, ret_spec, lruh_spec, lbuf_spec]
    aliased = (x_mixed,) + (tuple(new_states) if new_states is not None else ())
    in_specs = [row_spec(PROJ_WIDTH), row_spec(D_MODEL),
                _const_spec((1, LANES)), _const_spec((1, LANES))] + state_specs
    in_specs += [_const_spec(w.shape) for w in wts]
    n_in = len(in_specs)
    in_specs += [pl.BlockSpec(memory_space=pl.ANY)] * len(aliased)
    out_shapes = (jax.ShapeDtypeStruct((t, D_MODEL), F32),) + tuple(
        jax.ShapeDtypeStruct(s.shape, F32) for s in states)
    out_specs = (row_spec(D_MODEL),) + tuple(state_specs)
    return pl.pallas_call(
        functools.partial(_sample_mixer_kernel, n_aliased=len(aliased)),
        grid=(n_sample // bt, SSD_HEADS // hbk),
        in_specs=in_specs,
        out_specs=out_specs,
        out_shape=out_shapes,
        scratch_shapes=[pltpu.VMEM((SSD_HEADS, SSD_HEAD_DIM, bt), F32),
                        pltpu.VMEM((SSD_HEADS, SSD_HEAD_DIM, bt), F32),
                        pltpu.VMEM((SSD_GROUPS, bt, SSD_STATE), F32),
                        pltpu.VMEM((SSD_GROUPS, bt, SSD_STATE), F32),
                        pltpu.VMEM((SSD_HEADS, bt, SSD_HEAD_DIM), F32),
                        pltpu.VMEM((bt, SSD_WIDTH), F32),
                        pltpu.VMEM((bt, D_MIX), F32)],
        input_output_aliases={n_in + k: k for k in range(len(aliased))},
        compiler_params=_params(("arbitrary", "arbitrary")),
        name="sample_mixer",
    )(proj, x_all, cosf, sinf, *states, *wts, *aliased)


def _batcher_pairs(n):
    pairs = []

    def merge(lo, hi, r):
        step = r * 2
        if step < hi - lo:
            merge(lo, hi, step)
            merge(lo + r, hi, step)
            pairs.extend((i, i + r) for i in range(lo + r, hi - r, step))
        else:
            pairs.append((lo, lo + r))

    def sort(lo, hi):
        if hi - lo >= 1:
            mid = lo + (hi - lo) // 2
            sort(lo, mid)
            sort(mid + 1, hi)
            merge(lo, hi, 1)

    sort(0, n - 1)
    return pairs


_SORT16_PAIRS = _batcher_pairs(PEER_TOPK)


def _first(a, b):
    return (a[0] > b[0]) | ((a[0] == b[0]) & (a[1] < b[1]))


def _exchange(a, b):
    a_first = _first(a, b)
    hi = (jnp.maximum(a[0], b[0]),) + tuple(jnp.where(a_first, pa, pb) for pa, pb in zip(a[1:], b[1:]))
    lo = (jnp.minimum(a[0], b[0]),) + tuple(jnp.where(a_first, pb, pa) for pa, pb in zip(a[1:], b[1:]))
    return hi, lo


def _better(a, b):
    a_first = _first(a, b)
    return (jnp.maximum(a[0], b[0]),) + tuple(jnp.where(a_first, pa, pb) for pa, pb in zip(a[1:], b[1:]))


def _sort16(items):
    items = list(items)
    for i, j in _SORT16_PAIRS:
        items[i], items[j] = _exchange(items[i], items[j])
    return items


def _merge_top16(a, b):
    k = PEER_TOPK
    c = [_better(a[i], b[k - 1 - i]) for i in range(k)]
    d = k // 2
    while d >= 1:
        for i in range(k):
            if i & d == 0:
                c[i], c[i + d] = _exchange(c[i], c[i + d])
        d //= 2
    return c


def _top16_of_lists(vals, tags):
    k = PEER_TOPK
    best = None
    for g in range(0, len(vals), k):
        group = _sort16([(vals[i]() if callable(vals[i]) else vals[i], tags[i]) for i in range(g, g + k)])
        best = group if best is None else _merge_top16(best, group)
    return [b[0] for b in best], [b[1] for b in best]


ROUTE_TILE = SUBLANES * LANES
ROW_PITCH = PEER_KEYS + SUBLANES


def _peer_route_kernel(x_ref, n_ref, wq_ref, keys_ref,
                       xn_ref, ia_ref, ib_ref, gate_ref,
                       q_ref, z1_ref, z2_ref, za_ref, zb_ref, zg_ref):
    k = PEER_TOPK
    tile = (SUBLANES, LANES)
    xn = (_rms(x_ref[...]) * n_ref[...]).astype(BF16)
    xn_ref[...] = xn
    q = _dot(xn, wq_ref[...]).astype(BF16)
    for hc in range(2 * PEER_HEADS):
        q_ref[hc] = q[:, hc * PEER_HALF:(hc + 1) * PEER_HALF]

    def head(h, carry):
        for c, z_ref in ((0, z1_ref), (1, z2_ref)):
            keys = keys_ref[2 * h + c]
            for j in range(SUBLANES):
                z_ref[j * ROW_PITCH:j * ROW_PITCH + PEER_KEYS, :] = _dot_nt(
                    keys, q_ref[2 * h + c, j * LANES:(j + 1) * LANES, :])

        def top_keys(z_ref):
            load = lambda key: (lambda: z_ref[pl.ds(key, SUBLANES, stride=ROW_PITCH), :])
            return _top16_of_lists([load(key) for key in range(PEER_KEYS)],
                                   [jnp.full(tile, float(key), F32) for key in range(PEER_KEYS)])

        t1, i1 = top_keys(z1_ref)
        t2, i2 = top_keys(z2_ref)
        i1s = [v * float(PEER_KEYS) for v in i1]

        def pair(a, b):
            return (t1[a] + t2[b], jnp.full(tile, float(a * k + b), F32), i1s[a] + i2[b])

        pad = (jnp.full(tile, -jnp.inf, F32), jnp.full(tile, float(k * k), F32), jnp.zeros(tile, F32))
        row = lambda a: [pair(a, b) for b in range(k // (a + 1))]
        first = _merge_top16(row(0), row(1) + [pad] * (k - len(row(1))))
        second = _sort16(row(2) + row(3) + row(4) + row(5) + row(6))
        third_items = row(7) + [pair(a, 0) for a in range(8, k)]
        third = _sort16(third_items + [pad] * (k - len(third_items)))
        best = _merge_top16(_merge_top16(first, second), third)

        e = [jnp.exp(item[0] - best[0][0]) for item in best]
        z = e[0]
        for v in e[1:]:
            z = z + v
        inv = 1.0 / z
        for r in range(k):
            sel = h * k + r
            expert = best[r][2]
            a_idx = jnp.floor(expert * (1.0 / PEER_KEYS))
            za_ref[pl.ds(sel, SUBLANES, stride=ROW_PITCH), :] = a_idx
            zb_ref[pl.ds(sel, SUBLANES, stride=ROW_PITCH), :] = expert - a_idx * float(PEER_KEYS)
            zg_ref[pl.ds(sel, SUBLANES, stride=ROW_PITCH), :] = e[r] * inv
        return carry

    lax.fori_loop(0, PEER_HEADS, head, 0)
    for j in range(SUBLANES):
        rows = slice(j * LANES, (j + 1) * LANES)
        src = slice(j * ROW_PITCH, j * ROW_PITCH + PEER_HEADS * k)
        ia_ref[rows, :] = za_ref[src, :].T
        ib_ref[rows, :] = zb_ref[src, :].T
        gate_ref[rows, :] = zg_ref[src, :].T


def _peer_route(x, n2, wq, keys):
    t = x.shape[0]
    tm = ROUTE_TILE
    sel_spec = pl.BlockSpec((tm, PEER_HEADS * PEER_TOPK), lambda i: (i, 0))
    sel_shape = jax.ShapeDtypeStruct((t, PEER_HEADS * PEER_TOPK), F32)
    z_scratch = pltpu.VMEM((SUBLANES * ROW_PITCH, LANES), F32)
    return pl.pallas_call(
        _peer_route_kernel,
        grid=(pl.cdiv(t, tm),),
        in_specs=[pl.BlockSpec((tm, D_MODEL), lambda i: (i, 0)),
                  _const_spec((1, D_MODEL)),
                  _const_spec(wq.shape),
                  _const_spec(keys.shape)],
        out_specs=(pl.BlockSpec((tm, D_MODEL), lambda i: (i, 0)), sel_spec, sel_spec, sel_spec),
        out_shape=(jax.ShapeDtypeStruct((t, D_MODEL), BF16), sel_shape, sel_shape, sel_shape),
        scratch_shapes=[pltpu.VMEM((2 * PEER_HEADS, tm, PEER_HALF), BF16),
                        z_scratch, z_scratch, z_scratch, z_scratch, z_scratch],
        compiler_params=_params(("arbitrary",)),
        name="peer_route",
    )(x, n2, wq, keys)


def _peer_mlp_kernel(xn_ref, u_hbm, v_hbm, ia_ref, ib_ref, gate_ref, x_ref, fw_ref, *rest,
                     layer, final, n_prompt_tiles, n_sample):
    if final:
        o_ref, os_ref, acc_ref, act_a_ref, act_b_ref, w_ref, g_ref, ubuf, vbuf, sem = rest
    else:
        o_ref, acc_ref, act_a_ref, act_b_ref, w_ref, g_ref, ubuf, vbuf, sem = rest
    tm = xn_ref.shape[0]
    n_tiles = PEER_EXPERTS // EXPERT_TILE
    rows_per_tile = EXPERT_TILE // PEER_KEYS

    def table_copy(hbm, buf, which, tile, slot):
        rows = pl.ds(pl.multiple_of(tile * EXPERT_TILE, EXPERT_TILE), EXPERT_TILE)
        return pltpu.make_async_copy(hbm.at[layer, rows, :], buf.at[slot], sem.at[which, slot])

    u_copy = functools.partial(table_copy, u_hbm, ubuf, 0)
    v_copy = functools.partial(table_copy, v_hbm, vbuf, 1)
    u_copy(0, 0).start()
    v_copy(0, 0).start()
    u_copy(1, 1).start()

    acc_ref[...] = jnp.zeros_like(acc_ref)
    key = lax.broadcasted_iota(jnp.int32, (PEER_KEYS, PEER_HEADS * PEER_TOPK), 0).astype(F32)

    def build(t, carry):
        gate = gate_ref[pl.ds(t, 1), :]
        sel_gate = jnp.where(key == ia_ref[pl.ds(t, 1), :], gate, 0.0)
        hi = sel_gate.astype(BF16)
        lo = (sel_gate - hi.astype(F32)).astype(BF16)
        onehot = jnp.where(key == ib_ref[pl.ds(t, 1), :], 1.0, 0.0).astype(BF16)
        g_ref[pl.ds(pl.multiple_of(t * ROW_PITCH, SUBLANES), PEER_KEYS), :] = _dot_nt(
            jnp.concatenate([hi, lo], axis=1), jnp.concatenate([onehot, onehot], axis=1))
        return carry

    lax.fori_loop(0, tm, build, 0, unroll=BUILD_UNROLL)
    u_copy(0, 0).wait()
    act_a_ref[...] = _dot_nt(xn_ref[...], ubuf[0])

    def step(e, slot, act_ref, act_next_ref):
        @pl.when(e + 2 < n_tiles)
        def _():
            u_copy(e + 2, slot).start()

        @pl.when(e + 1 < n_tiles)
        def _():
            v_copy(e + 1, 1 - slot).start()
            u_copy(e + 1, 1 - slot).wait()

        v_copy(e, slot).wait()
        act_next_ref[...] = _dot_nt(xn_ref[...], ubuf[1 - slot])
        for j in range(rows_per_tile):
            i1 = e * rows_per_tile + j
            cols = slice(j * PEER_KEYS, (j + 1) * PEER_KEYS)
            gate = g_ref[pl.ds(i1, tm, stride=ROW_PITCH), :]
            w_ref[:, cols] = (gate * _gelu(act_ref[:, cols])).astype(BF16)
        acc_ref[...] += _dot(w_ref[...], vbuf[slot])

    def pair(p, carry):
        step(2 * p, 0, act_a_ref, act_b_ref)
        step(2 * p + 1, 1, act_b_ref, act_a_ref)
        return carry

    lax.fori_loop(0, n_tiles // 2, pair, 0)

    y = x_ref[...] + acc_ref[...]
    if not final:
        o_ref[...] = y
        return
    y = _rms(y) * fw_ref[...]
    tile = pl.program_id(0)

    @pl.when(tile < n_prompt_tiles)
    def _():
        o_ref[...] = y

    @pl.when(tile == pl.num_programs(0) - 1)
    def _():
        os_ref[...] = y[tm - n_sample:, :]


def _peer_mlp(xn, u_bf, v_bf, route, x, fw, tm, layer, final, n_prompt):
    t = x.shape[0]
    n_sample = t - n_prompt
    assert n_sample <= tm and n_sample % SUBLANES == 0
    n_prompt_tiles = pl.cdiv(n_prompt, tm)
    n_tiles = PEER_EXPERTS // EXPERT_TILE
    assert n_tiles % 2 == 0 and n_tiles >= 2
    ia, ib, gate = route
    tok_spec = lambda width: pl.BlockSpec((tm, width), lambda i: (i, 0))
    if final:
        out_specs = (pl.BlockSpec((tm, D_MODEL), lambda i: (jnp.minimum(i, n_prompt_tiles - 1), 0)),
                     pl.BlockSpec((n_sample, D_MODEL), lambda i: (0, 0)))
        out_shape = (jax.ShapeDtypeStruct((n_prompt, D_MODEL), F32),
                     jax.ShapeDtypeStruct((n_sample, D_MODEL), F32))
    else:
        out_specs = tok_spec(D_MODEL)
        out_shape = jax.ShapeDtypeStruct((t, D_MODEL), F32)
    table_buffer = pltpu.VMEM((2, EXPERT_TILE, D_MODEL), BF16)
    return pl.pallas_call(
        functools.partial(_peer_mlp_kernel, layer=layer, final=final, n_prompt_tiles=n_prompt_tiles,
                          n_sample=n_sample),
        grid=(t // tm,),
        in_specs=[tok_spec(D_MODEL),
                  pl.BlockSpec(memory_space=pl.ANY),
                  pl.BlockSpec(memory_space=pl.ANY),
                  tok_spec(PEER_HEADS * PEER_TOPK), tok_spec(PEER_HEADS * PEER_TOPK),
                  tok_spec(PEER_HEADS * PEER_TOPK),
                  tok_spec(D_MODEL),
                  _const_spec((1, D_MODEL))],
        out_specs=out_specs,
        out_shape=out_shape,
        scratch_shapes=[pltpu.VMEM((tm, D_MODEL), F32),
                        pltpu.VMEM((tm, EXPERT_TILE), F32),
                        pltpu.VMEM((tm, EXPERT_TILE), F32),
                        pltpu.VMEM((tm, EXPERT_TILE), BF16),
                        pltpu.VMEM((tm * ROW_PITCH, PEER_KEYS), F32),
                        table_buffer, table_buffer,
                        pltpu.SemaphoreType.DMA((2, 2))],
        compiler_params=_params(("arbitrary",)),
        name="peer_mlp",
    )(xn, u_bf, v_bf, ia, ib, gate, x, fw)


def _token_tile(t):
    for tm in (512, 384, 256, 128):
        if t % tm == 0:
            return tm
    raise ValueError(f"token count {t} must be a multiple of {LANES}")


def _rope_tables(pos):
    half = RET_HEAD_DIM // 2
    inv = ROPE_BASE ** (-jnp.arange(half, dtype=F32) / half)
    ang = pos.astype(F32)[:, None] * inv[None, :]
    cos = jnp.cos(ang)
    sin = jnp.sin(ang)
    return jnp.concatenate([cos, cos], axis=1), jnp.concatenate([-sin, sin], axis=1)


def _block_diag(w):
    eye = jnp.eye(LRU_BLOCKS, dtype=w.dtype)
    return jnp.einsum('hij,hg->higj', w, eye).reshape(LRU_WIDTH, LRU_WIDTH)


def kernel(x_prompt, x_sample, state_ssd, state_ssd_conv, state_ret, state_lru, state_lru_conv, norm1_w, w_in, ssd_conv_w, ssd_conv_b, ssd_dt_bias, ssd_a_log, ssd_d, ssd_norm_w, lru_conv_w, lru_conv_b, lru_wa, lru_ba, lru_wi, lru_bi, lru_lambda, w_out, norm2_w, peer_wq, peer_keys, peer_u, peer_v, final_norm_w):
    batch, seq, _ = x_prompt.shape
    n_sample, dec_seq, _ = x_sample.shape
    depth = w_in.shape[0]
    assert dec_seq == 1 and seq % SCAN_CHUNK == 0 and n_sample % SAMPLE_TILE == 0
    n_prompt = batch * seq
    t = n_prompt + n_sample
    tm = _token_tile(t)

    x = jnp.concatenate([x_prompt.reshape(n_prompt, D_MODEL), x_sample.reshape(n_sample, D_MODEL)], axis=0)
    cos_p, sin_p = _rope_tables(jnp.arange(seq))
    cos_s, sin_s = _rope_tables(PAST_LEN + jnp.arange(dec_seq))
    row = lambda a: a.reshape(1, -1).astype(F32)
    per_channel = lambda a: jnp.repeat(a.astype(F32), SSD_HEAD_DIM).reshape(1, SSD_WIDTH)
    per_head = lambda a: jnp.pad(a.astype(F32), (0, LANES - SSD_HEADS)).reshape(1, LANES)
    expand = (jnp.arange(LANES)[:, None] == jnp.arange(SSD_WIDTH)[None, :] // SSD_HEAD_DIM).astype(F32)
    off = SSD_WIDTH + SSD_CONV_DIM

    carried = (jnp.swapaxes(state_ssd, -1, -2), state_ssd_conv, state_ret, state_lru, state_lru_conv)
    u_bf = peer_u.astype(BF16)
    v_bf = peer_v.astype(BF16)
    new_p, st_s = [], None
    for i in range(depth):
        wi_ = w_in[i]
        dt_cols = wi_[:, off:off + SSD_HEADS]
        w_in_p = jnp.concatenate(
            [wi_[:, :off],
             jnp.pad(dt_cols, ((0, 0), (0, LANES - SSD_HEADS))),
             wi_[:, off + SSD_HEADS:]], axis=1).astype(BF16)
        wts = (ssd_conv_w[i], row(ssd_conv_b[i]), per_head(ssd_dt_bias[i]), per_head(ssd_a_log[i]),
               expand, per_channel(ssd_d[i]),
               row(ssd_norm_w[i]), lru_conv_w[i], row(lru_conv_b[i]),
               _block_diag(lru_wa[i]).astype(BF16), row(lru_ba[i]),
               _block_diag(lru_wi[i]).astype(BF16), row(lru_bi[i]), row(lru_lambda[i]),
               w_out[i].astype(BF16))

        proj = _inproj(x, row(norm1_w[i]), w_in_p, tm)
        x_mixed, *st_p = _prompt_mixer(proj, x, cos_p, sin_p, wts, batch, seq)
        st_p[3] = st_p[3].reshape(batch, LRU_WIDTH)
        x, *st_s = _sample_mixer(proj, x, x_mixed, cos_s, sin_s, carried, st_s, wts, i, n_prompt, n_sample)
        new_p.append(st_p)

        keys_bf = peer_keys[i].reshape(PEER_HEADS * 2, PEER_KEYS, PEER_HALF).astype(BF16)
        xn, *route = _peer_route(x, row(norm2_w[i]), peer_wq[i].astype(BF16), keys_bf)
        x = _peer_mlp(xn, u_bf, v_bf, route, x, row(final_norm_w), tm, i, i == depth - 1, n_prompt)

    y_prompt, y_sample = x
    stk = lambda states, j: jnp.stack([s[j] for s in states])
    return (y_prompt.reshape(batch, seq, D_MODEL), y_sample.reshape(n_sample, dec_seq, D_MODEL),
            stk(new_p, 0), stk(new_p, 1), stk(new_p, 2), stk(new_p, 3), stk(new_p, 4),
            jnp.swapaxes(st_s[0], -1, -2), *st_s[1:])
```

```python
import functools

import numpy as np
import jax
import jax.numpy as jnp
from jax import lax
from jax.experimental import pallas as pl
from jax.experimental.pallas import tpu as pltpu

F32 = jnp.float32
BF16 = jnp.bfloat16

D_MODEL = 1024
PAST_LEN = 16384
D_MIX = 2 * D_MODEL
CONV_WIDTH = 4
NORM_EPS = 1e-6
SSD_WIDTH = D_MIX // 2
SSD_HEAD_DIM = 64
SSD_HEADS = SSD_WIDTH // SSD_HEAD_DIM
SSD_GROUPS = 2
SSD_HPG = SSD_HEADS // SSD_GROUPS
SSD_STATE = 128
SSD_CONV_DIM = SSD_WIDTH + 2 * SSD_GROUPS * SSD_STATE
RET_WIDTH = D_MIX // 4
RET_HEADS = 4
RET_HEAD_DIM = RET_WIDTH // RET_HEADS
ROPE_BASE = 10000.0
LRU_WIDTH = D_MIX // 4
LRU_BLOCKS = 8
LRU_BLOCK_DIM = LRU_WIDTH // LRU_BLOCKS
LRU_C = 8.0
PEER_KEYS = 128
PEER_EXPERTS = PEER_KEYS * PEER_KEYS
PEER_HEADS = 8
PEER_TOPK = 16
PEER_QUERY_DIM = 256
PEER_HALF = PEER_QUERY_DIM // 2

LANES = 128
SUBLANES = 8
VMEM_LIMIT = 56 * 1024 * 1024

COL_Z = 0
COL_XBC = COL_Z + SSD_WIDTH
COL_DT16 = COL_XBC + SSD_CONV_DIM
COL_Q = COL_DT16 + LANES
COL_K = COL_Q + RET_WIDTH
COL_V = COL_K + RET_WIDTH
COL_G = COL_V + RET_WIDTH
COL_GATE = COL_G + RET_WIDTH
COL_XR = COL_GATE + LRU_WIDTH
PROJ_WIDTH = COL_XR + LRU_WIDTH

SCAN_CHUNK = 128
SAMPLE_TILE = SUBLANES
SAMPLE_HEAD_BLOCK = 4
EXPERT_TILE = 1024
BUILD_UNROLL = 64
LOG_GAMMA = [float(np.log(1.0 - 2.0 ** (-5.0 - h))) for h in range(RET_HEADS)]

_NT = (((1,), (1,)), ((), ()))


def _rms(x):
    return x * lax.rsqrt(jnp.mean(x * x, axis=-1, keepdims=True) + NORM_EPS)


def _softplus(x):
    return jnp.maximum(x, 0.0) + jnp.log1p(jnp.exp(-jnp.abs(x)))


def _silu(x):
    return x * jax.nn.sigmoid(x)


def _gelu(x):
    return 0.5 * x * (1.0 + lax.erf(x * np.float32(np.sqrt(0.5))))


def _neg_expm1(x):
    return -jnp.tanh(0.5 * x) * (jnp.exp(x) + 1.0)


def _dot(a, b):
    return jnp.dot(a, b, preferred_element_type=F32)


def _dot_nt(a, b):
    return lax.dot_general(a, b, _NT, preferred_element_type=F32)


def _split3(x):
    hi = x.astype(BF16)
    rest = x - hi.astype(F32)
    mid = rest.astype(BF16)
    return hi, mid, (rest - mid.astype(F32)).astype(BF16)


def _dot_select_cols(x, select_bf):
    return sum(_dot(part, select_bf) for part in _split3(x))


def _dot_prefix_rows(mask_bf, x):
    return sum(_dot(mask_bf, part) for part in _split3(x))


def _params(semantics):
    return pltpu.CompilerParams(dimension_semantics=semantics, vmem_limit_bytes=VMEM_LIMIT)


def _const_spec(shape):
    zeros = (0,) * len(shape)
    return pl.BlockSpec(shape, lambda *_: zeros)


def _inproj_kernel(x_ref, n_ref, w_ref, o_ref):
    h = (_rms(x_ref[...]) * n_ref[...]).astype(BF16)
    step = 8 * LANES
    for j in range(0, PROJ_WIDTH, step):
        hi = min(j + step, PROJ_WIDTH)
        o_ref[:, j:hi] = _dot(h, w_ref[:, j:hi])


def _inproj(x, n1, w_in_p, tm):
    t = x.shape[0]
    return pl.pallas_call(
        _inproj_kernel,
        grid=(t // tm,),
        in_specs=[pl.BlockSpec((tm, D_MODEL), lambda i: (i, 0)),
                  _const_spec((1, D_MODEL)),
                  pl.BlockSpec((D_MODEL, PROJ_WIDTH), lambda i: (0, 0), pipeline_mode=pl.Buffered(1))],
        out_specs=pl.BlockSpec((tm, PROJ_WIDTH), lambda i: (i, 0)),
        out_shape=jax.ShapeDtypeStruct((t, PROJ_WIDTH), F32),
        compiler_params=_params(("arbitrary",)),
        name="inproj",
    )(x, n1, w_in_p)


def _prompt_mixer_kernel(proj_ref, x_ref, cos_ref, sin_ref, scw_ref, scb_ref, dtb16_ref, alog16_ref,
                         expand_ref, dskx_ref, snw_ref, lcw_ref, lcb_ref, wa_ref, ba_ref,
                         wi_ref, bi_ref, lam_ref, wout_ref,
                         xo_ref, ssd_ref, sbuf_ref, ret_ref, lruh_ref, lbuf_ref,
                         xps_ref, xpl_ref, mix_ref):
    c = SCAN_CHUNK
    step = pl.program_id(1)
    pad = SUBLANES
    keep = CONV_WIDTH - 1

    @pl.when(step == 0)
    def _():
        ssd_ref[...] = jnp.zeros_like(ssd_ref)
        ret_ref[...] = jnp.zeros_like(ret_ref)
        lruh_ref[...] = jnp.zeros_like(lruh_ref)
        xps_ref[0:pad, :] = jnp.zeros((pad, SSD_CONV_DIM), F32)
        xpl_ref[0:pad, :] = jnp.zeros((pad, LRU_WIDTH), F32)

    def conv(xp_ref, x_new, w_ref, b_ref, buf_out_ref):
        xp_ref[pad:pad + c, :] = x_new
        y = b_ref[...] + xp_ref[pad - keep:pad - keep + c, :] * w_ref[0:1, :]
        for tap in range(1, CONV_WIDTH):
            y = y + xp_ref[pad - keep + tap:pad - keep + tap + c, :] * w_ref[tap:tap + 1, :]
        tail = xp_ref[pad + c - keep:pad + c, :]
        xp_ref[pad - keep:pad, :] = tail
        buf_out_ref[0] = tail
        return y

    row = lax.broadcasted_iota(jnp.int32, (c, c), 0)
    col = lax.broadcasted_iota(jnp.int32, (c, c), 1)
    causal = row >= col
    tri = causal.astype(F32).astype(BF16)
    rowf = lax.broadcasted_iota(jnp.int32, (c, LANES), 0).astype(F32)

    xbc = _silu(conv(xps_ref, proj_ref[:, COL_XBC:COL_XBC + SSD_CONV_DIM], scw_ref, scb_ref, sbuf_ref))
    xs = xbc[:, :SSD_WIDTH]
    bm = xbc[:, SSD_WIDTH:SSD_WIDTH + SSD_GROUPS * SSD_STATE]
    cm = xbc[:, SSD_WIDTH + SSD_GROUPS * SSD_STATE:]
    dt16 = _softplus(proj_ref[:, COL_DT16:COL_DT16 + LANES] + dtb16_ref[...])
    cum16 = _dot_prefix_rows(tri, -dt16 * jnp.exp(alog16_ref[...]))
    cum16_t = cum16.T
    wide = _dot_select_cols(jnp.concatenate([dt16, cum16], axis=0), expand_ref[...])
    dtx = wide[:c]
    cumx = wide[c:]
    v = xs * dtx
    v_bf = v.astype(BF16)
    cum_last = cumx[c - 1:c, :]
    vte_bf = (v * jnp.exp(cum_last - cumx)).astype(BF16)
    ecum = jnp.exp(cumx)
    chunk_decay = jnp.exp(cum_last)
    for g in range(SSD_GROUPS):
        bmg = bm[:, g * SSD_STATE:(g + 1) * SSD_STATE]
        cmg_bf = cm[:, g * SSD_STATE:(g + 1) * SSD_STATE].astype(BF16)
        scores = _dot_nt(cmg_bf, bmg.astype(BF16))
        bmg_t_bf = bmg.T.astype(BF16)
        for r in range(SSD_HPG):
            h = g * SSD_HPG + r
            hs = slice(h * SSD_HEAD_DIM, (h + 1) * SSD_HEAD_DIM)
            seg = cum16[:, h:h + 1] - cum16_t[h:h + 1, :]
            decay = jnp.exp(jnp.where(causal, seg, -jnp.inf))
            y_intra = _dot((scores * decay).astype(BF16), v_bf[:, hs])
            s_in = ssd_ref[0, h]
            y_inter = _dot(cmg_bf, s_in.astype(BF16)) * ecum[:, hs]
            ssd_ref[0, h] = s_in * chunk_decay[:, hs] + _dot(bmg_t_bf, vte_bf[:, hs])
            mix_ref[:, hs] = y_intra + y_inter
    y = mix_ref[:, :SSD_WIDTH] + dskx_ref[...] * xs
    y = y * _silu(proj_ref[:, COL_Z:COL_Z + SSD_WIDTH])
    mix_ref[:, :SSD_WIDTH] = _rms(y) * snw_ref[...]

    cosf = cos_ref[...]
    sinf = sin_ref[...]
    rowcol = (row - col).astype(F32)
    for h in range(RET_HEADS):
        lg = LOG_GAMMA[h]
        hs = slice(h * RET_HEAD_DIM, (h + 1) * RET_HEAD_DIM)
        qh = proj_ref[:, COL_Q + h * RET_HEAD_DIM:COL_Q + (h + 1) * RET_HEAD_DIM]
        kh = proj_ref[:, COL_K + h * RET_HEAD_DIM:COL_K + (h + 1) * RET_HEAD_DIM]
        vh = proj_ref[:, COL_V + h * RET_HEAD_DIM:COL_V + (h + 1) * RET_HEAD_DIM]
        gh = proj_ref[:, COL_G + h * RET_HEAD_DIM:COL_G + (h + 1) * RET_HEAD_DIM]
        qh = qh * cosf + pltpu.roll(qh, RET_HEAD_DIM // 2, 1) * sinf
        kh = (kh * cosf + pltpu.roll(kh, RET_HEAD_DIM // 2, 1) * sinf) * (RET_HEAD_DIM ** -0.5)
        qh_bf = qh.astype(BF16)
        scores = _dot_nt(qh_bf, kh.astype(BF16))
        decay = jnp.exp(jnp.where(causal, rowcol * lg, -jnp.inf))
        y_intra = _dot((scores * decay).astype(BF16), vh.astype(BF16))
        s_in = ret_ref[0, h]
        y_inter = _dot(qh_bf, s_in.astype(BF16)) * jnp.exp((rowf + 1.0) * lg)
        v_to_end = (vh * jnp.exp((c - 1.0 - rowf) * lg)).astype(BF16)
        ret_ref[0, h] = s_in * float(np.exp(c * lg)) + _dot(kh.T.astype(BF16), v_to_end)
        mix_ref[:, SSD_WIDTH + h * RET_HEAD_DIM:SSD_WIDTH + (h + 1) * RET_HEAD_DIM] = (
            _rms(y_intra + y_inter) * _silu(gh))

    xc = conv(xpl_ref, proj_ref[:, COL_XR:COL_XR + LRU_WIDTH], lcw_ref, lcb_ref, lbuf_ref)
    xc_bf = xc.astype(BF16)
    r_gate = jax.nn.sigmoid(_dot(xc_bf, wa_ref[...]) + ba_ref[...])
    i_gate = jax.nn.sigmoid(_dot(xc_bf, wi_ref[...]) + bi_ref[...])
    log_a = -LRU_C * r_gate * _softplus(-lam_ref[...])
    a_run = jnp.exp(log_a)
    b_run = jnp.sqrt(_neg_expm1(2.0 * log_a)) * (i_gate * xc)
    rowl = lax.broadcasted_iota(jnp.int32, (c, LRU_WIDTH), 0)
    shift = 1
    while shift < c:
        valid = rowl >= shift
        a_prev = pltpu.roll(a_run, shift, 0)
        b_prev = pltpu.roll(b_run, shift, 0)
        b_run = jnp.where(valid, a_run * b_prev + b_run, b_run)
        a_run = jnp.where(valid, a_run * a_prev, a_run)
        shift *= 2
    h_all = b_run + a_run * lruh_ref[0]
    lruh_ref[0] = h_all[c - 1:c, :]
    mix_ref[:, SSD_WIDTH + RET_WIDTH:] = h_all * _gelu(proj_ref[:, COL_GATE:COL_GATE + LRU_WIDTH])

    xo_ref[...] = x_ref[...] + _dot(mix_ref[...].astype(BF16), wout_ref[...])


def _prompt_mixer(proj, x_all, cosf, sinf, wts, batch, seq):
    c = SCAN_CHUNK
    nc = seq // c
    t = x_all.shape[0]
    row_spec = lambda width: pl.BlockSpec((c, width), lambda b, s: (b * nc + s, 0))
    state_spec = lambda shape: pl.BlockSpec((1,) + shape, lambda b, s: (b,) + (0,) * len(shape))
    in_specs = [row_spec(PROJ_WIDTH), row_spec(D_MODEL),
                pl.BlockSpec((c, LANES), lambda b, s: (s, 0)),
                pl.BlockSpec((c, LANES), lambda b, s: (s, 0))]
    in_specs += [_const_spec(w.shape) for w in wts]
    out_shapes = (jax.ShapeDtypeStruct((t, D_MODEL), F32),
                  jax.ShapeDtypeStruct((batch, SSD_HEADS, SSD_STATE, SSD_HEAD_DIM), F32),
                  jax.ShapeDtypeStruct((batch, CONV_WIDTH - 1, SSD_CONV_DIM), F32),
                  jax.ShapeDtypeStruct((batch, RET_HEADS, RET_HEAD_DIM, RET_HEAD_DIM), F32),
                  jax.ShapeDtypeStruct((batch, 1, LRU_WIDTH), F32),
                  jax.ShapeDtypeStruct((batch, CONV_WIDTH - 1, LRU_WIDTH), F32))
    out_specs = (row_spec(D_MODEL),
                 state_spec((SSD_HEADS, SSD_STATE, SSD_HEAD_DIM)),
                 state_spec((CONV_WIDTH - 1, SSD_CONV_DIM)),
                 state_spec((RET_HEADS, RET_HEAD_DIM, RET_HEAD_DIM)),
                 state_spec((1, LRU_WIDTH)),
                 state_spec((CONV_WIDTH - 1, LRU_WIDTH)))
    return pl.pallas_call(
        _prompt_mixer_kernel,
        grid=(batch, nc),
        in_specs=in_specs,
        out_specs=out_specs,
        out_shape=out_shapes,
        scratch_shapes=[pltpu.VMEM((SUBLANES + c, SSD_CONV_DIM), F32),
                        pltpu.VMEM((SUBLANES + c, LRU_WIDTH), F32),
                        pltpu.VMEM((c, D_MIX), F32)],
        compiler_params=_params(("arbitrary", "arbitrary")),
        name="prompt_mixer",
    )(proj, x_all, cosf, sinf, *wts)


def _sample_mixer_kernel(proj_ref, x_ref, cos_ref, sin_ref, ssd_in_ref, sbuf_in_ref, ret_in_ref,
                         lruh_in_ref, lbuf_in_ref, scw_ref, scb_ref, dtb16_ref, alog16_ref,
                         expand_ref, dskx_ref, snw_ref, lcw_ref, lcb_ref, wa_ref, ba_ref,
                         wi_ref, bi_ref, lam_ref, wout_ref, *rest, n_aliased):
    (xo_ref, ssd_ref, sbuf_ref, ret_ref, lruh_ref, lbuf_ref,
     vh_ref, ah_ref, bm_ref, cm_ref, yh_ref, xs_ref, mix_ref) = rest[n_aliased:]
    bt = SAMPLE_TILE
    hb = pl.program_id(1)
    nhb = pl.num_programs(1)
    keep = CONV_WIDTH - 1

    def conv(x_new, buf_in_ref, w_ref, b_ref, buf_out_ref):
        y = b_ref[...] + x_new * w_ref[keep:keep + 1, :]
        for tap in range(keep):
            y = y + buf_in_ref[:, tap, :] * w_ref[tap:tap + 1, :]
        for tap in range(keep - 1):
            buf_out_ref[:, tap, :] = buf_in_ref[:, tap + 1, :]
        buf_out_ref[:, keep - 1, :] = x_new
        return y

    @pl.when(hb == 0)
    def _():
        xbc = _silu(conv(proj_ref[:, COL_XBC:COL_XBC + SSD_CONV_DIM], sbuf_in_ref, scw_ref, scb_ref, sbuf_ref))
        xs = xbc[:, :SSD_WIDTH]
        xs_ref[...] = xs
        dt16 = _softplus(proj_ref[:, COL_DT16:COL_DT16 + LANES] + dtb16_ref[...])
        a16 = jnp.exp(-dt16 * jnp.exp(alog16_ref[...]))
        wide = _dot_select_cols(jnp.concatenate([dt16, a16], axis=0), expand_ref[...])
        dtx = wide[:bt]
        a = wide[bt:]
        v = xs * dtx
        for h in range(SSD_HEADS):
            hs = slice(h * SSD_HEAD_DIM, (h + 1) * SSD_HEAD_DIM)
            vh_ref[h] = v[:, hs].T
            ah_ref[h] = a[:, hs].T
        for g in range(SSD_GROUPS):
            bm_ref[g] = xbc[:, SSD_WIDTH + g * SSD_STATE:SSD_WIDTH + (g + 1) * SSD_STATE]
            off = SSD_WIDTH + SSD_GROUPS * SSD_STATE
            cm_ref[g] = xbc[:, off + g * SSD_STATE:off + (g + 1) * SSD_STATE]

    g = (hb * SAMPLE_HEAD_BLOCK) // SSD_HPG
    bm = bm_ref[g]
    cm = cm_ref[g]
    cm_bf = cm.astype(BF16)
    lane = lax.broadcasted_iota(jnp.int32, (SSD_HEAD_DIM, bt), 1)
    for r in range(SAMPLE_HEAD_BLOCK):
        h = hb * SAMPLE_HEAD_BLOCK + r
        vh_t = vh_ref[h]
        ah_t = ah_ref[h]
        y_t = jnp.zeros((SSD_HEAD_DIM, bt), F32)
        for b in range(bt):
            s_new = ssd_in_ref[b, r] * ah_t[:, b:b + 1] + vh_t[:, b:b + 1] * bm[b:b + 1, :]
            ssd_ref[b, r] = s_new
            y_all = _dot_nt(s_new.astype(BF16), cm_bf)
            y_t = jnp.where(lane == b, y_all, y_t)
        yh_ref[h] = y_t.T

    @pl.when(hb == nhb - 1)
    def _():
        for h in range(SSD_HEADS):
            mix_ref[:, h * SSD_HEAD_DIM:(h + 1) * SSD_HEAD_DIM] = yh_ref[h]
        y = mix_ref[:, :SSD_WIDTH] + dskx_ref[...] * xs_ref[...]
        y = y * _silu(proj_ref[:, COL_Z:COL_Z + SSD_WIDTH])
        mix_ref[:, :SSD_WIDTH] = _rms(y) * snw_ref[...]

        cosf = cos_ref[...]
        sinf = sin_ref[...]
        for h in range(RET_HEADS):
            lg = LOG_GAMMA[h]
            qh = proj_ref[:, COL_Q + h * RET_HEAD_DIM:COL_Q + (h + 1) * RET_HEAD_DIM]
            kh = proj_ref[:, COL_K + h * RET_HEAD_DIM:COL_K + (h + 1) * RET_HEAD_DIM]
            vh = proj_ref[:, COL_V + h * RET_HEAD_DIM:COL_V + (h + 1) * RET_HEAD_DIM]
            gh = proj_ref[:, COL_G + h * RET_HEAD_DIM:COL_G + (h + 1) * RET_HEAD_DIM]
            qh = qh * cosf + pltpu.roll(qh, RET_HEAD_DIM // 2, 1) * sinf
            kh = (kh * cosf + pltpu.roll(kh, RET_HEAD_DIM // 2, 1) * sinf) * (RET_HEAD_DIM ** -0.5)
            qt = qh.T
            kt = kh.T
            hs = slice(SSD_WIDTH + h * RET_HEAD_DIM, SSD_WIDTH + (h + 1) * RET_HEAD_DIM)
            for b in range(bt):
                s_new = ret_in_ref[b, h] * float(np.exp(lg)) + kt[:, b:b + 1] * vh[b:b + 1, :]
                ret_ref[b, h] = s_new
                mix_ref[b:b + 1, hs] = jnp.sum(qt[:, b:b + 1] * s_new, axis=0, keepdims=True)
            mix_ref[:, hs] = _rms(mix_ref[:, hs]) * _silu(gh)

        xc = conv(proj_ref[:, COL_XR:COL_XR + LRU_WIDTH], lbuf_in_ref, lcw_ref, lcb_ref, lbuf_ref)
        xc_bf = xc.astype(BF16)
        r_gate = jax.nn.sigmoid(_dot(xc_bf, wa_ref[...]) + ba_ref[...])
        i_gate = jax.nn.sigmoid(_dot(xc_bf, wi_ref[...]) + bi_ref[...])
        log_a = -LRU_C * r_gate * _softplus(-lam_ref[...])
        h_new = (jnp.exp(log_a) * lruh_in_ref[...]
                 + jnp.sqrt(_neg_expm1(2.0 * log_a)) * (i_gate * xc))
        lruh_ref[...] = h_new
        mix_ref[:, SSD_WIDTH + RET_WIDTH:] = h_new * _gelu(proj_ref[:, COL_GATE:COL_GATE + LRU_WIDTH])

        xo_ref[...] = x_ref[...] + _dot(mix_ref[...].astype(BF16), wout_ref[...])


def _sample_mixer(proj, x_all, x_mixed, cosf, sinf, states, new_states, wts, layer, n_prompt, n_sample):
    bt = SAMPLE_TILE
    hbk = SAMPLE_HEAD_BLOCK
    base = n_prompt // bt
    t = x_all.shape[0]
    row_spec = lambda width: pl.BlockSpec((bt, width), lambda i, j: (base + i, 0))
    ssd_spec = pl.BlockSpec((None, bt, hbk, SSD_HEAD_DIM, SSD_STATE), lambda i, j: (layer, i, j, 0, 0))
    sbuf_spec = pl.BlockSpec((None, bt, CONV_WIDTH - 1, SSD_CONV_DIM), lambda i, j: (layer, i, 0, 0))
    ret_spec = pl.BlockSpec((None, bt, RET_HEADS, RET_HEAD_DIM, RET_HEAD_DIM), lambda i, j: (layer, i, 0, 0, 0))
    lruh_spec = pl.BlockSpec((None, bt, LRU_WIDTH), lambda i, j: (layer, i, 0))
    lbuf_spec = pl.BlockSpec((None, bt, CONV_WIDTH - 1, LRU_WIDTH), lambda i, j: (layer, i, 0, 0))
    state_specs = [ssd_spec, sbuf_spec, ret_spec, lruh_spec, lbuf_spec]
    aliased = (x_mixed,) + (tuple(new_states) if new_states is not None else ())
    in_specs = [row_spec(PROJ_WIDTH), row_spec(D_MODEL),
                _const_spec((1, LANES)), _const_spec((1, LANES))] + state_specs
    in_specs += [_const_spec(w.shape) for w in wts]
    n_in = len(in_specs)
    in_specs += [pl.BlockSpec(memory_space=pl.ANY)] * len(aliased)
    out_shapes = (jax.ShapeDtypeStruct((t, D_MODEL), F32),) + tuple(
        jax.ShapeDtypeStruct(s.shape, F32) for s in states)
    out_specs = (row_spec(D_MODEL),) + tuple(state_specs)
    return pl.pallas_call(
        functools.partial(_sample_mixer_kernel, n_aliased=len(aliased)),
        grid=(n_sample // bt, SSD_HEADS // hbk),
        in_specs=in_specs,
        out_specs=out_specs,
        out_shape=out_shapes,
        scratch_shapes=[pltpu.VMEM((SSD_HEADS, SSD_HEAD_DIM, bt), F32),
                        pltpu.VMEM((SSD_HEADS, SSD_HEAD_DIM, bt), F32),
                        pltpu.VMEM((SSD_GROUPS, bt, SSD_STATE), F32),
                        pltpu.VMEM((SSD_GROUPS, bt, SSD_STATE), F32),
                        pltpu.VMEM((SSD_HEADS, bt, SSD_HEAD_DIM), F32),
                        pltpu.VMEM((bt, SSD_WIDTH), F32),
                        pltpu.VMEM((bt, D_MIX), F32)],
        input_output_aliases={n_in + k: k for k in range(len(aliased))},
        compiler_params=_params(("arbitrary", "arbitrary")),
        name="sample_mixer",
    )(proj, x_all, cosf, sinf, *states, *wts, *aliased)


def _batcher_pairs(n):
    pairs = []

    def merge(lo, hi, r):
        step = r * 2
        if step < hi - lo:
            merge(lo, hi, step)
            merge(lo + r, hi, step)
            pairs.extend((i, i + r) for i in range(lo + r, hi - r, step))
        else:
            pairs.append((lo, lo + r))

    def sort(lo, hi):
        if hi - lo >= 1:
            mid = lo + (hi - lo) // 2
            sort(lo, mid)
            sort(mid + 1, hi)
            merge(lo, hi, 1)

    sort(0, n - 1)
    return pairs


_SORT16_PAIRS = _batcher_pairs(PEER_TOPK)


def _first(a, b):
    return (a[0] > b[0]) | ((a[0] == b[0]) & (a[1] < b[1]))


def _exchange(a, b):
    a_first = _first(a, b)
    hi = (jnp.maximum(a[0], b[0]),) + tuple(jnp.where(a_first, pa, pb) for pa, pb in zip(a[1:], b[1:]))
    lo = (jnp.minimum(a[0], b[0]),) + tuple(jnp.where(a_first, pb, pa) for pa, pb in zip(a[1:], b[1:]))
    return hi, lo


def _better(a, b):
    a_first = _first(a, b)
    return (jnp.maximum(a[0], b[0]),) + tuple(jnp.where(a_first, pa, pb) for pa, pb in zip(a[1:], b[1:]))


def _sort16(items):
    items = list(items)
    for i, j in _SORT16_PAIRS:
        items[i], items[j] = _exchange(items[i], items[j])
    return items


def _merge_top16(a, b):
    k = PEER_TOPK
    c = [_better(a[i], b[k - 1 - i]) for i in range(k)]
    d = k // 2
    while d >= 1:
        for i in range(k):
            if i & d == 0:
                c[i], c[i + d] = _exchange(c[i], c[i + d])
        d //= 2
    return c


def _top16_of_lists(vals, tags):
    k = PEER_TOPK
    best = None
    for g in range(0, len(vals), k):
        group = _sort16([(vals[i]() if callable(vals[i]) else vals[i], tags[i]) for i in range(g, g + k)])
        best = group if best is None else _merge_top16(best, group)
    return [b[0] for b in best], [b[1] for b in best]


ROUTE_TILE = SUBLANES * LANES
ROW_PITCH = PEER_KEYS + SUBLANES


def _peer_route_kernel(x_ref, n_ref, wq_ref, keys_ref,
                       xn_ref, ia_ref, ib_ref, gate_ref,
                       q_ref, z1_ref, z2_ref, za_ref, zb_ref, zg_ref):
    k = PEER_TOPK
    tile = (SUBLANES, LANES)
    xn = (_rms(x_ref[...]) * n_ref[...]).astype(BF16)
    xn_ref[...] = xn
    q = _dot(xn, wq_ref[...]).astype(BF16)
    for hc in range(2 * PEER_HEADS):
        q_ref[hc] = q[:, hc * PEER_HALF:(hc + 1) * PEER_HALF]

    def head(h, carry):
        for c, z_ref in ((0, z1_ref), (1, z2_ref)):
            keys = keys_ref[2 * h + c]
            for j in range(SUBLANES):
                z_ref[j * ROW_PITCH:j * ROW_PITCH + PEER_KEYS, :] = _dot_nt(
                    keys, q_ref[2 * h + c, j * LANES:(j + 1) * LANES, :])

        def top_keys(z_ref):
            load = lambda key: (lambda: z_ref[pl.ds(key, SUBLANES, stride=ROW_PITCH), :])
            return _top16_of_lists([load(key) for key in range(PEER_KEYS)],
                                   [jnp.full(tile, float(key), F32) for key in range(PEER_KEYS)])

        t1, i1 = top_keys(z1_ref)
        t2, i2 = top_keys(z2_ref)
        i1s = [v * float(PEER_KEYS) for v in i1]

        def pair(a, b):
            return (t1[a] + t2[b], jnp.full(tile, float(a * k + b), F32), i1s[a] + i2[b])

        pad = (jnp.full(tile, -jnp.inf, F32), jnp.full(tile, float(k * k), F32), jnp.zeros(tile, F32))
        row = lambda a: [pair(a, b) for b in range(k // (a + 1))]
        first = _merge_top16(row(0), row(1) + [pad] * (k - len(row(1))))
        second = _sort16(row(2) + row(3) + row(4) + row(5) + row(6))
        third_items = row(7) + [pair(a, 0) for a in range(8, k)]
        third = _sort16(third_items + [pad] * (k - len(third_items)))
        best = _merge_top16(_merge_top16(first, second), third)

        e = [jnp.exp(item[0] - best[0][0]) for item in best]
        z = e[0]
        for v in e[1:]:
            z = z + v
        inv = 1.0 / z
        for r in range(k):
            sel = h * k + r
            expert = best[r][2]
            a_idx = jnp.floor(expert * (1.0 / PEER_KEYS))
            za_ref[pl.ds(sel, SUBLANES, stride=ROW_PITCH), :] = a_idx
            zb_ref[pl.ds(sel, SUBLANES, stride=ROW_PITCH), :] = expert - a_idx * float(PEER_KEYS)
            zg_ref[pl.ds(sel, SUBLANES, stride=ROW_PITCH), :] = e[r] * inv
        return carry

    lax.fori_loop(0, PEER_HEADS, head, 0)
    for j in range(SUBLANES):
        rows = slice(j * LANES, (j + 1) * LANES)
        src = slice(j * ROW_PITCH, j * ROW_PITCH + PEER_HEADS * k)
        ia_ref[rows, :] = za_ref[src, :].T
        ib_ref[rows, :] = zb_ref[src, :].T
        gate_ref[rows, :] = zg_ref[src, :].T


def _peer_route(x, n2, wq, keys):
    t = x.shape[0]
    tm = ROUTE_TILE
    sel_spec = pl.BlockSpec((tm, PEER_HEADS * PEER_TOPK), lambda i: (i, 0))
    sel_shape = jax.ShapeDtypeStruct((t, PEER_HEADS * PEER_TOPK), F32)
    z_scratch = pltpu.VMEM((SUBLANES * ROW_PITCH, LANES), F32)
    return pl.pallas_call(
        _peer_route_kernel,
        grid=(pl.cdiv(t, tm),),
        in_specs=[pl.BlockSpec((tm, D_MODEL), lambda i: (i, 0)),
                  _const_spec((1, D_MODEL)),
                  _const_spec(wq.shape),
                  _const_spec(keys.shape)],
        out_specs=(pl.BlockSpec((tm, D_MODEL), lambda i: (i, 0)), sel_spec, sel_spec, sel_spec),
        out_shape=(jax.ShapeDtypeStruct((t, D_MODEL), BF16), sel_shape, sel_shape, sel_shape),
        scratch_shapes=[pltpu.VMEM((2 * PEER_HEADS, tm, PEER_HALF), BF16),
                        z_scratch, z_scratch, z_scratch, z_scratch, z_scratch],
        compiler_params=_params(("arbitrary",)),
        name="peer_route",
    )(x, n2, wq, keys)


def _peer_mlp_kernel(xn_ref, u_hbm, v_hbm, ia_ref, ib_ref, gate_ref, x_ref, fw_ref, *rest,
                     layer, final, n_prompt_tiles, n_sample):
    if final:
        o_ref, os_ref, acc_ref, act_a_ref, act_b_ref, w_ref, g_ref, ubuf, vbuf, sem = rest
    else:
        o_ref, acc_ref, act_a_ref, act_b_ref, w_ref, g_ref, ubuf, vbuf, sem = rest
    tm = xn_ref.shape[0]
    n_tiles = PEER_EXPERTS // EXPERT_TILE
    rows_per_tile = EXPERT_TILE // PEER_KEYS

    def table_copy(hbm, buf, which, tile, slot):
        rows = pl.ds(pl.multiple_of(tile * EXPERT_TILE, EXPERT_TILE), EXPERT_TILE)
        return pltpu.make_async_copy(hbm.at[layer, rows, :], buf.at[slot], sem.at[which, slot])

    u_copy = functools.partial(table_copy, u_hbm, ubuf, 0)
    v_copy = functools.partial(table_copy, v_hbm, vbuf, 1)
    u_copy(0, 0).start()
    v_copy(0, 0).start()
    u_copy(1, 1).start()

    acc_ref[...] = jnp.zeros_like(acc_ref)
    key = lax.broadcasted_iota(jnp.int32, (PEER_KEYS, PEER_HEADS * PEER_TOPK), 0).astype(F32)

    def build(t, carry):
        gate = gate_ref[pl.ds(t, 1), :]
        sel_gate = jnp.where(key == ia_ref[pl.ds(t, 1), :], gate, 0.0)
        hi = sel_gate.astype(BF16)
        lo = (sel_gate - hi.astype(F32)).astype(BF16)
        onehot = jnp.where(key == ib_ref[pl.ds(t, 1), :], 1.0, 0.0).astype(BF16)
        g_ref[pl.ds(pl.multiple_of(t * ROW_PITCH, SUBLANES), PEER_KEYS), :] = _dot_nt(
            jnp.concatenate([hi, lo], axis=1), jnp.concatenate([onehot, onehot], axis=1))
        return carry

    lax.fori_loop(0, tm, build, 0, unroll=BUILD_UNROLL)
    u_copy(0, 0).wait()
    act_a_ref[...] = _dot_nt(xn_ref[...], ubuf[0])

    def step(e, slot, act_ref, act_next_ref):
        @pl.when(e + 2 < n_tiles)
        def _():
            u_copy(e + 2, slot).start()

        @pl.when(e + 1 < n_tiles)
        def _():
            v_copy(e + 1, 1 - slot).start()
            u_copy(e + 1, 1 - slot).wait()

        v_copy(e, slot).wait()
        act_next_ref[...] = _dot_nt(xn_ref[...], ubuf[1 - slot])
        for j in range(rows_per_tile):
            i1 = e * rows_per_tile + j
            cols = slice(j * PEER_KEYS, (j + 1) * PEER_KEYS)
            gate = g_ref[pl.ds(i1, tm, stride=ROW_PITCH), :]
            w_ref[:, cols] = (gate * _gelu(act_ref[:, cols])).astype(BF16)
        acc_ref[...] += _dot(w_ref[...], vbuf[slot])

    def pair(p, carry):
        step(2 * p, 0, act_a_ref, act_b_ref)
        step(2 * p + 1, 1, act_b_ref, act_a_ref)
        return carry

    lax.fori_loop(0, n_tiles // 2, pair, 0)

    y = x_ref[...] + acc_ref[...]
    if not final:
        o_ref[...] = y
        return
    y = _rms(y) * fw_ref[...]
    tile = pl.program_id(0)

    @pl.when(tile < n_prompt_tiles)
    def _():
        o_ref[...] = y

    @pl.when(tile == pl.num_programs(0) - 1)
    def _():
        os_ref[...] = y[tm - n_sample:, :]


def _peer_mlp(xn, u_bf, v_bf, route, x, fw, tm, layer, final, n_prompt):
    t = x.shape[0]
    n_sample = t - n_prompt
    assert n_sample <= tm and n_sample % SUBLANES == 0
    n_prompt_tiles = pl.cdiv(n_prompt, tm)
    n_tiles = PEER_EXPERTS // EXPERT_TILE
    assert n_tiles % 2 == 0 and n_tiles >= 2
    ia, ib, gate = route
    tok_spec = lambda width: pl.BlockSpec((tm, width), lambda i: (i, 0))
    if final:
        out_specs = (pl.BlockSpec((tm, D_MODEL), lambda i: (jnp.minimum(i, n_prompt_tiles - 1), 0)),
                     pl.BlockSpec((n_sample, D_MODEL), lambda i: (0, 0)))
        out_shape = (jax.ShapeDtypeStruct((n_prompt, D_MODEL), F32),
                     jax.ShapeDtypeStruct((n_sample, D_MODEL), F32))
    else:
        out_specs = tok_spec(D_MODEL)
        out_shape = jax.ShapeDtypeStruct((t, D_MODEL), F32)
    table_buffer = pltpu.VMEM((2, EXPERT_TILE, D_MODEL), BF16)
    return pl.pallas_call(
        functools.partial(_peer_mlp_kernel, layer=layer, final=final, n_prompt_tiles=n_prompt_tiles,
                          n_sample=n_sample),
        grid=(t // tm,),
        in_specs=[tok_spec(D_MODEL),
                  pl.BlockSpec(memory_space=pl.ANY),
                  pl.BlockSpec(memory_space=pl.ANY),
                  tok_spec(PEER_HEADS * PEER_TOPK), tok_spec(PEER_HEADS * PEER_TOPK),
                  tok_spec(PEER_HEADS * PEER_TOPK),
                  tok_spec(D_MODEL),
                  _const_spec((1, D_MODEL))],
        out_specs=out_specs,
        out_shape=out_shape,
        scratch_shapes=[pltpu.VMEM((tm, D_MODEL), F32),
                        pltpu.VMEM((tm, EXPERT_TILE), F32),
                        pltpu.VMEM((tm, EXPERT_TILE), F32),
                        pltpu.VMEM((tm, EXPERT_TILE), BF16),
                        pltpu.VMEM((tm * ROW_PITCH, PEER_KEYS), F32),
                        table_buffer, table_buffer,
                        pltpu.SemaphoreType.DMA((2, 2))],
        compiler_params=_params(("arbitrary",)),
        name="peer_mlp",
    )(xn, u_bf, v_bf, ia, ib, gate, x, fw)


def _token_tile(t):
    for tm in (512, 384, 256, 128):
        if t % tm == 0:
            return tm
    raise ValueError(f"token count {t} must be a multiple of {LANES}")


def _rope_tables(pos):
    half = RET_HEAD_DIM // 2
    inv = ROPE_BASE ** (-jnp.arange(half, dtype=F32) / half)
    ang = pos.astype(F32)[:, None] * inv[None, :]
    cos = jnp.cos(ang)
    sin = jnp.sin(ang)
    return jnp.concatenate([cos, cos], axis=1), jnp.concatenate([-sin, sin], axis=1)


def _block_diag(w):
    eye = jnp.eye(LRU_BLOCKS, dtype=w.dtype)
    return jnp.einsum('hij,hg->higj', w, eye).reshape(LRU_WIDTH, LRU_WIDTH)


def kernel(x_prompt, x_sample, state_ssd, state_ssd_conv, state_ret, state_lru, state_lru_conv, norm1_w, w_in, ssd_conv_w, ssd_conv_b, ssd_dt_bias, ssd_a_log, ssd_d, ssd_norm_w, lru_conv_w, lru_conv_b, lru_wa, lru_ba, lru_wi, lru_bi, lru_lambda, w_out, norm2_w, peer_wq, peer_keys, peer_u, peer_v, final_norm_w):
    batch, seq, _ = x_prompt.shape
    n_sample, dec_seq, _ = x_sample.shape
    depth = w_in.shape[0]
    assert dec_seq == 1 and seq % SCAN_CHUNK == 0 and n_sample % SAMPLE_TILE == 0
    n_prompt = batch * seq
    t = n_prompt + n_sample
    tm = _token_tile(t)

    x = jnp.concatenate([x_prompt.reshape(n_prompt, D_MODEL), x_sample.reshape(n_sample, D_MODEL)], axis=0)
    cos_p, sin_p = _rope_tables(jnp.arange(seq))
    cos_s, sin_s = _rope_tables(PAST_LEN + jnp.arange(dec_seq))
    row = lambda a: a.reshape(1, -1).astype(F32)
    per_channel = lambda a: jnp.repeat(a.astype(F32), SSD_HEAD_DIM).reshape(1, SSD_WIDTH)
    per_head = lambda a: jnp.pad(a.astype(F32), (0, LANES - SSD_HEADS)).reshape(1, LANES)
    expand = (jnp.arange(LANES)[:, None] == jnp.arange(SSD_WIDTH)[None, :] // SSD_HEAD_DIM).astype(BF16)
    off = SSD_WIDTH + SSD_CONV_DIM

    carried = (jnp.swapaxes(state_ssd, -1, -2), state_ssd_conv, state_ret, state_lru, state_lru_conv)
    u_bf = peer_u.astype(BF16)
    v_bf = peer_v.astype(BF16)
    new_p, st_s = [], None
    for i in range(depth):
        wi_ = w_in[i]
        dt_cols = wi_[:, off:off + SSD_HEADS]
        w_in_p = jnp.concatenate(
            [wi_[:, :off],
             jnp.pad(dt_cols, ((0, 0), (0, LANES - SSD_HEADS))),
             wi_[:, off + SSD_HEADS:]], axis=1).astype(BF16)
        wts = (ssd_conv_w[i], row(ssd_conv_b[i]), per_head(ssd_dt_bias[i]), per_head(ssd_a_log[i]),
               expand, per_channel(ssd_d[i]),
               row(ssd_norm_w[i]), lru_conv_w[i], row(lru_conv_b[i]),
               _block_diag(lru_wa[i]).astype(BF16), row(lru_ba[i]),
               _block_diag(lru_wi[i]).astype(BF16), row(lru_bi[i]), row(lru_lambda[i]),
               w_out[i].astype(BF16))

        proj = _inproj(x, row(norm1_w[i]), w_in_p, tm)
        x_mixed, *st_p = _prompt_mixer(proj, x, cos_p, sin_p, wts, batch, seq)
        st_p[3] = st_p[3].reshape(batch, LRU_WIDTH)
        x, *st_s = _sample_mixer(proj, x, x_mixed, cos_s, sin_s, carried, st_s, wts, i, n_prompt, n_sample)
        new_p.append(st_p)

        keys_bf = peer_keys[i].reshape(PEER_HEADS * 2, PEER_KEYS, PEER_HALF).astype(BF16)
        xn, *route = _peer_route(x, row(norm2_w[i]), peer_wq[i].astype(BF16), keys_bf)
        x = _peer_mlp(xn, u_bf, v_bf, route, x, row(final_norm_w), tm, i, i == depth - 1, n_prompt)

    y_prompt, y_sample = x
    stk = lambda states, j: jnp.stack([s[j] for s in states])
    return (y_prompt.reshape(batch, seq, D_MODEL), y_sample.reshape(n_sample, dec_seq, D_MODEL),
            stk(new_p, 0), stk(new_p, 1), stk(new_p, 2), stk(new_p, 3), stk(new_p, 4),
            jnp.swapaxes(st_s[0], -1, -2), *st_s[1:])
```

```python
import functools

import numpy as np
import jax
import jax.numpy as jnp
from jax import lax
from jax.experimental import pallas as pl
from jax.experimental.pallas import tpu as pltpu

F32 = jnp.float32
BF16 = jnp.bfloat16

D_MODEL = 1024
PAST_LEN = 16384
D_MIX = 2 * D_MODEL
CONV_WIDTH = 4
NORM_EPS = 1e-6
SSD_WIDTH = D_MIX // 2
SSD_HEAD_DIM = 64
SSD_HEADS = SSD_WIDTH // SSD_HEAD_DIM
SSD_GROUPS = 2
SSD_HPG = SSD_HEADS // SSD_GROUPS
SSD_STATE = 128
SSD_CONV_DIM = SSD_WIDTH + 2 * SSD_GROUPS * SSD_STATE
RET_WIDTH = D_MIX // 4
RET_HEADS = 4
RET_HEAD_DIM = RET_WIDTH // RET_HEADS
ROPE_BASE = 10000.0
LRU_WIDTH = D_MIX // 4
LRU_BLOCKS = 8
LRU_BLOCK_DIM = LRU_WIDTH // LRU_BLOCKS
LRU_C = 8.0
PEER_KEYS = 128
PEER_EXPERTS = PEER_KEYS * PEER_KEYS
PEER_HEADS = 8
PEER_TOPK = 16
PEER_QUERY_DIM = 256
PEER_HALF = PEER_QUERY_DIM // 2

LANES = 128
SUBLANES = 8
VMEM_LIMIT = 56 * 1024 * 1024

COL_Z = 0
COL_XBC = COL_Z + SSD_WIDTH
COL_DT16 = COL_XBC + SSD_CONV_DIM
COL_Q = COL_DT16 + LANES
COL_K = COL_Q + RET_WIDTH
COL_V = COL_K + RET_WIDTH
COL_G = COL_V + RET_WIDTH
COL_GATE = COL_G + RET_WIDTH
COL_XR = COL_GATE + LRU_WIDTH
PROJ_WIDTH = COL_XR + LRU_WIDTH

SCAN_CHUNK = 256
SAMPLE_TILE = SUBLANES
SAMPLE_HEAD_BLOCK = 8
EXPERT_TILE = 1024
BUILD_UNROLL = 64
LOG_GAMMA = [float(np.log(1.0 - 2.0 ** (-5.0 - h))) for h in range(RET_HEADS)]

_NT = (((1,), (1,)), ((), ()))


def _rms(x):
    return x * lax.rsqrt(jnp.mean(x * x, axis=-1, keepdims=True) + NORM_EPS)


def _softplus(x):
    return jnp.maximum(x, 0.0) + jnp.log1p(jnp.exp(-jnp.abs(x)))


def _silu(x):
    return x * jax.nn.sigmoid(x)


def _gelu(x):
    return 0.5 * x * (1.0 + lax.erf(x * np.float32(np.sqrt(0.5))))


def _neg_expm1(x):
    return -jnp.tanh(0.5 * x) * (jnp.exp(x) + 1.0)


def _dot(a, b):
    return jnp.dot(a, b, preferred_element_type=F32)


def _dot_nt(a, b):
    return lax.dot_general(a, b, _NT, preferred_element_type=F32)


def _split3(x):
    hi = x.astype(BF16)
    rest = x - hi.astype(F32)
    mid = rest.astype(BF16)
    return hi, mid, (rest - mid.astype(F32)).astype(BF16)


def _dot_select_cols(x, select_bf):
    return sum(_dot(part, select_bf) for part in _split3(x))


def _dot_prefix_rows(mask_bf, x):
    return sum(_dot(mask_bf, part) for part in _split3(x))


def _params(semantics):
    return pltpu.CompilerParams(dimension_semantics=semantics, vmem_limit_bytes=VMEM_LIMIT)


def _const_spec(shape):
    zeros = (0,) * len(shape)
    return pl.BlockSpec(shape, lambda *_: zeros)


def _inproj_kernel(x_ref, n_ref, w_ref, o_ref):
    h = (_rms(x_ref[...]) * n_ref[...]).astype(BF16)
    step = 8 * LANES
    for j in range(0, PROJ_WIDTH, step):
        hi = min(j + step, PROJ_WIDTH)
        o_ref[:, j:hi] = _dot(h, w_ref[:, j:hi])


def _inproj(x, n1, w_in_p, tm):
    t = x.shape[0]
    return pl.pallas_call(
        _inproj_kernel,
        grid=(t // tm,),
        in_specs=[pl.BlockSpec((tm, D_MODEL), lambda i: (i, 0)),
                  _const_spec((1, D_MODEL)),
                  pl.BlockSpec((D_MODEL, PROJ_WIDTH), lambda i: (0, 0), pipeline_mode=pl.Buffered(1))],
        out_specs=pl.BlockSpec((tm, PROJ_WIDTH), lambda i: (i, 0)),
        out_shape=jax.ShapeDtypeStruct((t, PROJ_WIDTH), F32),
        compiler_params=_params(("arbitrary",)),
        name="inproj",
    )(x, n1, w_in_p)


def _prompt_mixer_kernel(proj_ref, x_ref, cos_ref, sin_ref, scw_ref, scb_ref, dtb16_ref, alog16_ref,
                         expand_ref, dskx_ref, snw_ref, lcw_ref, lcb_ref, wa_ref, ba_ref,
                         wi_ref, bi_ref, lam_ref, wout_ref,
                         xo_ref, ssd_ref, sbuf_ref, ret_ref, lruh_ref, lbuf_ref,
                         xps_ref, xpl_ref, mix_ref):
    c = SCAN_CHUNK
    step = pl.program_id(1)
    pad = SUBLANES
    keep = CONV_WIDTH - 1

    @pl.when(step == 0)
    def _():
        ssd_ref[...] = jnp.zeros_like(ssd_ref)
        ret_ref[...] = jnp.zeros_like(ret_ref)
        lruh_ref[...] = jnp.zeros_like(lruh_ref)
        xps_ref[0:pad, :] = jnp.zeros((pad, SSD_CONV_DIM), F32)
        xpl_ref[0:pad, :] = jnp.zeros((pad, LRU_WIDTH), F32)

    def conv(xp_ref, x_new, w_ref, b_ref, buf_out_ref):
        xp_ref[pad:pad + c, :] = x_new
        y = b_ref[...] + xp_ref[pad - keep:pad - keep + c, :] * w_ref[0:1, :]
        for tap in range(1, CONV_WIDTH):
            y = y + xp_ref[pad - keep + tap:pad - keep + tap + c, :] * w_ref[tap:tap + 1, :]
        tail = xp_ref[pad + c - keep:pad + c, :]
        xp_ref[pad - keep:pad, :] = tail
        buf_out_ref[0] = tail
        return y

    row = lax.broadcasted_iota(jnp.int32, (c, c), 0)
    col = lax.broadcasted_iota(jnp.int32, (c, c), 1)
    causal = row >= col
    tri = causal.astype(F32).astype(BF16)
    rowf = lax.broadcasted_iota(jnp.int32, (c, LANES), 0).astype(F32)

    xbc = _silu(conv(xps_ref, proj_ref[:, COL_XBC:COL_XBC + SSD_CONV_DIM], scw_ref, scb_ref, sbuf_ref))
    xs = xbc[:, :SSD_WIDTH]
    bm = xbc[:, SSD_WIDTH:SSD_WIDTH + SSD_GROUPS * SSD_STATE]
    cm = xbc[:, SSD_WIDTH + SSD_GROUPS * SSD_STATE:]
    dt16 = _softplus(proj_ref[:, COL_DT16:COL_DT16 + LANES] + dtb16_ref[...])
    cum16 = _dot_prefix_rows(tri, -dt16 * jnp.exp(alog16_ref[...]))
    cum16_t = cum16.T
    wide = _dot_select_cols(jnp.concatenate([dt16, cum16], axis=0), expand_ref[...])
    dtx = wide[:c]
    cumx = wide[c:]
    v = xs * dtx
    v_bf = v.astype(BF16)
    cum_last = cumx[c - 1:c, :]
    vte_bf = (v * jnp.exp(cum_last - cumx)).astype(BF16)
    ecum = jnp.exp(cumx)
    chunk_decay = jnp.exp(cum_last)
    for g in range(SSD_GROUPS):
        bmg = bm[:, g * SSD_STATE:(g + 1) * SSD_STATE]
        cmg_bf = cm[:, g * SSD_STATE:(g + 1) * SSD_STATE].astype(BF16)
        scores = _dot_nt(cmg_bf, bmg.astype(BF16))
        bmg_t_bf = bmg.T.astype(BF16)
        for r in range(SSD_HPG):
            h = g * SSD_HPG + r
            hs = slice(h * SSD_HEAD_DIM, (h + 1) * SSD_HEAD_DIM)
            seg = cum16[:, h:h + 1] - cum16_t[h:h + 1, :]
            decay = jnp.exp(jnp.where(causal, seg, -jnp.inf))
            y_intra = _dot((scores * decay).astype(BF16), v_bf[:, hs])
            s_in = ssd_ref[0, h]
            y_inter = _dot(cmg_bf, s_in.astype(BF16)) * ecum[:, hs]
            ssd_ref[0, h] = s_in * chunk_decay[:, hs] + _dot(bmg_t_bf, vte_bf[:, hs])
            mix_ref[:, hs] = y_intra + y_inter
    y = mix_ref[:, :SSD_WIDTH] + dskx_ref[...] * xs
    y = y * _silu(proj_ref[:, COL_Z:COL_Z + SSD_WIDTH])
    mix_ref[:, :SSD_WIDTH] = _rms(y) * snw_ref[...]

    cosf = cos_ref[...]
    sinf = sin_ref[...]
    rowcol = (row - col).astype(F32)
    for h in range(RET_HEADS):
        lg = LOG_GAMMA[h]
        hs = slice(h * RET_HEAD_DIM, (h + 1) * RET_HEAD_DIM)
        qh = proj_ref[:, COL_Q + h * RET_HEAD_DIM:COL_Q + (h + 1) * RET_HEAD_DIM]
        kh = proj_ref[:, COL_K + h * RET_HEAD_DIM:COL_K + (h + 1) * RET_HEAD_DIM]
        vh = proj_ref[:, COL_V + h * RET_HEAD_DIM:COL_V + (h + 1) * RET_HEAD_DIM]
        gh = proj_ref[:, COL_G + h * RET_HEAD_DIM:COL_G + (h + 1) * RET_HEAD_DIM]
        qh = qh * cosf + pltpu.roll(qh, RET_HEAD_DIM // 2, 1) * sinf
        kh = (kh * cosf + pltpu.roll(kh, RET_HEAD_DIM // 2, 1) * sinf) * (RET_HEAD_DIM ** -0.5)
        qh_bf = qh.astype(BF16)
        scores = _dot_nt(qh_bf, kh.astype(BF16))
        decay = jnp.exp(jnp.where(causal, rowcol * lg, -jnp.inf))
        y_intra = _dot((scores * decay).astype(BF16), vh.astype(BF16))
        s_in = ret_ref[0, h]
        y_inter = _dot(qh_bf, s_in.astype(BF16)) * jnp.exp((rowf + 1.0) * lg)
        v_to_end = (vh * jnp.exp((c - 1.0 - rowf) * lg)).astype(BF16)
        ret_ref[0, h] = s_in * float(np.exp(c * lg)) + _dot(kh.T.astype(BF16), v_to_end)
        mix_ref[:, SSD_WIDTH + h * RET_HEAD_DIM:SSD_WIDTH + (h + 1) * RET_HEAD_DIM] = (
            _rms(y_intra + y_inter) * _silu(gh))

    xc = conv(xpl_ref, proj_ref[:, COL_XR:COL_XR + LRU_WIDTH], lcw_ref, lcb_ref, lbuf_ref)
    xc_bf = xc.astype(BF16)
    r_gate = jax.nn.sigmoid(_dot(xc_bf, wa_ref[...]) + ba_ref[...])
    i_gate = jax.nn.sigmoid(_dot(xc_bf, wi_ref[...]) + bi_ref[...])
    log_a = -LRU_C * r_gate * _softplus(-lam_ref[...])
    a_run = jnp.exp(log_a)
    b_run = jnp.sqrt(_neg_expm1(2.0 * log_a)) * (i_gate * xc)
    rowl = lax.broadcasted_iota(jnp.int32, (c, LRU_WIDTH), 0)
    shift = 1
    while shift < c:
        valid = rowl >= shift
        a_prev = pltpu.roll(a_run, shift, 0)
        b_prev = pltpu.roll(b_run, shift, 0)
        b_run = jnp.where(valid, a_run * b_prev + b_run, b_run)
        a_run = jnp.where(valid, a_run * a_prev, a_run)
        shift *= 2
    h_all = b_run + a_run * lruh_ref[0]
    lruh_ref[0] = h_all[c - 1:c, :]
    mix_ref[:, SSD_WIDTH + RET_WIDTH:] = h_all * _gelu(proj_ref[:, COL_GATE:COL_GATE + LRU_WIDTH])

    xo_ref[...] = x_ref[...] + _dot(mix_ref[...].astype(BF16), wout_ref[...])


def _prompt_mixer(proj, x_all, cosf, sinf, wts, batch, seq):
    c = SCAN_CHUNK
    nc = seq // c
    t = x_all.shape[0]
    row_spec = lambda width: pl.BlockSpec((c, width), lambda b, s: (b * nc + s, 0))
    state_spec = lambda shape: pl.BlockSpec((1,) + shape, lambda b, s: (b,) + (0,) * len(shape))
    in_specs = [row_spec(PROJ_WIDTH), row_spec(D_MODEL),
                pl.BlockSpec((c, LANES), lambda b, s: (s, 0)),
                pl.BlockSpec((c, LANES), lambda b, s: (s, 0))]
    in_specs += [_const_spec(w.shape) for w in wts]
    out_shapes = (jax.ShapeDtypeStruct((t, D_MODEL), F32),
                  jax.ShapeDtypeStruct((batch, SSD_HEADS, SSD_STATE, SSD_HEAD_DIM), F32),
                  jax.ShapeDtypeStruct((batch, CONV_WIDTH - 1, SSD_CONV_DIM), F32),
                  jax.ShapeDtypeStruct((batch, RET_HEADS, RET_HEAD_DIM, RET_HEAD_DIM), F32),
                  jax.ShapeDtypeStruct((batch, 1, LRU_WIDTH), F32),
                  jax.ShapeDtypeStruct((batch, CONV_WIDTH - 1, LRU_WIDTH), F32))
    out_specs = (row_spec(D_MODEL),
                 state_spec((SSD_HEADS, SSD_STATE, SSD_HEAD_DIM)),
                 state_spec((CONV_WIDTH - 1, SSD_CONV_DIM)),
                 state_spec((RET_HEADS, RET_HEAD_DIM, RET_HEAD_DIM)),
                 state_spec((1, LRU_WIDTH)),
                 state_spec((CONV_WIDTH - 1, LRU_WIDTH)))
    return pl.pallas_call(
        _prompt_mixer_kernel,
        grid=(batch, nc),
        in_specs=in_specs,
        out_specs=out_specs,
        out_shape=out_shapes,
        scratch_shapes=[pltpu.VMEM((SUBLANES + c, SSD_CONV_DIM), F32),
                        pltpu.VMEM((SUBLANES + c, LRU_WIDTH), F32),
                        pltpu.VMEM((c, D_MIX), F32)],
        compiler_params=_params(("arbitrary", "arbitrary")),
        name="prompt_mixer",
    )(proj, x_all, cosf, sinf, *wts)


def _sample_mixer_kernel(proj_ref, x_ref, cos_ref, sin_ref, ssd_in_ref, sbuf_in_ref, ret_in_ref,
                         lruh_in_ref, lbuf_in_ref, scw_ref, scb_ref, dtb16_ref, alog16_ref,
                         expand_ref, dskx_ref, snw_ref, lcw_ref, lcb_ref, wa_ref, ba_ref,
                         wi_ref, bi_ref, lam_ref, wout_ref, *rest, n_aliased):
    (xo_ref, ssd_ref, sbuf_ref, ret_ref, lruh_ref, lbuf_ref,
     vh_ref, ah_ref, bm_ref, cm_ref, yh_ref, xs_ref, mix_ref) = rest[n_aliased:]
    bt = SAMPLE_TILE
    hb = pl.program_id(1)
    nhb = pl.num_programs(1)
    keep = CONV_WIDTH - 1

    def conv(x_new, buf_in_ref, w_ref, b_ref, buf_out_ref):
        y = b_ref[...] + x_new * w_ref[keep:keep + 1, :]
        for tap in range(keep):
            y = y + buf_in_ref[:, tap, :] * w_ref[tap:tap + 1, :]
        for tap in range(keep - 1):
            buf_out_ref[:, tap, :] = buf_in_ref[:, tap + 1, :]
        buf_out_ref[:, keep - 1, :] = x_new
        return y

    @pl.when(hb == 0)
    def _():
        xbc = _silu(conv(proj_ref[:, COL_XBC:COL_XBC + SSD_CONV_DIM], sbuf_in_ref, scw_ref, scb_ref, sbuf_ref))
        xs = xbc[:, :SSD_WIDTH]
        xs_ref[...] = xs
        dt16 = _softplus(proj_ref[:, COL_DT16:COL_DT16 + LANES] + dtb16_ref[...])
        a16 = jnp.exp(-dt16 * jnp.exp(alog16_ref[...]))
        wide = _dot_select_cols(jnp.concatenate([dt16, a16], axis=0), expand_ref[...])
        dtx = wide[:bt]
        a = wide[bt:]
        v = xs * dtx
        for h in range(SSD_HEADS):
            hs = slice(h * SSD_HEAD_DIM, (h + 1) * SSD_HEAD_DIM)
            vh_ref[h] = v[:, hs].T
            ah_ref[h] = a[:, hs].T
        for g in range(SSD_GROUPS):
            bm_ref[g] = xbc[:, SSD_WIDTH + g * SSD_STATE:SSD_WIDTH + (g + 1) * SSD_STATE]
            off = SSD_WIDTH + SSD_GROUPS * SSD_STATE
            cm_ref[g] = xbc[:, off + g * SSD_STATE:off + (g + 1) * SSD_STATE]

    g = (hb * SAMPLE_HEAD_BLOCK) // SSD_HPG
    bm = bm_ref[g]
    cm = cm_ref[g]
    cm_bf = cm.astype(BF16)
    lane = lax.broadcasted_iota(jnp.int32, (SSD_HEAD_DIM, bt), 1)
    for r in range(SAMPLE_HEAD_BLOCK):
        h = hb * SAMPLE_HEAD_BLOCK + r
        vh_t = vh_ref[h]
        ah_t = ah_ref[h]
        y_t = jnp.zeros((SSD_HEAD_DIM, bt), F32)
        for b in range(bt):
            s_new = ssd_in_ref[b, r] * ah_t[:, b:b + 1] + vh_t[:, b:b + 1] * bm[b:b + 1, :]
            ssd_ref[b, r] = s_new
            y_all = _dot_nt(s_new.astype(BF16), cm_bf)
            y_t = jnp.where(lane == b, y_all, y_t)
        yh_ref[h] = y_t.T

    @pl.when(hb == nhb - 1)
    def _():
        for h in range(SSD_HEADS):
            mix_ref[:, h * SSD_HEAD_DIM:(h + 1) * SSD_HEAD_DIM] = yh_ref[h]
        y = mix_ref[:, :SSD_WIDTH] + dskx_ref[...] * xs_ref[...]
        y = y * _silu(proj_ref[:, COL_Z:COL_Z + SSD_WIDTH])
        mix_ref[:, :SSD_WIDTH] = _rms(y) * snw_ref[...]

        cosf = cos_ref[...]
        sinf = sin_ref[...]
        for h in range(RET_HEADS):
            lg = LOG_GAMMA[h]
            qh = proj_ref[:, COL_Q + h * RET_HEAD_DIM:COL_Q + (h + 1) * RET_HEAD_DIM]
            kh = proj_ref[:, COL_K + h * RET_HEAD_DIM:COL_K + (h + 1) * RET_HEAD_DIM]
            vh = proj_ref[:, COL_V + h * RET_HEAD_DIM:COL_V + (h + 1) * RET_HEAD_DIM]
            gh = proj_ref[:, COL_G + h * RET_HEAD_DIM:COL_G + (h + 1) * RET_HEAD_DIM]
            qh = qh * cosf + pltpu.roll(qh, RET_HEAD_DIM // 2, 1) * sinf
            kh = (kh * cosf + pltpu.roll(kh, RET_HEAD_DIM // 2, 1) * sinf) * (RET_HEAD_DIM ** -0.5)
            qt = qh.T
            kt = kh.T
            hs = slice(SSD_WIDTH + h * RET_HEAD_DIM, SSD_WIDTH + (h + 1) * RET_HEAD_DIM)
            for b in range(bt):
                s_new = ret_in_ref[b, h] * float(np.exp(lg)) + kt[:, b:b + 1] * vh[b:b + 1, :]
                ret_ref[b, h] = s_new
                mix_ref[b:b + 1, hs] = jnp.sum(qt[:, b:b + 1] * s_new, axis=0, keepdims=True)
            mix_ref[:, hs] = _rms(mix_ref[:, hs]) * _silu(gh)

        xc = conv(proj_ref[:, COL_XR:COL_XR + LRU_WIDTH], lbuf_in_ref, lcw_ref, lcb_ref, lbuf_ref)
        xc_bf = xc.astype(BF16)
        r_gate = jax.nn.sigmoid(_dot(xc_bf, wa_ref[...]) + ba_ref[...])
        i_gate = jax.nn.sigmoid(_dot(xc_bf, wi_ref[...]) + bi_ref[...])
        log_a = -LRU_C * r_gate * _softplus(-lam_ref[...])
        h_new = (jnp.exp(log_a) * lruh_in_ref[...]
                 + jnp.sqrt(_neg_expm1(2.0 * log_a)) * (i_gate * xc))
        lruh_ref[...] = h_new
        mix_ref[:, SSD_WIDTH + RET_WIDTH:] = h_new * _gelu(proj_ref[:, COL_GATE:COL_GATE + LRU_WIDTH])

        xo_ref[...] = x_ref[...] + _dot(mix_ref[...].astype(BF16), wout_ref[...])


def _sample_mixer(proj, x_all, x_mixed, cosf, sinf, states, new_states, wts, layer, n_prompt, n_sample):
    bt = SAMPLE_TILE
    hbk = SAMPLE_HEAD_BLOCK
    base = n_prompt // bt
    t = x_all.shape[0]
    row_spec = lambda width: pl.BlockSpec((bt, width), lambda i, j: (base + i, 0))
    ssd_spec = pl.BlockSpec((None, bt, hbk, SSD_HEAD_DIM, SSD_STATE), lambda i, j: (layer, i, j, 0, 0))
    sbuf_spec = pl.BlockSpec((None, bt, CONV_WIDTH - 1, SSD_CONV_DIM), lambda i, j: (layer, i, 0, 0))
    ret_spec = pl.BlockSpec((None, bt, RET_HEADS, RET_HEAD_DIM, RET_HEAD_DIM), lambda i, j: (layer, i, 0, 0, 0))
    lruh_spec = pl.BlockSpec((None, bt, LRU_WIDTH), lambda i, j: (layer, i, 0))
    lbuf_spec = pl.BlockSpec((None, bt, CONV_WIDTH - 1, LRU_WIDTH), lambda i, j: (layer, i, 0, 0))
    state_specs = [ssd_spec, sbuf_spec, ret_spec, lruh_spec, lbuf_spec]
    aliased = (x_mixed,) + (tuple(new_states) if new_states is not None else ())
    in_specs = [row_spec(PROJ_WIDTH), row_spec(D_MODEL),
                _const_spec((1, LANES)), _const_spec((1, LANES))] + state_specs
    in_specs += [_const_spec(w.shape) for w in wts]
    n_in = len(in_specs)
    in_specs += [pl.BlockSpec(memory_space=pl.ANY)] * len(aliased)
    out_shapes = (jax.ShapeDtypeStruct((t, D_MODEL), F32),) + tuple(
        jax.ShapeDtypeStruct(s.shape, F32) for s in states)
    out_specs = (row_spec(D_MODEL),) + tuple(state_specs)
    return pl.pallas_call(
        functools.partial(_sample_mixer_kernel, n_aliased=len(aliased)),
        grid=(n_sample // bt, SSD_HEADS // hbk),
        in_specs=in_specs,
        out_specs=out_specs,
        out_shape=out_shapes,
        scratch_shapes=[pltpu.VMEM((SSD_HEADS, SSD_HEAD_DIM, bt), F32),
                        pltpu.VMEM((SSD_HEADS, SSD_HEAD_DIM, bt), F32),
                        pltpu.VMEM((SSD_GROUPS, bt, SSD_STATE), F32),
                        pltpu.VMEM((SSD_GROUPS, bt, SSD_STATE), F32),
                        pltpu.VMEM((SSD_HEADS, bt, SSD_HEAD_DIM), F32),
                        pltpu.VMEM((bt, SSD_WIDTH), F32),
                        pltpu.VMEM((bt, D_MIX), F32)],
        input_output_aliases={n_in + k: k for k in range(len(aliased))},
        compiler_params=_params(("arbitrary", "arbitrary")),
        name="sample_mixer",
    )(proj, x_all, cosf, sinf, *states, *wts, *aliased)


def _batcher_pairs(n):
    pairs = []

    def merge(lo, hi, r):
        step = r * 2
        if step < hi - lo:
            merge(lo, hi, step)
            merge(lo + r, hi, step)
            pairs.extend((i, i + r) for i in range(lo + r, hi - r, step))
        else:
            pairs.append((lo, lo + r))

    def sort(lo, hi):
        if hi - lo >= 1:
            mid = lo + (hi - lo) // 2
            sort(lo, mid)
            sort(mid + 1, hi)
            merge(lo, hi, 1)

    sort(0, n - 1)
    return pairs


_SORT16_PAIRS = _batcher_pairs(PEER_TOPK)


def _first(a, b):
    return (a[0] > b[0]) | ((a[0] == b[0]) & (a[1] < b[1]))


def _exchange(a, b):
    a_first = _first(a, b)
    hi = (jnp.maximum(a[0], b[0]),) + tuple(jnp.where(a_first, pa, pb) for pa, pb in zip(a[1:], b[1:]))
    lo = (jnp.minimum(a[0], b[0]),) + tuple(jnp.where(a_first, pb, pa) for pa, pb in zip(a[1:], b[1:]))
    return hi, lo


def _better(a, b):
    a_first = _first(a, b)
    return (jnp.maximum(a[0], b[0]),) + tuple(jnp.where(a_first, pa, pb) for pa, pb in zip(a[1:], b[1:]))


def _sort16(items):
    items = list(items)
    for i, j in _SORT16_PAIRS:
        items[i], items[j] = _exchange(items[i], items[j])
    return items


def _merge_top16(a, b):
    k = PEER_TOPK
    c = [_better(a[i], b[k - 1 - i]) for i in range(k)]
    d = k // 2
    while d >= 1:
        for i in range(k):
            if i & d == 0:
                c[i], c[i + d] = _exchange(c[i], c[i + d])
        d //= 2
    return c


def _top16_of_lists(vals, tags):
    k = PEER_TOPK
    best = None
    for g in range(0, len(vals), k):
        group = _sort16([(vals[i]() if callable(vals[i]) else vals[i], tags[i]) for i in range(g, g + k)])
        best = group if best is None else _merge_top16(best, group)
    return [b[0] for b in best], [b[1] for b in best]


ROUTE_TILE = SUBLANES * LANES
ROW_PITCH = PEER_KEYS + SUBLANES


def _peer_route_kernel(x_ref, n_ref, wq_ref, keys_ref,
                       xn_ref, ia_ref, ib_ref, gate_ref,
                       q_ref, z1_ref, z2_ref, za_ref, zb_ref, zg_ref):
    k = PEER_TOPK
    tile = (SUBLANES, LANES)
    xn = (_rms(x_ref[...]) * n_ref[...]).astype(BF16)
    xn_ref[...] = xn
    q = _dot(xn, wq_ref[...]).astype(BF16)
    for hc in range(2 * PEER_HEADS):
        q_ref[hc] = q[:, hc * PEER_HALF:(hc + 1) * PEER_HALF]

    def head(h, carry):
        for c, z_ref in ((0, z1_ref), (1, z2_ref)):
            keys = keys_ref[2 * h + c]
            for j in range(SUBLANES):
                z_ref[j * ROW_PITCH:j * ROW_PITCH + PEER_KEYS, :] = _dot_nt(
                    keys, q_ref[2 * h + c, j * LANES:(j + 1) * LANES, :])

        def top_keys(z_ref):
            load = lambda key: (lambda: z_ref[pl.ds(key, SUBLANES, stride=ROW_PITCH), :])
            return _top16_of_lists([load(key) for key in range(PEER_KEYS)],
                                   [jnp.full(tile, float(key), F32) for key in range(PEER_KEYS)])

        t1, i1 = top_keys(z1_ref)
        t2, i2 = top_keys(z2_ref)
        i1s = [v * float(PEER_KEYS) for v in i1]

        def pair(a, b):
            return (t1[a] + t2[b], jnp.full(tile, float(a * k + b), F32), i1s[a] + i2[b])

        pad = (jnp.full(tile, -jnp.inf, F32), jnp.full(tile, float(k * k), F32), jnp.zeros(tile, F32))
        row = lambda a: [pair(a, b) for b in range(k // (a + 1))]
        first = _merge_top16(row(0), row(1) + [pad] * (k - len(row(1))))
        second = _sort16(row(2) + row(3) + row(4) + row(5) + row(6))
        third_items = row(7) + [pair(a, 0) for a in range(8, k)]
        third = _sort16(third_items + [pad] * (k - len(third_items)))
        best = _merge_top16(_merge_top16(first, second), third)

        e = [jnp.exp(item[0] - best[0][0]) for item in best]
        z = e[0]
        for v in e[1:]:
            z = z + v
        inv = 1.0 / z
        for r in range(k):
            sel = h * k + r
            expert = best[r][2]
            a_idx = jnp.floor(expert * (1.0 / PEER_KEYS))
            za_ref[pl.ds(sel, SUBLANES, stride=ROW_PITCH), :] = a_idx
            zb_ref[pl.ds(sel, SUBLANES, stride=ROW_PITCH), :] = expert - a_idx * float(PEER_KEYS)
            zg_ref[pl.ds(sel, SUBLANES, stride=ROW_PITCH), :] = e[r] * inv
        return carry

    lax.fori_loop(0, PEER_HEADS, head, 0)
    for j in range(SUBLANES):
        rows = slice(j * LANES, (j + 1) * LANES)
        src = slice(j * ROW_PITCH, j * ROW_PITCH + PEER_HEADS * k)
        ia_ref[rows, :] = za_ref[src, :].T
        ib_ref[rows, :] = zb_ref[src, :].T
        gate_ref[rows, :] = zg_ref[src, :].T


def _peer_route(x, n2, wq, keys):
    t = x.shape[0]
    tm = ROUTE_TILE
    sel_spec = pl.BlockSpec((tm, PEER_HEADS * PEER_TOPK), lambda i: (i, 0))
    sel_shape = jax.ShapeDtypeStruct((t, PEER_HEADS * PEER_TOPK), F32)
    z_scratch = pltpu.VMEM((SUBLANES * ROW_PITCH, LANES), F32)
    return pl.pallas_call(
        _peer_route_kernel,
        grid=(pl.cdiv(t, tm),),
        in_specs=[pl.BlockSpec((tm, D_MODEL), lambda i: (i, 0)),
                  _const_spec((1, D_MODEL)),
                  _const_spec(wq.shape),
                  _const_spec(keys.shape)],
        out_specs=(pl.BlockSpec((tm, D_MODEL), lambda i: (i, 0)), sel_spec, sel_spec, sel_spec),
        out_shape=(jax.ShapeDtypeStruct((t, D_MODEL), BF16), sel_shape, sel_shape, sel_shape),
        scratch_shapes=[pltpu.VMEM((2 * PEER_HEADS, tm, PEER_HALF), BF16),
                        z_scratch, z_scratch, z_scratch, z_scratch, z_scratch],
        compiler_params=_params(("arbitrary",)),
        name="peer_route",
    )(x, n2, wq, keys)


def _peer_mlp_kernel(xn_ref, u_hbm, v_hbm, ia_ref, ib_ref, gate_ref, x_ref, fw_ref, *rest,
                     layer, final, n_prompt_tiles, n_sample):
    if final:
        o_ref, os_ref, acc_ref, act_a_ref, act_b_ref, w_ref, g_ref, ubuf, vbuf, sem = rest
    else:
        o_ref, acc_ref, act_a_ref, act_b_ref, w_ref, g_ref, ubuf, vbuf, sem = rest
    tm = xn_ref.shape[0]
    n_tiles = PEER_EXPERTS // EXPERT_TILE
    rows_per_tile = EXPERT_TILE // PEER_KEYS

    def table_copy(hbm, buf, which, tile, slot):
        rows = pl.ds(pl.multiple_of(tile * EXPERT_TILE, EXPERT_TILE), EXPERT_TILE)
        return pltpu.make_async_copy(hbm.at[layer, rows, :], buf.at[slot], sem.at[which, slot])

    u_copy = functools.partial(table_copy, u_hbm, ubuf, 0)
    v_copy = functools.partial(table_copy, v_hbm, vbuf, 1)
    u_copy(0, 0).start()
    v_copy(0, 0).start()
    u_copy(1, 1).start()

    acc_ref[...] = jnp.zeros_like(acc_ref)
    key = lax.broadcasted_iota(jnp.int32, (PEER_KEYS, PEER_HEADS * PEER_TOPK), 0).astype(F32)

    def build(t, carry):
        gate = gate_ref[pl.ds(t, 1), :]
        sel_gate = jnp.where(key == ia_ref[pl.ds(t, 1), :], gate, 0.0)
        hi = sel_gate.astype(BF16)
        lo = (sel_gate - hi.astype(F32)).astype(BF16)
        onehot = jnp.where(key == ib_ref[pl.ds(t, 1), :], 1.0, 0.0).astype(BF16)
        g_ref[pl.ds(pl.multiple_of(t * ROW_PITCH, SUBLANES), PEER_KEYS), :] = _dot_nt(
            jnp.concatenate([hi, lo], axis=1), jnp.concatenate([onehot, onehot], axis=1))
        return carry

    lax.fori_loop(0, tm, build, 0, unroll=BUILD_UNROLL)
    u_copy(0, 0).wait()
    act_a_ref[...] = _dot_nt(xn_ref[...], ubuf[0])

    def step(e, slot, act_ref, act_next_ref):
        @pl.when(e + 2 < n_tiles)
        def _():
            u_copy(e + 2, slot).start()

        @pl.when(e + 1 < n_tiles)
        def _():
            v_copy(e + 1, 1 - slot).start()
            u_copy(e + 1, 1 - slot).wait()

        v_copy(e, slot).wait()
        act_next_ref[...] = _dot_nt(xn_ref[...], ubuf[1 - slot])
        for j in range(rows_per_tile):
            i1 = e * rows_per_tile + j
            cols = slice(j * PEER_KEYS, (j + 1) * PEER_KEYS)
            gate = g_ref[pl.ds(i1, tm, stride=ROW_PITCH), :]
            w_ref[:, cols] = (gate * _gelu(act_ref[:, cols])).astype(BF16)
        acc_ref[...] += _dot(w_ref[...], vbuf[slot])

    def pair(p, carry):
        step(2 * p, 0, act_a_ref, act_b_ref)
        step(2 * p + 1, 1, act_b_ref, act_a_ref)
        return carry

    lax.fori_loop(0, n_tiles // 2, pair, 0)

    y = x_ref[...] + acc_ref[...]
    if not final:
        o_ref[...] = y
        return
    y = _rms(y) * fw_ref[...]
    tile = pl.program_id(0)

    @pl.when(tile < n_prompt_tiles)
    def _():
        o_ref[...] = y

    @pl.when(tile == pl.num_programs(0) - 1)
    def _():
        os_ref[...] = y[tm - n_sample:, :]


def _peer_mlp(xn, u_bf, v_bf, route, x, fw, tm, layer, final, n_prompt):
    t = x.shape[0]
    n_sample = t - n_prompt
    assert n_sample <= tm and n_sample % SUBLANES == 0
    n_prompt_tiles = pl.cdiv(n_prompt, tm)
    n_tiles = PEER_EXPERTS // EXPERT_TILE
    assert n_tiles % 2 == 0 and n_tiles >= 2
    ia, ib, gate = route
    tok_spec = lambda width: pl.BlockSpec((tm, width), lambda i: (i, 0))
    if final:
        out_specs = (pl.BlockSpec((tm, D_MODEL), lambda i: (jnp.minimum(i, n_prompt_tiles - 1), 0)),
                     pl.BlockSpec((n_sample, D_MODEL), lambda i: (0, 0)))
        out_shape = (jax.ShapeDtypeStruct((n_prompt, D_MODEL), F32),
                     jax.ShapeDtypeStruct((n_sample, D_MODEL), F32))
    else:
        out_specs = tok_spec(D_MODEL)
        out_shape = jax.ShapeDtypeStruct((t, D_MODEL), F32)
    table_buffer = pltpu.VMEM((2, EXPERT_TILE, D_MODEL), BF16)
    return pl.pallas_call(
        functools.partial(_peer_mlp_kernel, layer=layer, final=final, n_prompt_tiles=n_prompt_tiles,
                          n_sample=n_sample),
        grid=(t // tm,),
        in_specs=[tok_spec(D_MODEL),
                  pl.BlockSpec(memory_space=pl.ANY),
                  pl.BlockSpec(memory_space=pl.ANY),
                  tok_spec(PEER_HEADS * PEER_TOPK), tok_spec(PEER_HEADS * PEER_TOPK),
                  tok_spec(PEER_HEADS * PEER_TOPK),
                  tok_spec(D_MODEL),
                  _const_spec((1, D_MODEL))],
        out_specs=out_specs,
        out_shape=out_shape,
        scratch_shapes=[pltpu.VMEM((tm, D_MODEL), F32),
                        pltpu.VMEM((tm, EXPERT_TILE), F32),
                        pltpu.VMEM((tm, EXPERT_TILE), F32),
                        pltpu.VMEM((tm, EXPERT_TILE), BF16),
                        pltpu.VMEM((tm * ROW_PITCH, PEER_KEYS), F32),
                        table_buffer, table_buffer,
                        pltpu.SemaphoreType.DMA((2, 2))],
        compiler_params=_params(("arbitrary",)),
        name="peer_mlp",
    )(xn, u_bf, v_bf, ia, ib, gate, x, fw)


def _token_tile(t):
    for tm in (512, 384, 256, 128):
        if t % tm == 0:
            return tm
    raise ValueError(f"token count {t} must be a multiple of {LANES}")


def _rope_tables(pos):
    half = RET_HEAD_DIM // 2
    inv = ROPE_BASE ** (-jnp.arange(half, dtype=F32) / half)
    ang = pos.astype(F32)[:, None] * inv[None, :]
    cos = jnp.cos(ang)
    sin = jnp.sin(ang)
    return jnp.concatenate([cos, cos], axis=1), jnp.concatenate([-sin, sin], axis=1)


def _block_diag(w):
    eye = jnp.eye(LRU_BLOCKS, dtype=w.dtype)
    return jnp.einsum('hij,hg->higj', w, eye).reshape(LRU_WIDTH, LRU_WIDTH)


def kernel(x_prompt, x_sample, state_ssd, state_ssd_conv, state_ret, state_lru, state_lru_conv, norm1_w, w_in, ssd_conv_w, ssd_conv_b, ssd_dt_bias, ssd_a_log, ssd_d, ssd_norm_w, lru_conv_w, lru_conv_b, lru_wa, lru_ba, lru_wi, lru_bi, lru_lambda, w_out, norm2_w, peer_wq, peer_keys, peer_u, peer_v, final_norm_w):
    batch, seq, _ = x_prompt.shape
    n_sample, dec_seq, _ = x_sample.shape
    depth = w_in.shape[0]
    assert dec_seq == 1 and seq % SCAN_CHUNK == 0 and n_sample % SAMPLE_TILE == 0
    n_prompt = batch * seq
    t = n_prompt + n_sample
    tm = _token_tile(t)

    x = jnp.concatenate([x_prompt.reshape(n_prompt, D_MODEL), x_sample.reshape(n_sample, D_MODEL)], axis=0)
    cos_p, sin_p = _rope_tables(jnp.arange(seq))
    cos_s, sin_s = _rope_tables(PAST_LEN + jnp.arange(dec_seq))
    row = lambda a: a.reshape(1, -1).astype(F32)
    per_channel = lambda a: jnp.repeat(a.astype(F32), SSD_HEAD_DIM).reshape(1, SSD_WIDTH)
    per_head = lambda a: jnp.pad(a.astype(F32), (0, LANES - SSD_HEADS)).reshape(1, LANES)
    expand = (jnp.arange(LANES)[:, None] == jnp.arange(SSD_WIDTH)[None, :] // SSD_HEAD_DIM).astype(BF16)
    off = SSD_WIDTH + SSD_CONV_DIM

    carried = (jnp.swapaxes(state_ssd, -1, -2), state_ssd_conv, state_ret, state_lru, state_lru_conv)
    u_bf = peer_u.astype(BF16)
    v_bf = peer_v.astype(BF16)
    new_p, st_s = [], None
    for i in range(depth):
        wi_ = w_in[i]
        dt_cols = wi_[:, off:off + SSD_HEADS]
        w_in_p = jnp.concatenate(
            [wi_[:, :off],
             jnp.pad(dt_cols, ((0, 0), (0, LANES - SSD_HEADS))),
             wi_[:, off + SSD_HEADS:]], axis=1).astype(BF16)
        wts = (ssd_conv_w[i], row(ssd_conv_b[i]), per_head(ssd_dt_bias[i]), per_head(ssd_a_log[i]),
               expand, per_channel(ssd_d[i]),
               row(ssd_norm_w[i]), lru_conv_w[i], row(lru_conv_b[i]),
               _block_diag(lru_wa[i]).astype(BF16), row(lru_ba[i]),
               _block_diag(lru_wi[i]).astype(BF16), row(lru_bi[i]), row(lru_lambda[i]),
               w_out[i].astype(BF16))

        proj = _inproj(x, row(norm1_w[i]), w_in_p, tm)
        x_mixed, *st_p = _prompt_mixer(proj, x, cos_p, sin_p, wts, batch, seq)
        st_p[3] = st_p[3].reshape(batch, LRU_WIDTH)
        x, *st_s = _sample_mixer(proj, x, x_mixed, cos_s, sin_s, carried, st_s, wts, i, n_prompt, n_sample)
        new_p.append(st_p)

        keys_bf = peer_keys[i].reshape(PEER_HEADS * 2, PEER_KEYS, PEER_HALF).astype(BF16)
        xn, *route = _peer_route(x, row(norm2_w[i]), peer_wq[i].astype(BF16), keys_bf)
        x = _peer_mlp(xn, u_bf, v_bf, route, x, row(final_norm_w), tm, i, i == depth - 1, n_prompt)

    y_prompt, y_sample = x
    stk = lambda states, j: jnp.stack([s[j] for s in states])
    return (y_prompt.reshape(batch, seq, D_MODEL), y_sample.reshape(n_sample, dec_seq, D_MODEL),
            stk(new_p, 0), stk(new_p, 1), stk(new_p, 2), stk(new_p, 3), stk(new_p, 4),
            jnp.swapaxes(st_s[0], -1, -2), *st_s[1:])
```

```python
import functools

import numpy as np
import jax
import jax.numpy as jnp
from jax import lax
from jax.experimental import pallas as pl
from jax.experimental.pallas import tpu as pltpu

F32 = jnp.float32
BF16 = jnp.bfloat16

D_MODEL = 1024
PAST_LEN = 16384
D_MIX = 2 * D_MODEL
CONV_WIDTH = 4
NORM_EPS = 1e-6
SSD_WIDTH = D_MIX // 2
SSD_HEAD_DIM = 64
SSD_HEADS = SSD_WIDTH // SSD_HEAD_DIM
SSD_GROUPS = 2
SSD_HPG = SSD_HEADS // SSD_GROUPS
SSD_STATE = 128
SSD_CONV_DIM = SSD_WIDTH + 2 * SSD_GROUPS * SSD_STATE
RET_WIDTH = D_MIX // 4
RET_HEADS = 4
RET_HEAD_DIM = RET_WIDTH // RET_HEADS
ROPE_BASE = 10000.0
LRU_WIDTH = D_MIX // 4
LRU_BLOCKS = 8
LRU_BLOCK_DIM = LRU_WIDTH // LRU_BLOCKS
LRU_C = 8.0
PEER_KEYS = 128
PEER_EXPERTS = PEER_KEYS * PEER_KEYS
PEER_HEADS = 8
PEER_TOPK = 16
PEER_QUERY_DIM = 256
PEER_HALF = PEER_QUERY_DIM // 2

LANES = 128
SUBLANES = 8
VMEM_LIMIT = 56 * 1024 * 1024

COL_Z = 0
COL_XBC = COL_Z + SSD_WIDTH
COL_DT16 = COL_XBC + SSD_CONV_DIM
COL_Q = COL_DT16 + LANES
COL_K = COL_Q + RET_WIDTH
COL_V = COL_K + RET_WIDTH
COL_G = COL_V + RET_WIDTH
COL_GATE = COL_G + RET_WIDTH
COL_XR = COL_GATE + LRU_WIDTH
PROJ_WIDTH = COL_XR + LRU_WIDTH

SCAN_CHUNK = 256
SAMPLE_TILE = SUBLANES
SAMPLE_HEAD_BLOCK = 8
EXPERT_TILE = 1024
BUILD_UNROLL = 128
LOG_GAMMA = [float(np.log(1.0 - 2.0 ** (-5.0 - h))) for h in range(RET_HEADS)]

_NT = (((1,), (1,)), ((), ()))


def _rms(x):
    return x * lax.rsqrt(jnp.mean(x * x, axis=-1, keepdims=True) + NORM_EPS)


def _softplus(x):
    return jnp.maximum(x, 0.0) + jnp.log1p(jnp.exp(-jnp.abs(x)))


def _silu(x):
    return x * jax.nn.sigmoid(x)


def _gelu(x):
    return 0.5 * x * (1.0 + lax.erf(x * np.float32(np.sqrt(0.5))))


def _neg_expm1(x):
    return -jnp.tanh(0.5 * x) * (jnp.exp(x) + 1.0)


def _dot(a, b):
    return jnp.dot(a, b, preferred_element_type=F32)


def _dot_nt(a, b):
    return lax.dot_general(a, b, _NT, preferred_element_type=F32)


def _split3(x):
    hi = x.astype(BF16)
    rest = x - hi.astype(F32)
    mid = rest.astype(BF16)
    return hi, mid, (rest - mid.astype(F32)).astype(BF16)


def _dot_select_cols(x, select_bf):
    return sum(_dot(part, select_bf) for part in _split3(x))


def _dot_prefix_rows(mask_bf, x):
    return sum(_dot(mask_bf, part) for part in _split3(x))


def _params(semantics):
    return pltpu.CompilerParams(dimension_semantics=semantics, vmem_limit_bytes=VMEM_LIMIT)


def _const_spec(shape):
    zeros = (0,) * len(shape)
    return pl.BlockSpec(shape, lambda *_: zeros)


def _inproj_kernel(x_ref, n_ref, w_ref, o_ref):
    h = (_rms(x_ref[...]) * n_ref[...]).astype(BF16)
    step = 8 * LANES
    for j in range(0, PROJ_WIDTH, step):
        hi = min(j + step, PROJ_WIDTH)
        o_ref[:, j:hi] = _dot(h, w_ref[:, j:hi])


def _inproj(x, n1, w_in_p, tm):
    t = x.shape[0]
    return pl.pallas_call(
        _inproj_kernel,
        grid=(t // tm,),
        in_specs=[pl.BlockSpec((tm, D_MODEL), lambda i: (i, 0)),
                  _const_spec((1, D_MODEL)),
                  pl.BlockSpec((D_MODEL, PROJ_WIDTH), lambda i: (0, 0), pipeline_mode=pl.Buffered(1))],
        out_specs=pl.BlockSpec((tm, PROJ_WIDTH), lambda i: (i, 0)),
        out_shape=jax.ShapeDtypeStruct((t, PROJ_WIDTH), F32),
        compiler_params=_params(("arbitrary",)),
        name="inproj",
    )(x, n1, w_in_p)


def _prompt_mixer_kernel(proj_ref, x_ref, cos_ref, sin_ref, scw_ref, scb_ref, dtb16_ref, alog16_ref,
                         expand_ref, dskx_ref, snw_ref, lcw_ref, lcb_ref, wa_ref, ba_ref,
                         wi_ref, bi_ref, lam_ref, wout_ref,
                         xo_ref, ssd_ref, sbuf_ref, ret_ref, lruh_ref, lbuf_ref,
                         xps_ref, xpl_ref, mix_ref):
    c = SCAN_CHUNK
    step = pl.program_id(1)
    pad = SUBLANES
    keep = CONV_WIDTH - 1

    @pl.when(step == 0)
    def _():
        ssd_ref[...] = jnp.zeros_like(ssd_ref)
        ret_ref[...] = jnp.zeros_like(ret_ref)
        lruh_ref[...] = jnp.zeros_like(lruh_ref)
        xps_ref[0:pad, :] = jnp.zeros((pad, SSD_CONV_DIM), F32)
        xpl_ref[0:pad, :] = jnp.zeros((pad, LRU_WIDTH), F32)

    def conv(xp_ref, x_new, w_ref, b_ref, buf_out_ref):
        xp_ref[pad:pad + c, :] = x_new
        y = b_ref[...] + xp_ref[pad - keep:pad - keep + c, :] * w_ref[0:1, :]
        for tap in range(1, CONV_WIDTH):
            y = y + xp_ref[pad - keep + tap:pad - keep + tap + c, :] * w_ref[tap:tap + 1, :]
        tail = xp_ref[pad + c - keep:pad + c, :]
        xp_ref[pad - keep:pad, :] = tail
        buf_out_ref[0] = tail
        return y

    row = lax.broadcasted_iota(jnp.int32, (c, c), 0)
    col = lax.broadcasted_iota(jnp.int32, (c, c), 1)
    causal = row >= col
    tri = causal.astype(F32).astype(BF16)
    rowf = lax.broadcasted_iota(jnp.int32, (c, LANES), 0).astype(F32)

    xbc = _silu(conv(xps_ref, proj_ref[:, COL_XBC:COL_XBC + SSD_CONV_DIM], scw_ref, scb_ref, sbuf_ref))
    xs = xbc[:, :SSD_WIDTH]
    bm = xbc[:, SSD_WIDTH:SSD_WIDTH + SSD_GROUPS * SSD_STATE]
    cm = xbc[:, SSD_WIDTH + SSD_GROUPS * SSD_STATE:]
    dt16 = _softplus(proj_ref[:, COL_DT16:COL_DT16 + LANES] + dtb16_ref[...])
    cum16 = _dot_prefix_rows(tri, -dt16 * jnp.exp(alog16_ref[...]))
    cum16_t = cum16.T
    wide = _dot_select_cols(jnp.concatenate([dt16, cum16], axis=0), expand_ref[...])
    dtx = wide[:c]
    cumx = wide[c:]
    v = xs * dtx
    v_bf = v.astype(BF16)
    cum_last = cumx[c - 1:c, :]
    vte_bf = (v * jnp.exp(cum_last - cumx)).astype(BF16)
    ecum = jnp.exp(cumx)
    chunk_decay = jnp.exp(cum_last)
    for g in range(SSD_GROUPS):
        bmg = bm[:, g * SSD_STATE:(g + 1) * SSD_STATE]
        cmg_bf = cm[:, g * SSD_STATE:(g + 1) * SSD_STATE].astype(BF16)
        scores = _dot_nt(cmg_bf, bmg.astype(BF16))
        bmg_t_bf = bmg.T.astype(BF16)
        for r in range(SSD_HPG):
            h = g * SSD_HPG + r
            hs = slice(h * SSD_HEAD_DIM, (h + 1) * SSD_HEAD_DIM)
            seg = cum16[:, h:h + 1] - cum16_t[h:h + 1, :]
            decay = jnp.exp(jnp.where(causal, seg, -jnp.inf))
            y_intra = _dot((scores * decay).astype(BF16), v_bf[:, hs])
            s_in = ssd_ref[0, h]
            y_inter = _dot(cmg_bf, s_in.astype(BF16)) * ecum[:, hs]
            ssd_ref[0, h] = s_in * chunk_decay[:, hs] + _dot(bmg_t_bf, vte_bf[:, hs])
            mix_ref[:, hs] = y_intra + y_inter
    y = mix_ref[:, :SSD_WIDTH] + dskx_ref[...] * xs
    y = y * _silu(proj_ref[:, COL_Z:COL_Z + SSD_WIDTH])
    mix_ref[:, :SSD_WIDTH] = _rms(y) * snw_ref[...]

    cosf = cos_ref[...]
    sinf = sin_ref[...]
    rowcol = (row - col).astype(F32)
    for h in range(RET_HEADS):
        lg = LOG_GAMMA[h]
        hs = slice(h * RET_HEAD_DIM, (h + 1) * RET_HEAD_DIM)
        qh = proj_ref[:, COL_Q + h * RET_HEAD_DIM:COL_Q + (h + 1) * RET_HEAD_DIM]
        kh = proj_ref[:, COL_K + h * RET_HEAD_DIM:COL_K + (h + 1) * RET_HEAD_DIM]
        vh = proj_ref[:, COL_V + h * RET_HEAD_DIM:COL_V + (h + 1) * RET_HEAD_DIM]
        gh = proj_ref[:, COL_G + h * RET_HEAD_DIM:COL_G + (h + 1) * RET_HEAD_DIM]
        qh = qh * cosf + pltpu.roll(qh, RET_HEAD_DIM // 2, 1) * sinf
        kh = (kh * cosf + pltpu.roll(kh, RET_HEAD_DIM // 2, 1) * sinf) * (RET_HEAD_DIM ** -0.5)
        qh_bf = qh.astype(BF16)
        scores = _dot_nt(qh_bf, kh.astype(BF16))
        decay = jnp.exp(jnp.where(causal, rowcol * lg, -jnp.inf))
        y_intra = _dot((scores * decay).astype(BF16), vh.astype(BF16))
        s_in = ret_ref[0, h]
        y_inter = _dot(qh_bf, s_in.astype(BF16)) * jnp.exp((rowf + 1.0) * lg)
        v_to_end = (vh * jnp.exp((c - 1.0 - rowf) * lg)).astype(BF16)
        ret_ref[0, h] = s_in * float(np.exp(c * lg)) + _dot(kh.T.astype(BF16), v_to_end)
        mix_ref[:, SSD_WIDTH + h * RET_HEAD_DIM:SSD_WIDTH + (h + 1) * RET_HEAD_DIM] = (
            _rms(y_intra + y_inter) * _silu(gh))

    xc = conv(xpl_ref, proj_ref[:, COL_XR:COL_XR + LRU_WIDTH], lcw_ref, lcb_ref, lbuf_ref)
    xc_bf = xc.astype(BF16)
    r_gate = jax.nn.sigmoid(_dot(xc_bf, wa_ref[...]) + ba_ref[...])
    i_gate = jax.nn.sigmoid(_dot(xc_bf, wi_ref[...]) + bi_ref[...])
    log_a = -LRU_C * r_gate * _softplus(-lam_ref[...])
    a_run = jnp.exp(log_a)
    b_run = jnp.sqrt(_neg_expm1(2.0 * log_a)) * (i_gate * xc)
    rowl = lax.broadcasted_iota(jnp.int32, (c, LRU_WIDTH), 0)
    shift = 1
    while shift < c:
        valid = rowl >= shift
        a_prev = pltpu.roll(a_run, shift, 0)
        b_prev = pltpu.roll(b_run, shift, 0)
        b_run = jnp.where(valid, a_run * b_prev + b_run, b_run)
        a_run = jnp.where(valid, a_run * a_prev, a_run)
        shift *= 2
    h_all = b_run + a_run * lruh_ref[0]
    lruh_ref[0] = h_all[c - 1:c, :]
    mix_ref[:, SSD_WIDTH + RET_WIDTH:] = h_all * _gelu(proj_ref[:, COL_GATE:COL_GATE + LRU_WIDTH])

    xo_ref[...] = x_ref[...] + _dot(mix_ref[...].astype(BF16), wout_ref[...])


def _prompt_mixer(proj, x_all, cosf, sinf, wts, batch, seq):
    c = SCAN_CHUNK
    nc = seq // c
    t = x_all.shape[0]
    row_spec = lambda width: pl.BlockSpec((c, width), lambda b, s: (b * nc + s, 0))
    state_spec = lambda shape: pl.BlockSpec((1,) + shape, lambda b, s: (b,) + (0,) * len(shape))
    in_specs = [row_spec(PROJ_WIDTH), row_spec(D_MODEL),
                pl.BlockSpec((c, LANES), lambda b, s: (s, 0)),
                pl.BlockSpec((c, LANES), lambda b, s: (s, 0))]
    in_specs += [_const_spec(w.shape) for w in wts]
    out_shapes = (jax.ShapeDtypeStruct((t, D_MODEL), F32),
                  jax.ShapeDtypeStruct((batch, SSD_HEADS, SSD_STATE, SSD_HEAD_DIM), F32),
                  jax.ShapeDtypeStruct((batch, CONV_WIDTH - 1, SSD_CONV_DIM), F32),
                  jax.ShapeDtypeStruct((batch, RET_HEADS, RET_HEAD_DIM, RET_HEAD_DIM), F32),
                  jax.ShapeDtypeStruct((batch, 1, LRU_WIDTH), F32),
                  jax.ShapeDtypeStruct((batch, CONV_WIDTH - 1, LRU_WIDTH), F32))
    out_specs = (row_spec(D_MODEL),
                 state_spec((SSD_HEADS, SSD_STATE, SSD_HEAD_DIM)),
                 state_spec((CONV_WIDTH - 1, SSD_CONV_DIM)),
                 state_spec((RET_HEADS, RET_HEAD_DIM, RET_HEAD_DIM)),
                 state_spec((1, LRU_WIDTH)),
                 state_spec((CONV_WIDTH - 1, LRU_WIDTH)))
    return pl.pallas_call(
        _prompt_mixer_kernel,
        grid=(batch, nc),
        in_specs=in_specs,
        out_specs=out_specs,
        out_shape=out_shapes,
        scratch_shapes=[pltpu.VMEM((SUBLANES + c, SSD_CONV_DIM), F32),
                        pltpu.VMEM((SUBLANES + c, LRU_WIDTH), F32),
                        pltpu.VMEM((c, D_MIX), F32)],
        compiler_params=_params(("arbitrary", "arbitrary")),
        name="prompt_mixer",
    )(proj, x_all, cosf, sinf, *wts)


def _sample_mixer_kernel(proj_ref, x_ref, cos_ref, sin_ref, ssd_in_ref, sbuf_in_ref, ret_in_ref,
                         lruh_in_ref, lbuf_in_ref, scw_ref, scb_ref, dtb16_ref, alog16_ref,
                         expand_ref, dskx_ref, snw_ref, lcw_ref, lcb_ref, wa_ref, ba_ref,
                         wi_ref, bi_ref, lam_ref, wout_ref, *rest, n_aliased):
    (xo_ref, ssd_ref, sbuf_ref, ret_ref, lruh_ref, lbuf_ref,
     vh_ref, ah_ref, bm_ref, cm_ref, yh_ref, xs_ref, mix_ref) = rest[n_aliased:]
    bt = SAMPLE_TILE
    hb = pl.program_id(1)
    nhb = pl.num_programs(1)
    keep = CONV_WIDTH - 1

    def conv(x_new, buf_in_ref, w_ref, b_ref, buf_out_ref):
        y = b_ref[...] + x_new * w_ref[keep:keep + 1, :]
        for tap in range(keep):
            y = y + buf_in_ref[:, tap, :] * w_ref[tap:tap + 1, :]
        for tap in range(keep - 1):
            buf_out_ref[:, tap, :] = buf_in_ref[:, tap + 1, :]
        buf_out_ref[:, keep - 1, :] = x_new
        return y

    @pl.when(hb == 0)
    def _():
        xbc = _silu(conv(proj_ref[:, COL_XBC:COL_XBC + SSD_CONV_DIM], sbuf_in_ref, scw_ref, scb_ref, sbuf_ref))
        xs = xbc[:, :SSD_WIDTH]
        xs_ref[...] = xs
        dt16 = _softplus(proj_ref[:, COL_DT16:COL_DT16 + LANES] + dtb16_ref[...])
        a16 = jnp.exp(-dt16 * jnp.exp(alog16_ref[...]))
        wide = _dot_select_cols(jnp.concatenate([dt16, a16], axis=0), expand_ref[...])
        dtx = wide[:bt]
        a = wide[bt:]
        v = xs * dtx
        for h in range(SSD_HEADS):
            hs = slice(h * SSD_HEAD_DIM, (h + 1) * SSD_HEAD_DIM)
            vh_ref[h] = v[:, hs].T
            ah_ref[h] = a[:, hs].T
        for g in range(SSD_GROUPS):
            bm_ref[g] = xbc[:, SSD_WIDTH + g * SSD_STATE:SSD_WIDTH + (g + 1) * SSD_STATE]
            off = SSD_WIDTH + SSD_GROUPS * SSD_STATE
            cm_ref[g] = xbc[:, off + g * SSD_STATE:off + (g + 1) * SSD_STATE]

    g = (hb * SAMPLE_HEAD_BLOCK) // SSD_HPG
    bm = bm_ref[g]
    cm = cm_ref[g]
    cm_bf = cm.astype(BF16)
    lane = lax.broadcasted_iota(jnp.int32, (SSD_HEAD_DIM, bt), 1)
    for r in range(SAMPLE_HEAD_BLOCK):
        h = hb * SAMPLE_HEAD_BLOCK + r
        vh_t = vh_ref[h]
        ah_t = ah_ref[h]
        y_t = jnp.zeros((SSD_HEAD_DIM, bt), F32)
        for b in range(bt):
            s_new = ssd_in_ref[b, r] * ah_t[:, b:b + 1] + vh_t[:, b:b + 1] * bm[b:b + 1, :]
            ssd_ref[b, r] = s_new
            y_all = _dot_nt(s_new.astype(BF16), cm_bf)
            y_t = jnp.where(lane == b, y_all, y_t)
        yh_ref[h] = y_t.T

    @pl.when(hb == nhb - 1)
    def _():
        for h in range(SSD_HEADS):
            mix_ref[:, h * SSD_HEAD_DIM:(h + 1) * SSD_HEAD_DIM] = yh_ref[h]
        y = mix_ref[:, :SSD_WIDTH] + dskx_ref[...] * xs_ref[...]
        y = y * _silu(proj_ref[:, COL_Z:COL_Z + SSD_WIDTH])
        mix_ref[:, :SSD_WIDTH] = _rms(y) * snw_ref[...]

        cosf = cos_ref[...]
        sinf = sin_ref[...]
        for h in range(RET_HEADS):
            lg = LOG_GAMMA[h]
            qh = proj_ref[:, COL_Q + h * RET_HEAD_DIM:COL_Q + (h + 1) * RET_HEAD_DIM]
            kh = proj_ref[:, COL_K + h * RET_HEAD_DIM:COL_K + (h + 1) * RET_HEAD_DIM]
            vh = proj_ref[:, COL_V + h * RET_HEAD_DIM:COL_V + (h + 1) * RET_HEAD_DIM]
            gh = proj_ref[:, COL_G + h * RET_HEAD_DIM:COL_G + (h + 1) * RET_HEAD_DIM]
            qh = qh * cosf + pltpu.roll(qh, RET_HEAD_DIM // 2, 1) * sinf
            kh = (kh * cosf + pltpu.roll(kh, RET_HEAD_DIM // 2, 1) * sinf) * (RET_HEAD_DIM ** -0.5)
            qt = qh.T
            kt = kh.T
            hs = slice(SSD_WIDTH + h * RET_HEAD_DIM, SSD_WIDTH + (h + 1) * RET_HEAD_DIM)
            for b in range(bt):
                s_new = ret_in_ref[b, h] * float(np.exp(lg)) + kt[:, b:b + 1] * vh[b:b + 1, :]
                ret_ref[b, h] = s_new
                mix_ref[b:b + 1, hs] = jnp.sum(qt[:, b:b + 1] * s_new, axis=0, keepdims=True)
            mix_ref[:, hs] = _rms(mix_ref[:, hs]) * _silu(gh)

        xc = conv(proj_ref[:, COL_XR:COL_XR + LRU_WIDTH], lbuf_in_ref, lcw_ref, lcb_ref, lbuf_ref)
        xc_bf = xc.astype(BF16)
        r_gate = jax.nn.sigmoid(_dot(xc_bf, wa_ref[...]) + ba_ref[...])
        i_gate = jax.nn.sigmoid(_dot(xc_bf, wi_ref[...]) + bi_ref[...])
        log_a = -LRU_C * r_gate * _softplus(-lam_ref[...])
        h_new = (jnp.exp(log_a) * lruh_in_ref[...]
                 + jnp.sqrt(_neg_expm1(2.0 * log_a)) * (i_gate * xc))
        lruh_ref[...] = h_new
        mix_ref[:, SSD_WIDTH + RET_WIDTH:] = h_new * _gelu(proj_ref[:, COL_GATE:COL_GATE + LRU_WIDTH])

        xo_ref[...] = x_ref[...] + _dot(mix_ref[...].astype(BF16), wout_ref[...])


def _sample_mixer(proj, x_all, x_mixed, cosf, sinf, states, new_states, wts, layer, n_prompt, n_sample):
    bt = SAMPLE_TILE
    hbk = SAMPLE_HEAD_BLOCK
    base = n_prompt // bt
    t = x_all.shape[0]
    row_spec = lambda width: pl.BlockSpec((bt, width), lambda i, j: (base + i, 0))
    ssd_spec = pl.BlockSpec((None, bt, hbk, SSD_HEAD_DIM, SSD_STATE), lambda i, j: (layer, i, j, 0, 0))
    sbuf_spec = pl.BlockSpec((None, bt, CONV_WIDTH - 1, SSD_CONV_DIM), lambda i, j: (layer, i, 0, 0))
    ret_spec = pl.BlockSpec((None, bt, RET_HEADS, RET_HEAD_DIM, RET_HEAD_DIM), lambda i, j: (layer, i, 0, 0, 0))
    lruh_spec = pl.BlockSpec((None, bt, LRU_WIDTH), lambda i, j: (layer, i, 0))
    lbuf_spec = pl.BlockSpec((None, bt, CONV_WIDTH - 1, LRU_WIDTH), lambda i, j: (layer, i, 0, 0))
    state_specs = [ssd_spec, sbuf_spec, ret_spec, lruh_spec, lbuf_spec]
    aliased = (x_mixed,) + (tuple(new_states) if new_states is not None else ())
    in_specs = [row_spec(PROJ_WIDTH), row_spec(D_MODEL),
                _const_spec((1, LANES)), _const_spec((1, LANES))] + state_specs
    in_specs += [_const_spec(w.shape) for w in wts]
    n_in = len(in_specs)
    in_specs += [pl.BlockSpec(memory_space=pl.ANY)] * len(aliased)
    out_shapes = (jax.ShapeDtypeStruct((t, D_MODEL), F32),) + tuple(
        jax.ShapeDtypeStruct(s.shape, F32) for s in states)
    out_specs = (row_spec(D_MODEL),) + tuple(state_specs)
    return pl.pallas_call(
        functools.partial(_sample_mixer_kernel, n_aliased=len(aliased)),
        grid=(n_sample // bt, SSD_HEADS // hbk),
        in_specs=in_specs,
        out_specs=out_specs,
        out_shape=out_shapes,
        scratch_shapes=[pltpu.VMEM((SSD_HEADS, SSD_HEAD_DIM, bt), F32),
                        pltpu.VMEM((SSD_HEADS, SSD_HEAD_DIM, bt), F32),
                        pltpu.VMEM((SSD_GROUPS, bt, SSD_STATE), F32),
                        pltpu.VMEM((SSD_GROUPS, bt, SSD_STATE), F32),
                        pltpu.VMEM((SSD_HEADS, bt, SSD_HEAD_DIM), F32),
                        pltpu.VMEM((bt, SSD_WIDTH), F32),
                        pltpu.VMEM((bt, D_MIX), F32)],
        input_output_aliases={n_in + k: k for k in range(len(aliased))},
        compiler_params=_params(("arbitrary", "arbitrary")),
        name="sample_mixer",
    )(proj, x_all, cosf, sinf, *states, *wts, *aliased)


def _batcher_pairs(n):
    pairs = []

    def merge(lo, hi, r):
        step = r * 2
        if step < hi - lo:
            merge(lo, hi, step)
            merge(lo + r, hi, step)
            pairs.extend((i, i + r) for i in range(lo + r, hi - r, step))
        else:
            pairs.append((lo, lo + r))

    def sort(lo, hi):
        if hi - lo >= 1:
            mid = lo + (hi - lo) // 2
            sort(lo, mid)
            sort(mid + 1, hi)
            merge(lo, hi, 1)

    sort(0, n - 1)
    return pairs


_SORT16_PAIRS = _batcher_pairs(PEER_TOPK)


def _first(a, b):
    return (a[0] > b[0]) | ((a[0] == b[0]) & (a[1] < b[1]))


def _exchange(a, b):
    a_first = _first(a, b)
    hi = (jnp.maximum(a[0], b[0]),) + tuple(jnp.where(a_first, pa, pb) for pa, pb in zip(a[1:], b[1:]))
    lo = (jnp.minimum(a[0], b[0]),) + tuple(jnp.where(a_first, pb, pa) for pa, pb in zip(a[1:], b[1:]))
    return hi, lo


def _better(a, b):
    a_first = _first(a, b)
    return (jnp.maximum(a[0], b[0]),) + tuple(jnp.where(a_first, pa, pb) for pa, pb in zip(a[1:], b[1:]))


def _sort16(items):
    items = list(items)
    for i, j in _SORT16_PAIRS:
        items[i], items[j] = _exchange(items[i], items[j])
    return items


def _merge_top16(a, b):
    k = PEER_TOPK
    c = [_better(a[i], b[k - 1 - i]) for i in range(k)]
    d = k // 2
    while d >= 1:
        for i in range(k):
            if i & d == 0:
                c[i], c[i + d] = _exchange(c[i], c[i + d])
        d //= 2
    return c


def _top16_of_lists(vals, tags):
    k = PEER_TOPK
    best = None
    for g in range(0, len(vals), k):
        group = _sort16([(vals[i]() if callable(vals[i]) else vals[i], tags[i]) for i in range(g, g + k)])
        best = group if best is None else _merge_top16(best, group)
    return [b[0] for b in best], [b[1] for b in best]


ROUTE_TILE = SUBLANES * LANES
ROW_PITCH = PEER_KEYS + SUBLANES


def _peer_route_kernel(x_ref, n_ref, wq_ref, keys_ref,
                       xn_ref, ia_ref, ib_ref, gate_ref,
                       q_ref, z1_ref, z2_ref, za_ref, zb_ref, zg_ref):
    k = PEER_TOPK
    tile = (SUBLANES, LANES)
    xn = (_rms(x_ref[...]) * n_ref[...]).astype(BF16)
    xn_ref[...] = xn
    q = _dot(xn, wq_ref[...]).astype(BF16)
    for hc in range(2 * PEER_HEADS):
        q_ref[hc] = q[:, hc * PEER_HALF:(hc + 1) * PEER_HALF]

    def head(h, carry):
        for c, z_ref in ((0, z1_ref), (1, z2_ref)):
            keys = keys_ref[2 * h + c]
            for j in range(SUBLANES):
                z_ref[j * ROW_PITCH:j * ROW_PITCH + PEER_KEYS, :] = _dot_nt(
                    keys, q_ref[2 * h + c, j * LANES:(j + 1) * LANES, :])

        def top_keys(z_ref):
            load = lambda key: (lambda: z_ref[pl.ds(key, SUBLANES, stride=ROW_PITCH), :])
            return _top16_of_lists([load(key) for key in range(PEER_KEYS)],
                                   [jnp.full(tile, float(key), F32) for key in range(PEER_KEYS)])

        t1, i1 = top_keys(z1_ref)
        t2, i2 = top_keys(z2_ref)
        i1s = [v * float(PEER_KEYS) for v in i1]

        def pair(a, b):
            return (t1[a] + t2[b], jnp.full(tile, float(a * k + b), F32), i1s[a] + i2[b])

        pad = (jnp.full(tile, -jnp.inf, F32), jnp.full(tile, float(k * k), F32), jnp.zeros(tile, F32))
        row = lambda a: [pair(a, b) for b in range(k // (a + 1))]
        first = _merge_top16(row(0), row(1) + [pad] * (k - len(row(1))))
        second = _sort16(row(2) + row(3) + row(4) + row(5) + row(6))
        third_items = row(7) + [pair(a, 0) for a in range(8, k)]
        third = _sort16(third_items + [pad] * (k - len(third_items)))
        best = _merge_top16(_merge_top16(first, second), third)

        e = [jnp.exp(item[0] - best[0][0]) for item in best]
        z = e[0]
        for v in e[1:]:
            z = z + v
        inv = 1.0 / z
        for r in range(k):
            sel = h * k + r
            expert = best[r][2]
            a_idx = jnp.floor(expert * (1.0 / PEER_KEYS))
            za_ref[pl.ds(sel, SUBLANES, stride=ROW_PITCH), :] = a_idx
            zb_ref[pl.ds(sel, SUBLANES, stride=ROW_PITCH), :] = expert - a_idx * float(PEER_KEYS)
            zg_ref[pl.ds(sel, SUBLANES, stride=ROW_PITCH), :] = e[r] * inv
        return carry

    lax.fori_loop(0, PEER_HEADS, head, 0)
    for j in range(SUBLANES):
        rows = slice(j * LANES, (j + 1) * LANES)
        src = slice(j * ROW_PITCH, j * ROW_PITCH + PEER_HEADS * k)
        ia_ref[rows, :] = za_ref[src, :].T
        ib_ref[rows, :] = zb_ref[src, :].T
        gate_ref[rows, :] = zg_ref[src, :].T


def _peer_route(x, n2, wq, keys):
    t = x.shape[0]
    tm = ROUTE_TILE
    sel_spec = pl.BlockSpec((tm, PEER_HEADS * PEER_TOPK), lambda i: (i, 0))
    sel_shape = jax.ShapeDtypeStruct((t, PEER_HEADS * PEER_TOPK), F32)
    z_scratch = pltpu.VMEM((SUBLANES * ROW_PITCH, LANES), F32)
    return pl.pallas_call(
        _peer_route_kernel,
        grid=(pl.cdiv(t, tm),),
        in_specs=[pl.BlockSpec((tm, D_MODEL), lambda i: (i, 0)),
                  _const_spec((1, D_MODEL)),
                  _const_spec(wq.shape),
                  _const_spec(keys.shape)],
        out_specs=(pl.BlockSpec((tm, D_MODEL), lambda i: (i, 0)), sel_spec, sel_spec, sel_spec),
        out_shape=(jax.ShapeDtypeStruct((t, D_MODEL), BF16), sel_shape, sel_shape, sel_shape),
        scratch_shapes=[pltpu.VMEM((2 * PEER_HEADS, tm, PEER_HALF), BF16),
                        z_scratch, z_scratch, z_scratch, z_scratch, z_scratch],
        compiler_params=_params(("arbitrary",)),
        name="peer_route",
    )(x, n2, wq, keys)


def _peer_mlp_kernel(xn_ref, u_hbm, v_hbm, ia_ref, ib_ref, gate_ref, x_ref, fw_ref, *rest,
                     layer, final, n_prompt_tiles, n_sample):
    if final:
        o_ref, os_ref, acc_ref, act_a_ref, act_b_ref, w_ref, g_ref, ubuf, vbuf, sem = rest
    else:
        o_ref, acc_ref, act_a_ref, act_b_ref, w_ref, g_ref, ubuf, vbuf, sem = rest
    tm = xn_ref.shape[0]
    n_tiles = PEER_EXPERTS // EXPERT_TILE
    rows_per_tile = EXPERT_TILE // PEER_KEYS

    def table_copy(hbm, buf, which, tile, slot):
        rows = pl.ds(pl.multiple_of(tile * EXPERT_TILE, EXPERT_TILE), EXPERT_TILE)
        return pltpu.make_async_copy(hbm.at[layer, rows, :], buf.at[slot], sem.at[which, slot])

    u_copy = functools.partial(table_copy, u_hbm, ubuf, 0)
    v_copy = functools.partial(table_copy, v_hbm, vbuf, 1)
    u_copy(0, 0).start()
    v_copy(0, 0).start()
    u_copy(1, 1).start()

    acc_ref[...] = jnp.zeros_like(acc_ref)
    key = lax.broadcasted_iota(jnp.int32, (PEER_KEYS, PEER_HEADS * PEER_TOPK), 0).astype(F32)

    def build(t, carry):
        gate = gate_ref[pl.ds(t, 1), :]
        sel_gate = jnp.where(key == ia_ref[pl.ds(t, 1), :], gate, 0.0)
        hi = sel_gate.astype(BF16)
        lo = (sel_gate - hi.astype(F32)).astype(BF16)
        onehot = jnp.where(key == ib_ref[pl.ds(t, 1), :], 1.0, 0.0).astype(BF16)
        g_ref[pl.ds(pl.multiple_of(t * ROW_PITCH, SUBLANES), PEER_KEYS), :] = _dot_nt(
            jnp.concatenate([hi, lo], axis=1), jnp.concatenate([onehot, onehot], axis=1))
        return carry

    lax.fori_loop(0, tm, build, 0, unroll=BUILD_UNROLL)
    u_copy(0, 0).wait()
    act_a_ref[...] = _dot_nt(xn_ref[...], ubuf[0])

    def step(e, slot, act_ref, act_next_ref):
        @pl.when(e + 2 < n_tiles)
        def _():
            u_copy(e + 2, slot).start()

        @pl.when(e + 1 < n_tiles)
        def _():
            v_copy(e + 1, 1 - slot).start()
            u_copy(e + 1, 1 - slot).wait()

        v_copy(e, slot).wait()
        act_next_ref[...] = _dot_nt(xn_ref[...], ubuf[1 - slot])
        for j in range(rows_per_tile):
            i1 = e * rows_per_tile + j
            cols = slice(j * PEER_KEYS, (j + 1) * PEER_KEYS)
            gate = g_ref[pl.ds(i1, tm, stride=ROW_PITCH), :]
            w_ref[:, cols] = (gate * _gelu(act_ref[:, cols])).astype(BF16)
        acc_ref[...] += _dot(w_ref[...], vbuf[slot])

    def pair(p, carry):
        step(2 * p, 0, act_a_ref, act_b_ref)
        step(2 * p + 1, 1, act_b_ref, act_a_ref)
        return carry

    lax.fori_loop(0, n_tiles // 2, pair, 0)

    y = x_ref[...] + acc_ref[...]
    if not final:
        o_ref[...] = y
        return
    y = _rms(y) * fw_ref[...]
    tile = pl.program_id(0)

    @pl.when(tile < n_prompt_tiles)
    def _():
        o_ref[...] = y

    @pl.when(tile == pl.num_programs(0) - 1)
    def _():
        os_ref[...] = y[tm - n_sample:, :]


def _peer_mlp(xn, u_bf, v_bf, route, x, fw, tm, layer, final, n_prompt):
    t = x.shape[0]
    n_sample = t - n_prompt
    assert n_sample <= tm and n_sample % SUBLANES == 0
    n_prompt_tiles = pl.cdiv(n_prompt, tm)
    n_tiles = PEER_EXPERTS // EXPERT_TILE
    assert n_tiles % 2 == 0 and n_tiles >= 2
    ia, ib, gate = route
    tok_spec = lambda width: pl.BlockSpec((tm, width), lambda i: (i, 0))
    if final:
        out_specs = (pl.BlockSpec((tm, D_MODEL), lambda i: (jnp.minimum(i, n_prompt_tiles - 1), 0)),
                     pl.BlockSpec((n_sample, D_MODEL), lambda i: (0, 0)))
        out_shape = (jax.ShapeDtypeStruct((n_prompt, D_MODEL), F32),
                     jax.ShapeDtypeStruct((n_sample, D_MODEL), F32))
    else:
        out_specs = tok_spec(D_MODEL)
        out_shape = jax.ShapeDtypeStruct((t, D_MODEL), F32)
    table_buffer = pltpu.VMEM((2, EXPERT_TILE, D_MODEL), BF16)
    return pl.pallas_call(
        functools.partial(_peer_mlp_kernel, layer=layer, final=final, n_prompt_tiles=n_prompt_tiles,
                          n_sample=n_sample),
        grid=(t // tm,),
        in_specs=[tok_spec(D_MODEL),
                  pl.BlockSpec(memory_space=pl.ANY),
                  pl.BlockSpec(memory_space=pl.ANY),
                  tok_spec(PEER_HEADS * PEER_TOPK), tok_spec(PEER_HEADS * PEER_TOPK),
                  tok_spec(PEER_HEADS * PEER_TOPK),
                  tok_spec(D_MODEL),
                  _const_spec((1, D_MODEL))],
        out_specs=out_specs,
        out_shape=out_shape,
        scratch_shapes=[pltpu.VMEM((tm, D_MODEL), F32),
                        pltpu.VMEM((tm, EXPERT_TILE), F32),
                        pltpu.VMEM((tm, EXPERT_TILE), F32),
                        pltpu.VMEM((tm, EXPERT_TILE), BF16),
                        pltpu.VMEM((tm * ROW_PITCH, PEER_KEYS), F32),
                        table_buffer, table_buffer,
                        pltpu.SemaphoreType.DMA((2, 2))],
        compiler_params=_params(("arbitrary",)),
        name="peer_mlp",
    )(xn, u_bf, v_bf, ia, ib, gate, x, fw)


def _token_tile(t):
    for tm in (512, 384, 256, 128):
        if t % tm == 0:
            return tm
    raise ValueError(f"token count {t} must be a multiple of {LANES}")


def _rope_tables(pos):
    half = RET_HEAD_DIM // 2
    inv = ROPE_BASE ** (-jnp.arange(half, dtype=F32) / half)
    ang = pos.astype(F32)[:, None] * inv[None, :]
    cos = jnp.cos(ang)
    sin = jnp.sin(ang)
    return jnp.concatenate([cos, cos], axis=1), jnp.concatenate([-sin, sin], axis=1)


def _block_diag(w):
    eye = jnp.eye(LRU_BLOCKS, dtype=w.dtype)
    return jnp.einsum('hij,hg->higj', w, eye).reshape(LRU_WIDTH, LRU_WIDTH)


def kernel(x_prompt, x_sample, state_ssd, state_ssd_conv, state_ret, state_lru, state_lru_conv, norm1_w, w_in, ssd_conv_w, ssd_conv_b, ssd_dt_bias, ssd_a_log, ssd_d, ssd_norm_w, lru_conv_w, lru_conv_b, lru_wa, lru_ba, lru_wi, lru_bi, lru_lambda, w_out, norm2_w, peer_wq, peer_keys, peer_u, peer_v, final_norm_w):
    batch, seq, _ = x_prompt.shape
    n_sample, dec_seq, _ = x_sample.shape
    depth = w_in.shape[0]
    assert dec_seq == 1 and seq % SCAN_CHUNK == 0 and n_sample % SAMPLE_TILE == 0
    n_prompt = batch * seq
    t = n_prompt + n_sample
    tm = _token_tile(t)

    x = jnp.concatenate([x_prompt.reshape(n_prompt, D_MODEL), x_sample.reshape(n_sample, D_MODEL)], axis=0)
    cos_p, sin_p = _rope_tables(jnp.arange(seq))
    cos_s, sin_s = _rope_tables(PAST_LEN + jnp.arange(dec_seq))
    row = lambda a: a.reshape(1, -1).astype(F32)
    per_channel = lambda a: jnp.repeat(a.astype(F32), SSD_HEAD_DIM).reshape(1, SSD_WIDTH)
    per_head = lambda a: jnp.pad(a.astype(F32), (0, LANES - SSD_HEADS)).reshape(1, LANES)
    expand = (jnp.arange(LANES)[:, None] == jnp.arange(SSD_WIDTH)[None, :] // SSD_HEAD_DIM).astype(BF16)
    off = SSD_WIDTH + SSD_CONV_DIM

    carried = (jnp.swapaxes(state_ssd, -1, -2), state_ssd_conv, state_ret, state_lru, state_lru_conv)
    u_bf = peer_u.astype(BF16)
    v_bf = peer_v.astype(BF16)
    new_p, st_s = [], None
    for i in range(depth):
        wi_ = w_in[i]
        dt_cols = wi_[:, off:off + SSD_HEADS]
        w_in_p = jnp.concatenate(
            [wi_[:, :off],
             jnp.pad(dt_cols, ((0, 0), (0, LANES - SSD_HEADS))),
             wi_[:, off + SSD_HEADS:]], axis=1).astype(BF16)
        wts = (ssd_conv_w[i], row(ssd_conv_b[i]), per_head(ssd_dt_bias[i]), per_head(ssd_a_log[i]),
               expand, per_channel(ssd_d[i]),
               row(ssd_norm_w[i]), lru_conv_w[i], row(lru_conv_b[i]),
               _block_diag(lru_wa[i]).astype(BF16), row(lru_ba[i]),
               _block_diag(lru_wi[i]).astype(BF16), row(lru_bi[i]), row(lru_lambda[i]),
               w_out[i].astype(BF16))

        proj = _inproj(x, row(norm1_w[i]), w_in_p, tm)
        x_mixed, *st_p = _prompt_mixer(proj, x, cos_p, sin_p, wts, batch, seq)
        st_p[3] = st_p[3].reshape(batch, LRU_WIDTH)
        x, *st_s = _sample_mixer(proj, x, x_mixed, cos_s, sin_s, carried, st_s, wts, i, n_prompt, n_sample)
        new_p.append(st_p)

        keys_bf = peer_keys[i].reshape(PEER_HEADS * 2, PEER_KEYS, PEER_HALF).astype(BF16)
        xn, *route = _peer_route(x, row(norm2_w[i]), peer_wq[i].astype(BF16), keys_bf)
        x = _peer_mlp(xn, u_bf, v_bf, route, x, row(final_norm_w), tm, i, i == depth - 1, n_prompt)

    y_prompt, y_sample = x
    stk = lambda states, j: jnp.stack([s[j] for s in states])
    return (y_prompt.reshape(batch, seq, D_MODEL), y_sample.reshape(n_sample, dec_seq, D_MODEL),
            stk(new_p, 0), stk(new_p, 1), stk(new_p, 2), stk(new_p, 3), stk(new_p, 4),
            jnp.swapaxes(st_s[0], -1, -2), *st_s[1:])
```

```python
import functools

import numpy as np
import jax
import jax.numpy as jnp
from jax import lax
from jax.experimental import pallas as pl
from jax.experimental.pallas import tpu as pltpu

F32 = jnp.float32
BF16 = jnp.bfloat16

D_MODEL = 1024
PAST_LEN = 16384
D_MIX = 2 * D_MODEL
CONV_WIDTH = 4
NORM_EPS = 1e-6
SSD_WIDTH = D_MIX // 2
SSD_HEAD_DIM = 64
SSD_HEADS = SSD_WIDTH // SSD_HEAD_DIM
SSD_GROUPS = 2
SSD_HPG = SSD_HEADS // SSD_GROUPS
SSD_STATE = 128
SSD_CONV_DIM = SSD_WIDTH + 2 * SSD_GROUPS * SSD_STATE
RET_WIDTH = D_MIX // 4
RET_HEADS = 4
RET_HEAD_DIM = RET_WIDTH // RET_HEADS
ROPE_BASE = 10000.0
LRU_WIDTH = D_MIX // 4
LRU_BLOCKS = 8
LRU_BLOCK_DIM = LRU_WIDTH // LRU_BLOCKS
LRU_C = 8.0
PEER_KEYS = 128
PEER_EXPERTS = PEER_KEYS * PEER_KEYS
PEER_HEADS = 8
PEER_TOPK = 16
PEER_QUERY_DIM = 256
PEER_HALF = PEER_QUERY_DIM // 2

LANES = 128
SUBLANES = 8
VMEM_LIMIT = 56 * 1024 * 1024

COL_Z = 0
COL_XBC = COL_Z + SSD_WIDTH
COL_DT16 = COL_XBC + SSD_CONV_DIM
COL_Q = COL_DT16 + LANES
COL_K = COL_Q + RET_WIDTH
COL_V = COL_K + RET_WIDTH
COL_G = COL_V + RET_WIDTH
COL_GATE = COL_G + RET_WIDTH
COL_XR = COL_GATE + LRU_WIDTH
PROJ_WIDTH = COL_XR + LRU_WIDTH

SCAN_CHUNK = 256
SAMPLE_TILE = SUBLANES
SAMPLE_HEAD_BLOCK = 8
EXPERT_TILE = 1024
BUILD_UNROLL = 128
LOG_GAMMA = [float(np.log(1.0 - 2.0 ** (-5.0 - h))) for h in range(RET_HEADS)]

_NT = (((1,), (1,)), ((), ()))


def _rms(x):
    return x * lax.rsqrt(jnp.mean(x * x, axis=-1, keepdims=True) + NORM_EPS)


def _softplus(x):
    return jnp.maximum(x, 0.0) + jnp.log1p(jnp.exp(-jnp.abs(x)))


def _silu(x):
    return x * jax.nn.sigmoid(x)


def _gelu(x):
    return 0.5 * x * (1.0 + lax.erf(x * np.float32(np.sqrt(0.5))))


def _neg_expm1(x):
    return -jnp.tanh(0.5 * x) * (jnp.exp(x) + 1.0)


def _dot(a, b):
    return jnp.dot(a, b, preferred_element_type=F32)


def _dot_nt(a, b):
    return lax.dot_general(a, b, _NT, preferred_element_type=F32)


def _split3(x):
    hi = x.astype(BF16)
    rest = x - hi.astype(F32)
    mid = rest.astype(BF16)
    return hi, mid, (rest - mid.astype(F32)).astype(BF16)


def _dot_select_cols(x, select_bf):
    return sum(_dot(part, select_bf) for part in _split3(x))


def _dot_prefix_rows(mask_bf, x):
    return sum(_dot(mask_bf, part) for part in _split3(x))


def _params(semantics):
    return pltpu.CompilerParams(dimension_semantics=semantics, vmem_limit_bytes=VMEM_LIMIT)


def _const_spec(shape):
    zeros = (0,) * len(shape)
    return pl.BlockSpec(shape, lambda *_: zeros)


def _inproj_kernel(x_ref, n_ref, w_ref, o_ref):
    h = (_rms(x_ref[...]) * n_ref[...]).astype(BF16)
    step = 8 * LANES
    for j in range(0, PROJ_WIDTH, step):
        hi = min(j + step, PROJ_WIDTH)
        o_ref[:, j:hi] = _dot(h, w_ref[:, j:hi])


def _inproj(x, n1, w_in_p, tm):
    t = x.shape[0]
    return pl.pallas_call(
        _inproj_kernel,
        grid=(t // tm,),
        in_specs=[pl.BlockSpec((tm, D_MODEL), lambda i: (i, 0)),
                  _const_spec((1, D_MODEL)),
                  pl.BlockSpec((D_MODEL, PROJ_WIDTH), lambda i: (0, 0), pipeline_mode=pl.Buffered(1))],
        out_specs=pl.BlockSpec((tm, PROJ_WIDTH), lambda i: (i, 0)),
        out_shape=jax.ShapeDtypeStruct((t, PROJ_WIDTH), F32),
        compiler_params=_params(("arbitrary",)),
        name="inproj",
    )(x, n1, w_in_p)


def _prompt_mixer_kernel(proj_ref, x_ref, cos_ref, sin_ref, scw_ref, scb_ref, dtb16_ref, alog16_ref,
                         expand_ref, dskx_ref, snw_ref, lcw_ref, lcb_ref, wa_ref, ba_ref,
                         wi_ref, bi_ref, lam_ref, wout_ref,
                         xo_ref, ssd_ref, sbuf_ref, ret_ref, lruh_ref, lbuf_ref,
                         xps_ref, xpl_ref, mix_ref):
    c = SCAN_CHUNK
    step = pl.program_id(1)
    pad = SUBLANES
    keep = CONV_WIDTH - 1

    @pl.when(step == 0)
    def _():
        ssd_ref[...] = jnp.zeros_like(ssd_ref)
        ret_ref[...] = jnp.zeros_like(ret_ref)
        lruh_ref[...] = jnp.zeros_like(lruh_ref)
        xps_ref[0:pad, :] = jnp.zeros((pad, SSD_CONV_DIM), F32)
        xpl_ref[0:pad, :] = jnp.zeros((pad, LRU_WIDTH), F32)

    def conv(xp_ref, x_new, w_ref, b_ref, buf_out_ref):
        xp_ref[pad:pad + c, :] = x_new
        y = b_ref[...] + xp_ref[pad - keep:pad - keep + c, :] * w_ref[0:1, :]
        for tap in range(1, CONV_WIDTH):
            y = y + xp_ref[pad - keep + tap:pad - keep + tap + c, :] * w_ref[tap:tap + 1, :]
        tail = xp_ref[pad + c - keep:pad + c, :]
        xp_ref[pad - keep:pad, :] = tail
        buf_out_ref[0] = tail
        return y

    row = lax.broadcasted_iota(jnp.int32, (c, c), 0)
    col = lax.broadcasted_iota(jnp.int32, (c, c), 1)
    causal = row >= col
    tri = causal.astype(F32).astype(BF16)
    rowf = lax.broadcasted_iota(jnp.int32, (c, LANES), 0).astype(F32)

    xbc = _silu(conv(xps_ref, proj_ref[:, COL_XBC:COL_XBC + SSD_CONV_DIM], scw_ref, scb_ref, sbuf_ref))
    xs = xbc[:, :SSD_WIDTH]
    bm = xbc[:, SSD_WIDTH:SSD_WIDTH + SSD_GROUPS * SSD_STATE]
    cm = xbc[:, SSD_WIDTH + SSD_GROUPS * SSD_STATE:]
    dt16 = _softplus(proj_ref[:, COL_DT16:COL_DT16 + LANES] + dtb16_ref[...])
    cum16 = _dot_prefix_rows(tri, -dt16 * jnp.exp(alog16_ref[...]))
    cum16_t = cum16.T
    wide = _dot_select_cols(jnp.concatenate([dt16, cum16], axis=0), expand_ref[...])
    dtx = wide[:c]
    cumx = wide[c:]
    v = xs * dtx
    v_bf = v.astype(BF16)
    cum_last = cumx[c - 1:c, :]
    vte_bf = (v * jnp.exp(cum_last - cumx)).astype(BF16)
    ecum = jnp.exp(cumx)
    chunk_decay = jnp.exp(cum_last)
    for g in range(SSD_GROUPS):
        bmg = bm[:, g * SSD_STATE:(g + 1) * SSD_STATE]
        cmg_bf = cm[:, g * SSD_STATE:(g + 1) * SSD_STATE].astype(BF16)
        scores = _dot_nt(cmg_bf, bmg.astype(BF16))
        bmg_t_bf = bmg.T.astype(BF16)
        for r in range(SSD_HPG):
            h = g * SSD_HPG + r
            hs = slice(h * SSD_HEAD_DIM, (h + 1) * SSD_HEAD_DIM)
            seg = cum16[:, h:h + 1] - cum16_t[h:h + 1, :]
            decay = jnp.exp(jnp.where(causal, seg, -jnp.inf))
            y_intra = _dot((scores * decay).astype(BF16), v_bf[:, hs])
            s_in = ssd_ref[0, h]
            y_inter = _dot(cmg_bf, s_in.astype(BF16)) * ecum[:, hs]
            ssd_ref[0, h] = s_in * chunk_decay[:, hs] + _dot(bmg_t_bf, vte_bf[:, hs])
            mix_ref[:, hs] = y_intra + y_inter
    y = mix_ref[:, :SSD_WIDTH] + dskx_ref[...] * xs
    y = y * _silu(proj_ref[:, COL_Z:COL_Z + SSD_WIDTH])
    mix_ref[:, :SSD_WIDTH] = _rms(y) * snw_ref[...]

    cosf = cos_ref[...]
    sinf = sin_ref[...]
    rowcol = (row - col).astype(F32)
    for h in range(RET_HEADS):
        lg = LOG_GAMMA[h]
        hs = slice(h * RET_HEAD_DIM, (h + 1) * RET_HEAD_DIM)
        qh = proj_ref[:, COL_Q + h * RET_HEAD_DIM:COL_Q + (h + 1) * RET_HEAD_DIM]
        kh = proj_ref[:, COL_K + h * RET_HEAD_DIM:COL_K + (h + 1) * RET_HEAD_DIM]
        vh = proj_ref[:, COL_V + h * RET_HEAD_DIM:COL_V + (h + 1) * RET_HEAD_DIM]
        gh = proj_ref[:, COL_G + h * RET_HEAD_DIM:COL_G + (h + 1) * RET_HEAD_DIM]
        qh = qh * cosf + pltpu.roll(qh, RET_HEAD_DIM // 2, 1) * sinf
        kh = (kh * cosf + pltpu.roll(kh, RET_HEAD_DIM // 2, 1) * sinf) * (RET_HEAD_DIM ** -0.5)
        qh_bf = qh.astype(BF16)
        scores = _dot_nt(qh_bf, kh.astype(BF16))
        decay = jnp.exp(jnp.where(causal, rowcol * lg, -jnp.inf))
        y_intra = _dot((scores * decay).astype(BF16), vh.astype(BF16))
        s_in = ret_ref[0, h]
        y_inter = _dot(qh_bf, s_in.astype(BF16)) * jnp.exp((rowf + 1.0) * lg)
        v_to_end = (vh * jnp.exp((c - 1.0 - rowf) * lg)).astype(BF16)
        ret_ref[0, h] = s_in * float(np.exp(c * lg)) + _dot(kh.T.astype(BF16), v_to_end)
        mix_ref[:, SSD_WIDTH + h * RET_HEAD_DIM:SSD_WIDTH + (h + 1) * RET_HEAD_DIM] = (
            _rms(y_intra + y_inter) * _silu(gh))

    xc = conv(xpl_ref, proj_ref[:, COL_XR:COL_XR + LRU_WIDTH], lcw_ref, lcb_ref, lbuf_ref)
    xc_bf = xc.astype(BF16)
    r_gate = jax.nn.sigmoid(_dot(xc_bf, wa_ref[...]) + ba_ref[...])
    i_gate = jax.nn.sigmoid(_dot(xc_bf, wi_ref[...]) + bi_ref[...])
    log_a = -LRU_C * r_gate * _softplus(-lam_ref[...])
    a_run = jnp.exp(log_a)
    b_run = jnp.sqrt(_neg_expm1(2.0 * log_a)) * (i_gate * xc)
    rowl = lax.broadcasted_iota(jnp.int32, (c, LRU_WIDTH), 0)
    shift = 1
    while shift < c:
        valid = rowl >= shift
        a_prev = pltpu.roll(a_run, shift, 0)
        b_prev = pltpu.roll(b_run, shift, 0)
        b_run = jnp.where(valid, a_run * b_prev + b_run, b_run)
        a_run = jnp.where(valid, a_run * a_prev, a_run)
        shift *= 2
    h_all = b_run + a_run * lruh_ref[0]
    lruh_ref[0] = h_all[c - 1:c, :]
    mix_ref[:, SSD_WIDTH + RET_WIDTH:] = h_all * _gelu(proj_ref[:, COL_GATE:COL_GATE + LRU_WIDTH])

    xo_ref[...] = x_ref[...] + _dot(mix_ref[...].astype(BF16), wout_ref[...])


def _prompt_mixer(proj, x_all, cosf, sinf, wts, batch, seq):
    c = SCAN_CHUNK
    nc = seq // c
    t = x_all.shape[0]
    row_spec = lambda width: pl.BlockSpec((c, width), lambda b, s: (b * nc + s, 0))
    state_spec = lambda shape: pl.BlockSpec((1,) + shape, lambda b, s: (b,) + (0,) * len(shape))
    in_specs = [row_spec(PROJ_WIDTH), row_spec(D_MODEL),
                pl.BlockSpec((c, LANES), lambda b, s: (s, 0)),
                pl.BlockSpec((c, LANES), lambda b, s: (s, 0))]
    in_specs += [_const_spec(w.shape) for w in wts]
    out_shapes = (jax.ShapeDtypeStruct((t, D_MODEL), F32),
                  jax.ShapeDtypeStruct((batch, SSD_HEADS, SSD_STATE, SSD_HEAD_DIM), F32),
                  jax.ShapeDtypeStruct((batch, CONV_WIDTH - 1, SSD_CONV_DIM), F32),
                  jax.ShapeDtypeStruct((batch, RET_HEADS, RET_HEAD_DIM, RET_HEAD_DIM), F32),
                  jax.ShapeDtypeStruct((batch, 1, LRU_WIDTH), F32),
                  jax.ShapeDtypeStruct((batch, CONV_WIDTH - 1, LRU_WIDTH), F32))
    out_specs = (row_spec(D_MODEL),
                 state_spec((SSD_HEADS, SSD_STATE, SSD_HEAD_DIM)),
                 state_spec((CONV_WIDTH - 1, SSD_CONV_DIM)),
                 state_spec((RET_HEADS, RET_HEAD_DIM, RET_HEAD_DIM)),
                 state_spec((1, LRU_WIDTH)),
                 state_spec((CONV_WIDTH - 1, LRU_WIDTH)))
    return pl.pallas_call(
        _prompt_mixer_kernel,
        grid=(batch, nc),
        in_specs=in_specs,
        out_specs=out_specs,
        out_shape=out_shapes,
        scratch_shapes=[pltpu.VMEM((SUBLANES + c, SSD_CONV_DIM), F32),
                        pltpu.VMEM((SUBLANES + c, LRU_WIDTH), F32),
                        pltpu.VMEM((c, D_MIX), F32)],
        compiler_params=_params(("arbitrary", "arbitrary")),
        name="prompt_mixer",
    )(proj, x_all, cosf, sinf, *wts)


def _sample_mixer_kernel(proj_ref, x_ref, cos_ref, sin_ref, ssd_in_ref, sbuf_in_ref, ret_in_ref,
                         lruh_in_ref, lbuf_in_ref, scw_ref, scb_ref, dtb16_ref, alog16_ref,
                         expand_ref, dskx_ref, snw_ref, lcw_ref, lcb_ref, wa_ref, ba_ref,
                         wi_ref, bi_ref, lam_ref, wout_ref, *rest, n_aliased):
    (xo_ref, ssd_ref, sbuf_ref, ret_ref, lruh_ref, lbuf_ref,
     vh_ref, ah_ref, bm_ref, cm_ref, yh_ref, xs_ref, mix_ref) = rest[n_aliased:]
    bt = SAMPLE_TILE
    hb = pl.program_id(1)
    nhb = pl.num_programs(1)
    keep = CONV_WIDTH - 1

    def conv(x_new, buf_in_ref, w_ref, b_ref, buf_out_ref):
        y = b_ref[...] + x_new * w_ref[keep:keep + 1, :]
        for tap in range(keep):
            y = y + buf_in_ref[:, tap, :] * w_ref[tap:tap + 1, :]
        for tap in range(keep - 1):
            buf_out_ref[:, tap, :] = buf_in_ref[:, tap + 1, :]
        buf_out_ref[:, keep - 1, :] = x_new
        return y

    @pl.when(hb == 0)
    def _():
        xbc = _silu(conv(proj_ref[:, COL_XBC:COL_XBC + SSD_CONV_DIM], sbuf_in_ref, scw_ref, scb_ref, sbuf_ref))
        xs = xbc[:, :SSD_WIDTH]
        xs_ref[...] = xs
        dt16 = _softplus(proj_ref[:, COL_DT16:COL_DT16 + LANES] + dtb16_ref[...])
        a16 = jnp.exp(-dt16 * jnp.exp(alog16_ref[...]))
        wide = _dot_select_cols(jnp.concatenate([dt16, a16], axis=0), expand_ref[...])
        dtx = wide[:bt]
        a = wide[bt:]
        v = xs * dtx
        for h in range(SSD_HEADS):
            hs = slice(h * SSD_HEAD_DIM, (h + 1) * SSD_HEAD_DIM)
            vh_ref[h] = v[:, hs].T
            ah_ref[h] = a[:, hs].T
        for g in range(SSD_GROUPS):
            bm_ref[g] = xbc[:, SSD_WIDTH + g * SSD_STATE:SSD_WIDTH + (g + 1) * SSD_STATE]
            off = SSD_WIDTH + SSD_GROUPS * SSD_STATE
            cm_ref[g] = xbc[:, off + g * SSD_STATE:off + (g + 1) * SSD_STATE]

    g = (hb * SAMPLE_HEAD_BLOCK) // SSD_HPG
    bm = bm_ref[g]
    cm = cm_ref[g]
    cm_bf = cm.astype(BF16)
    lane = lax.broadcasted_iota(jnp.int32, (SSD_HEAD_DIM, bt), 1)
    for r in range(SAMPLE_HEAD_BLOCK):
        h = hb * SAMPLE_HEAD_BLOCK + r
        vh_t = vh_ref[h]
        ah_t = ah_ref[h]
        y_t = jnp.zeros((SSD_HEAD_DIM, bt), F32)
        for b in range(bt):
            s_new = ssd_in_ref[b, r] * ah_t[:, b:b + 1] + vh_t[:, b:b + 1] * bm[b:b + 1, :]
            ssd_ref[b, r] = s_new
            y_all = _dot_nt(s_new.astype(BF16), cm_bf)
            y_t = jnp.where(lane == b, y_all, y_t)
        yh_ref[h] = y_t.T

    @pl.when(hb == nhb - 1)
    def _():
        for h in range(SSD_HEADS):
            mix_ref[:, h * SSD_HEAD_DIM:(h + 1) * SSD_HEAD_DIM] = yh_ref[h]
        y = mix_ref[:, :SSD_WIDTH] + dskx_ref[...] * xs_ref[...]
        y = y * _silu(proj_ref[:, COL_Z:COL_Z + SSD_WIDTH])
        mix_ref[:, :SSD_WIDTH] = _rms(y) * snw_ref[...]

        cosf = cos_ref[...]
        sinf = sin_ref[...]
        for h in range(RET_HEADS):
            lg = LOG_GAMMA[h]
            qh = proj_ref[:, COL_Q + h * RET_HEAD_DIM:COL_Q + (h + 1) * RET_HEAD_DIM]
            kh = proj_ref[:, COL_K + h * RET_HEAD_DIM:COL_K + (h + 1) * RET_HEAD_DIM]
            vh = proj_ref[:, COL_V + h * RET_HEAD_DIM:COL_V + (h + 1) * RET_HEAD_DIM]
            gh = proj_ref[:, COL_G + h * RET_HEAD_DIM:COL_G + (h + 1) * RET_HEAD_DIM]
            qh = qh * cosf + pltpu.roll(qh, RET_HEAD_DIM // 2, 1) * sinf
            kh = (kh * cosf + pltpu.roll(kh, RET_HEAD_DIM // 2, 1) * sinf) * (RET_HEAD_DIM ** -0.5)
            qt = qh.T
            kt = kh.T
            hs = slice(SSD_WIDTH + h * RET_HEAD_DIM, SSD_WIDTH + (h + 1) * RET_HEAD_DIM)
            for b in range(bt):
                s_new = ret_in_ref[b, h] * float(np.exp(lg)) + kt[:, b:b + 1] * vh[b:b + 1, :]
                ret_ref[b, h] = s_new
                mix_ref[b:b + 1, hs] = jnp.sum(qt[:, b:b + 1] * s_new, axis=0, keepdims=True)
            mix_ref[:, hs] = _rms(mix_ref[:, hs]) * _silu(gh)

        xc = conv(proj_ref[:, COL_XR:COL_XR + LRU_WIDTH], lbuf_in_ref, lcw_ref, lcb_ref, lbuf_ref)
        xc_bf = xc.astype(BF16)
        r_gate = jax.nn.sigmoid(_dot(xc_bf, wa_ref[...]) + ba_ref[...])
        i_gate = jax.nn.sigmoid(_dot(xc_bf, wi_ref[...]) + bi_ref[...])
        log_a = -LRU_C * r_gate * _softplus(-lam_ref[...])
        h_new = (jnp.exp(log_a) * lruh_in_ref[...]
                 + jnp.sqrt(_neg_expm1(2.0 * log_a)) * (i_gate * xc))
        lruh_ref[...] = h_new
        mix_ref[:, SSD_WIDTH + RET_WIDTH:] = h_new * _gelu(proj_ref[:, COL_GATE:COL_GATE + LRU_WIDTH])

        xo_ref[...] = x_ref[...] + _dot(mix_ref[...].astype(BF16), wout_ref[...])


def _sample_mixer(proj, x_all, x_mixed, cosf, sinf, states, new_states, wts, layer, n_prompt, n_sample):
    bt = SAMPLE_TILE
    hbk = SAMPLE_HEAD_BLOCK
    base = n_prompt // bt
    t = x_all.shape[0]
    row_spec = lambda width: pl.BlockSpec((bt, width), lambda i, j: (base + i, 0))
    ssd_spec = pl.BlockSpec((None, bt, hbk, SSD_HEAD_DIM, SSD_STATE), lambda i, j: (layer, i, j, 0, 0))
    sbuf_spec = pl.BlockSpec((None, bt, CONV_WIDTH - 1, SSD_CONV_DIM), lambda i, j: (layer, i, 0, 0))
    ret_spec = pl.BlockSpec((None, bt, RET_HEADS, RET_HEAD_DIM, RET_HEAD_DIM), lambda i, j: (layer, i, 0, 0, 0))
    lruh_spec = pl.BlockSpec((None, bt, LRU_WIDTH), lambda i, j: (layer, i, 0))
    lbuf_spec = pl.BlockSpec((None, bt, CONV_WIDTH - 1, LRU_WIDTH), lambda i, j: (layer, i, 0, 0))
    state_specs = [ssd_spec, sbuf_spec, ret_spec, lruh_spec, lbuf_spec]
    aliased = (x_mixed,) + (tuple(new_states) if new_states is not None else ())
    in_specs = [row_spec(PROJ_WIDTH), row_spec(D_MODEL),
                _const_spec((1, LANES)), _const_spec((1, LANES))] + state_specs
    in_specs += [_const_spec(w.shape) for w in wts]
    n_in = len(in_specs)
    in_specs += [pl.BlockSpec(memory_space=pl.ANY)] * len(aliased)
    out_shapes = (jax.ShapeDtypeStruct((t, D_MODEL), F32),) + tuple(
        jax.ShapeDtypeStruct(s.shape, F32) for s in states)
    out_specs = (row_spec(D_MODEL),) + tuple(state_specs)
    return pl.pallas_call(
        functools.partial(_sample_mixer_kernel, n_aliased=len(aliased)),
        grid=(n_sample // bt, SSD_HEADS // hbk),
        in_specs=in_specs,
        out_specs=out_specs,
        out_shape=out_shapes,
        scratch_shapes=[pltpu.VMEM((SSD_HEADS, SSD_HEAD_DIM, bt), F32),
                        pltpu.VMEM((SSD_HEADS, SSD_HEAD_DIM, bt), F32),
                        pltpu.VMEM((SSD_GROUPS, bt, SSD_STATE), F32),
                        pltpu.VMEM((SSD_GROUPS, bt, SSD_STATE), F32),
                        pltpu.VMEM((SSD_HEADS, bt, SSD_HEAD_DIM), F32),
                        pltpu.VMEM((bt, SSD_WIDTH), F32),
                        pltpu.VMEM((bt, D_MIX), F32)],
        input_output_aliases={n_in + k: k for k in range(len(aliased))},
        compiler_params=_params(("arbitrary", "arbitrary")),
        name="sample_mixer",
    )(proj, x_all, cosf, sinf, *states, *wts, *aliased)


def _batcher_pairs(n):
    pairs = []

    def merge(lo, hi, r):
        step = r * 2
        if step < hi - lo:
            merge(lo, hi, step)
            merge(lo + r, hi, step)
            pairs.extend((i, i + r) for i in range(lo + r, hi - r, step))
        else:
            pairs.append((lo, lo + r))

    def sort(lo, hi):
        if hi - lo >= 1:
            mid = lo + (hi - lo) // 2
            sort(lo, mid)
            sort(mid + 1, hi)
            merge(lo, hi, 1)

    sort(0, n - 1)
    return pairs


_SORT16_PAIRS = _batcher_pairs(PEER_TOPK)


def _first(a, b):
    return (a[0] > b[0]) | ((a[0] == b[0]) & (a[1] < b[1]))


def _exchange(a, b):
    a_first = _first(a, b)
    hi = (jnp.maximum(a[0], b[0]),) + tuple(jnp.where(a_first, pa, pb) for pa, pb in zip(a[1:], b[1:]))
    lo = (jnp.minimum(a[0], b[0]),) + tuple(jnp.where(a_first, pb, pa) for pa, pb in zip(a[1:], b[1:]))
    return hi, lo


def _better(a, b):
    a_first = _first(a, b)
    return (jnp.maximum(a[0], b[0]),) + tuple(jnp.where(a_first, pa, pb) for pa, pb in zip(a[1:], b[1:]))


def _sort16(items):
    items = list(items)
    for i, j in _SORT16_PAIRS:
        items[i], items[j] = _exchange(items[i], items[j])
    return items


def _merge_top16(a, b):
    k = PEER_TOPK
    c = [_better(a[i], b[k - 1 - i]) for i in range(k)]
    d = k // 2
    while d >= 1:
        for i in range(k):
            if i & d == 0:
                c[i], c[i + d] = _exchange(c[i], c[i + d])
        d //= 2
    return c


def _top16_of_lists(vals, tags):
    k = PEER_TOPK
    best = None
    for g in range(0, len(vals), k):
        group = _sort16([(vals[i]() if callable(vals[i]) else vals[i], tags[i]) for i in range(g, g + k)])
        best = group if best is None else _merge_top16(best, group)
    return [b[0] for b in best], [b[1] for b in best]


ROUTE_TILE = SUBLANES * LANES
ROW_PITCH = PEER_KEYS + SUBLANES


def _peer_route_kernel(x_ref, n_ref, wq_ref, keys_ref,
                       xn_ref, ia_ref, ib_ref, gate_ref,
                       q_ref, z1_ref, z2_ref, za_ref, zb_ref, zg_ref, top_ref):
    k = PEER_TOPK
    tile = (SUBLANES, LANES)
    xn = (_rms(x_ref[...]) * n_ref[...]).astype(BF16)
    xn_ref[...] = xn
    q = _dot(xn, wq_ref[...]).astype(BF16)
    for hc in range(2 * PEER_HEADS):
        q_ref[hc] = q[:, hc * PEER_HALF:(hc + 1) * PEER_HALF]

    def head(h, carry):
        for c, z_ref in ((0, z1_ref), (1, z2_ref)):
            keys = keys_ref[2 * h + c]
            for j in range(SUBLANES):
                z_ref[j * ROW_PITCH:j * ROW_PITCH + PEER_KEYS, :] = _dot_nt(
                    keys, q_ref[2 * h + c, j * LANES:(j + 1) * LANES, :])

        def top_keys(z_ref):
            load = lambda key: (lambda: z_ref[pl.ds(key, SUBLANES, stride=ROW_PITCH), :])
            return _top16_of_lists([load(key) for key in range(PEER_KEYS)],
                                   [jnp.full(tile, float(key), F32) for key in range(PEER_KEYS)])

        def park(slot, items):
            for r, item in enumerate(items):
                top_ref[slot, r * SUBLANES:(r + 1) * SUBLANES, :] = item

        def parked(slot, r):
            return top_ref[slot, r * SUBLANES:(r + 1) * SUBLANES, :]

        t1, i1 = top_keys(z1_ref)
        park(0, t1)
        park(1, [v * float(PEER_KEYS) for v in i1])
        t2, i2 = top_keys(z2_ref)
        park(2, t2)
        park(3, i2)

        def pair(a, b):
            return (parked(0, a) + parked(2, b), jnp.full(tile, float(a * k + b), F32),
                    parked(1, a) + parked(3, b))

        pad = (jnp.full(tile, -jnp.inf, F32), jnp.full(tile, float(k * k), F32), jnp.zeros(tile, F32))
        row = lambda a: [pair(a, b) for b in range(k // (a + 1))]
        first = _merge_top16(row(0), row(1) + [pad] * (k - len(row(1))))
        second = _sort16(row(2) + row(3) + row(4) + row(5) + row(6))
        third_items = row(7) + [pair(a, 0) for a in range(8, k)]
        third = _sort16(third_items + [pad] * (k - len(third_items)))
        best = _merge_top16(_merge_top16(first, second), third)

        e = [jnp.exp(item[0] - best[0][0]) for item in best]
        z = e[0]
        for v in e[1:]:
            z = z + v
        inv = 1.0 / z
        for r in range(k):
            sel = h * k + r
            expert = best[r][2]
            a_idx = jnp.floor(expert * (1.0 / PEER_KEYS))
            za_ref[pl.ds(sel, SUBLANES, stride=ROW_PITCH), :] = a_idx
            zb_ref[pl.ds(sel, SUBLANES, stride=ROW_PITCH), :] = expert - a_idx * float(PEER_KEYS)
            zg_ref[pl.ds(sel, SUBLANES, stride=ROW_PITCH), :] = e[r] * inv
        return carry

    lax.fori_loop(0, PEER_HEADS, head, 0)
    for j in range(SUBLANES):
        rows = slice(j * LANES, (j + 1) * LANES)
        src = slice(j * ROW_PITCH, j * ROW_PITCH + PEER_HEADS * k)
        ia_ref[rows, :] = za_ref[src, :].T
        ib_ref[rows, :] = zb_ref[src, :].T
        gate_ref[rows, :] = zg_ref[src, :].T


def _peer_route(x, n2, wq, keys):
    t = x.shape[0]
    tm = ROUTE_TILE
    sel_spec = pl.BlockSpec((tm, PEER_HEADS * PEER_TOPK), lambda i: (i, 0))
    sel_shape = jax.ShapeDtypeStruct((t, PEER_HEADS * PEER_TOPK), F32)
    z_scratch = pltpu.VMEM((SUBLANES * ROW_PITCH, LANES), F32)
    return pl.pallas_call(
        _peer_route_kernel,
        grid=(pl.cdiv(t, tm),),
        in_specs=[pl.BlockSpec((tm, D_MODEL), lambda i: (i, 0)),
                  _const_spec((1, D_MODEL)),
                  _const_spec(wq.shape),
                  _const_spec(keys.shape)],
        out_specs=(pl.BlockSpec((tm, D_MODEL), lambda i: (i, 0)), sel_spec, sel_spec, sel_spec),
        out_shape=(jax.ShapeDtypeStruct((t, D_MODEL), BF16), sel_shape, sel_shape, sel_shape),
        scratch_shapes=[pltpu.VMEM((2 * PEER_HEADS, tm, PEER_HALF), BF16),
                        z_scratch, z_scratch, z_scratch, z_scratch, z_scratch,
                        pltpu.VMEM((4, PEER_TOPK * SUBLANES, LANES), F32)],
        compiler_params=_params(("arbitrary",)),
        name="peer_route",
    )(x, n2, wq, keys)


def _peer_mlp_kernel(xn_ref, u_hbm, v_hbm, ia_ref, ib_ref, gate_ref, x_ref, fw_ref, *rest,
                     layer, final, n_prompt_tiles, n_sample):
    if final:
        o_ref, os_ref, acc_ref, act_a_ref, act_b_ref, w_ref, g_ref, ubuf, vbuf, sem = rest
    else:
        o_ref, acc_ref, act_a_ref, act_b_ref, w_ref, g_ref, ubuf, vbuf, sem = rest
    tm = xn_ref.shape[0]
    n_tiles = PEER_EXPERTS // EXPERT_TILE
    rows_per_tile = EXPERT_TILE // PEER_KEYS

    def table_copy(hbm, buf, which, tile, slot):
        rows = pl.ds(pl.multiple_of(tile * EXPERT_TILE, EXPERT_TILE), EXPERT_TILE)
        return pltpu.make_async_copy(hbm.at[layer, rows, :], buf.at[slot], sem.at[which, slot])

    u_copy = functools.partial(table_copy, u_hbm, ubuf, 0)
    v_copy = functools.partial(table_copy, v_hbm, vbuf, 1)
    u_copy(0, 0).start()
    v_copy(0, 0).start()
    u_copy(1, 1).start()

    acc_ref[...] = jnp.zeros_like(acc_ref)
    key = lax.broadcasted_iota(jnp.int32, (PEER_KEYS, PEER_HEADS * PEER_TOPK), 0).astype(F32)

    def build(t, carry):
        gate = gate_ref[pl.ds(t, 1), :]
        sel_gate = jnp.where(key == ia_ref[pl.ds(t, 1), :], gate, 0.0)
        hi = sel_gate.astype(BF16)
        lo = (sel_gate - hi.astype(F32)).astype(BF16)
        onehot = jnp.where(key == ib_ref[pl.ds(t, 1), :], 1.0, 0.0).astype(BF16)
        g_ref[pl.ds(pl.multiple_of(t * ROW_PITCH, SUBLANES), PEER_KEYS), :] = _dot_nt(
            jnp.concatenate([hi, lo], axis=1), jnp.concatenate([onehot, onehot], axis=1))
        return carry

    lax.fori_loop(0, tm, build, 0, unroll=BUILD_UNROLL)
    u_copy(0, 0).wait()
    act_a_ref[...] = _dot_nt(xn_ref[...], ubuf[0])

    def step(e, slot, act_ref, act_next_ref):
        @pl.when(e + 2 < n_tiles)
        def _():
            u_copy(e + 2, slot).start()

        @pl.when(e + 1 < n_tiles)
        def _():
            v_copy(e + 1, 1 - slot).start()
            u_copy(e + 1, 1 - slot).wait()

        v_copy(e, slot).wait()
        act_next_ref[...] = _dot_nt(xn_ref[...], ubuf[1 - slot])
        for j in range(rows_per_tile):
            i1 = e * rows_per_tile + j
            cols = slice(j * PEER_KEYS, (j + 1) * PEER_KEYS)
            gate = g_ref[pl.ds(i1, tm, stride=ROW_PITCH), :]
            w_ref[:, cols] = (gate * _gelu(act_ref[:, cols])).astype(BF16)
        acc_ref[...] += _dot(w_ref[...], vbuf[slot])

    def pair(p, carry):
        step(2 * p, 0, act_a_ref, act_b_ref)
        step(2 * p + 1, 1, act_b_ref, act_a_ref)
        return carry

    lax.fori_loop(0, n_tiles // 2, pair, 0)

    y = x_ref[...] + acc_ref[...]
    if not final:
        o_ref[...] = y
        return
    y = _rms(y) * fw_ref[...]
    tile = pl.program_id(0)

    @pl.when(tile < n_prompt_tiles)
    def _():
        o_ref[...] = y

    @pl.when(tile == pl.num_programs(0) - 1)
    def _():
        os_ref[...] = y[tm - n_sample:, :]


def _peer_mlp(xn, u_bf, v_bf, route, x, fw, tm, layer, final, n_prompt):
    t = x.shape[0]
    n_sample = t - n_prompt
    assert n_sample <= tm and n_sample % SUBLANES == 0
    n_prompt_tiles = pl.cdiv(n_prompt, tm)
    n_tiles = PEER_EXPERTS // EXPERT_TILE
    assert n_tiles % 2 == 0 and n_tiles >= 2
    ia, ib, gate = route
    tok_spec = lambda width: pl.BlockSpec((tm, width), lambda i: (i, 0))
    if final:
        out_specs = (pl.BlockSpec((tm, D_MODEL), lambda i: (jnp.minimum(i, n_prompt_tiles - 1), 0)),
                     pl.BlockSpec((n_sample, D_MODEL), lambda i: (0, 0)))
        out_shape = (jax.ShapeDtypeStruct((n_prompt, D_MODEL), F32),
                     jax.ShapeDtypeStruct((n_sample, D_MODEL), F32))
    else:
        out_specs = tok_spec(D_MODEL)
        out_shape = jax.ShapeDtypeStruct((t, D_MODEL), F32)
    table_buffer = pltpu.VMEM((2, EXPERT_TILE, D_MODEL), BF16)
    return pl.pallas_call(
        functools.partial(_peer_mlp_kernel, layer=layer, final=final, n_prompt_tiles=n_prompt_tiles,
                          n_sample=n_sample),
        grid=(t // tm,),
        in_specs=[tok_spec(D_MODEL),
                  pl.BlockSpec(memory_space=pl.ANY),
                  pl.BlockSpec(memory_space=pl.ANY),
                  tok_spec(PEER_HEADS * PEER_TOPK), tok_spec(PEER_HEADS * PEER_TOPK),
                  tok_spec(PEER_HEADS * PEER_TOPK),
                  tok_spec(D_MODEL),
                  _const_spec((1, D_MODEL))],
        out_specs=out_specs,
        out_shape=out_shape,
        scratch_shapes=[pltpu.VMEM((tm, D_MODEL), F32),
                        pltpu.VMEM((tm, EXPERT_TILE), F32),
                        pltpu.VMEM((tm, EXPERT_TILE), F32),
                        pltpu.VMEM((tm, EXPERT_TILE), BF16),
                        pltpu.VMEM((tm * ROW_PITCH, PEER_KEYS), F32),
                        table_buffer, table_buffer,
                        pltpu.SemaphoreType.DMA((2, 2))],
        compiler_params=_params(("arbitrary",)),
        name="peer_mlp",
    )(xn, u_bf, v_bf, ia, ib, gate, x, fw)


def _token_tile(t):
    for tm in (512, 384, 256, 128):
        if t % tm == 0:
            return tm
    raise ValueError(f"token count {t} must be a multiple of {LANES}")


def _rope_tables(pos):
    half = RET_HEAD_DIM // 2
    inv = ROPE_BASE ** (-jnp.arange(half, dtype=F32) / half)
    ang = pos.astype(F32)[:, None] * inv[None, :]
    cos = jnp.cos(ang)
    sin = jnp.sin(ang)
    return jnp.concatenate([cos, cos], axis=1), jnp.concatenate([-sin, sin], axis=1)


def _block_diag(w):
    eye = jnp.eye(LRU_BLOCKS, dtype=w.dtype)
    return jnp.einsum('hij,hg->higj', w, eye).reshape(LRU_WIDTH, LRU_WIDTH)


def kernel(x_prompt, x_sample, state_ssd, state_ssd_conv, state_ret, state_lru, state_lru_conv, norm1_w, w_in, ssd_conv_w, ssd_conv_b, ssd_dt_bias, ssd_a_log, ssd_d, ssd_norm_w, lru_conv_w, lru_conv_b, lru_wa, lru_ba, lru_wi, lru_bi, lru_lambda, w_out, norm2_w, peer_wq, peer_keys, peer_u, peer_v, final_norm_w):
    batch, seq, _ = x_prompt.shape
    n_sample, dec_seq, _ = x_sample.shape
    depth = w_in.shape[0]
    assert dec_seq == 1 and seq % SCAN_CHUNK == 0 and n_sample % SAMPLE_TILE == 0
    n_prompt = batch * seq
    t = n_prompt + n_sample
    tm = _token_tile(t)

    x = jnp.concatenate([x_prompt.reshape(n_prompt, D_MODEL), x_sample.reshape(n_sample, D_MODEL)], axis=0)
    cos_p, sin_p = _rope_tables(jnp.arange(seq))
    cos_s, sin_s = _rope_tables(PAST_LEN + jnp.arange(dec_seq))
    row = lambda a: a.reshape(1, -1).astype(F32)
    per_channel = lambda a: jnp.repeat(a.astype(F32), SSD_HEAD_DIM).reshape(1, SSD_WIDTH)
    per_head = lambda a: jnp.pad(a.astype(F32), (0, LANES - SSD_HEADS)).reshape(1, LANES)
    expand = (jnp.arange(LANES)[:, None] == jnp.arange(SSD_WIDTH)[None, :] // SSD_HEAD_DIM).astype(BF16)
    off = SSD_WIDTH + SSD_CONV_DIM

    carried = (jnp.swapaxes(state_ssd, -1, -2), state_ssd_conv, state_ret, state_lru, state_lru_conv)
    u_bf = peer_u.astype(BF16)
    v_bf = peer_v.astype(BF16)
    new_p, st_s = [], None
    for i in range(depth):
        wi_ = w_in[i]
        dt_cols = wi_[:, off:off + SSD_HEADS]
        w_in_p = jnp.concatenate(
            [wi_[:, :off],
             jnp.pad(dt_cols, ((0, 0), (0, LANES - SSD_HEADS))),
             wi_[:, off + SSD_HEADS:]], axis=1).astype(BF16)
        wts = (ssd_conv_w[i], row(ssd_conv_b[i]), per_head(ssd_dt_bias[i]), per_head(ssd_a_log[i]),
               expand, per_channel(ssd_d[i]),
               row(ssd_norm_w[i]), lru_conv_w[i], row(lru_conv_b[i]),
               _block_diag(lru_wa[i]).astype(BF16), row(lru_ba[i]),
               _block_diag(lru_wi[i]).astype(BF16), row(lru_bi[i]), row(lru_lambda[i]),
               w_out[i].astype(BF16))

        proj = _inproj(x, row(norm1_w[i]), w_in_p, tm)
        x_mixed, *st_p = _prompt_mixer(proj, x, cos_p, sin_p, wts, batch, seq)
        st_p[3] = st_p[3].reshape(batch, LRU_WIDTH)
        x, *st_s = _sample_mixer(proj, x, x_mixed, cos_s, sin_s, carried, st_s, wts, i, n_prompt, n_sample)
        new_p.append(st_p)

        keys_bf = peer_keys[i].reshape(PEER_HEADS * 2, PEER_KEYS, PEER_HALF).astype(BF16)
        xn, *route = _peer_route(x, row(norm2_w[i]), peer_wq[i].astype(BF16), keys_bf)
        x = _peer_mlp(xn, u_bf, v_bf, route, x, row(final_norm_w), tm, i, i == depth - 1, n_prompt)

    y_prompt, y_sample = x
    stk = lambda states, j: jnp.stack([s[j] for s in states])
    return (y_prompt.reshape(batch, seq, D_MODEL), y_sample.reshape(n_sample, dec_seq, D_MODEL),
            stk(new_p, 0), stk(new_p, 1), stk(new_p, 2), stk(new_p, 3), stk(new_p, 4),
            jnp.swapaxes(st_s[0], -1, -2), *st_s[1:])
```
